```python
import math
import jax, jax.numpy as jnp
from jax import lax
import numpy as np

D_MODEL = 2048
BATCH = 4
SEQ = 4096
DEPTH = 1
DEC_BATCH = 16
DEC_SEQ = 64
PAST_LEN = 1024

CHUNK = 64
N_HEADS_A = 8
D_QK = 64
D_V = 2 * D_QK
WIDTH_A = N_HEADS_A * D_V
WIDTH_B = D_MODEL // 2
CONV_W = 3
D_FF = 5632
FFN_CONV_W = 3
N_BUCKETS = 32
MAX_DIST = 128
Q_BLOCK = 128
EPS = 1e-6
NEG_INF = -1e30
SPLIT_SIZES = (N_HEADS_A * 2 * D_QK, N_HEADS_A * 2 * D_QK, WIDTH_A, WIDTH_B, WIDTH_B, WIDTH_B, D_MODEL, D_MODEL)
IN_WIDTH = sum(SPLIT_SIZES)

kernel_name = "streaming_diffattn_shortconv_convffn"


def rms_norm(x, g):
    xf = x.astype(jnp.float32)
    y = xf * lax.rsqrt(jnp.mean(xf * xf, axis=-1, keepdims=True) + EPS)
    return (y * g.astype(jnp.float32)).astype(x.dtype)


def lambda_init_for(layer_idx):
    return 0.8 - 0.6 * math.exp(-0.3 * layer_idx)


def rel_bucket(rel):
    nb = N_BUCKETS // 2
    max_exact = nb // 2
    ret = jnp.where(rel > 0, nb, 0)
    n = jnp.abs(rel)
    large = max_exact + (jnp.log(jnp.maximum(n, 1).astype(jnp.float32) / max_exact)
                         / math.log(MAX_DIST / max_exact) * (nb - max_exact)).astype(jnp.int32)
    large = jnp.minimum(large, nb - 1)
    return ret + jnp.where(n < max_exact, n, large)


def position_bias(rel_bias, q_pos, k_pos):
    b = rel_bucket(k_pos[None, :] - q_pos[:, None])
    return jnp.transpose(rel_bias[b], (2, 0, 1)).astype(jnp.float32)


def diff_attn_core(q, k, v, q_pos, k_pos, rel_bias, lam):
    scale = D_QK ** -0.5
    q1, q2 = jnp.split(q, 2, axis=-1)
    k1, k2 = jnp.split(k, 2, axis=-1)
    bias = position_bias(rel_bias, q_pos, k_pos)
    mask = (k_pos[None, :] // CHUNK) <= (q_pos[:, None] // CHUNK)

    def probs(qa, ka):
        s = jnp.einsum('bqhd,bkhd->bhqk', qa, ka, preferred_element_type=jnp.float32) * scale + bias
        s = jnp.where(mask, s, NEG_INF)
        return jax.nn.softmax(s, axis=-1)

    a = probs(q1, k1) - lam * probs(q2, k2)
    return jnp.einsum('bhqk,bkhd->bqhd', a.astype(v.dtype), v)


def diff_attn_prompt(q, k, v, rel_bias, lam):
    B, S, H, _ = q.shape
    nblk = S // Q_BLOCK
    k_pos = jnp.arange(S)
    qb = q.reshape(B, nblk, Q_BLOCK, H, q.shape[-1]).swapaxes(0, 1)
    starts = jnp.arange(nblk) * Q_BLOCK

    def one(args):
        qi, s0 = args
        return diff_attn_core(qi, k, v, s0 + jnp.arange(Q_BLOCK), k_pos, rel_bias, lam)

    o = lax.map(one, (qb, starts))
    return o.swapaxes(0, 1).reshape(B, S, H, v.shape[-1])


def causal_dwconv(x, prev, w):
    T = x.shape[1]
    W = w.shape[0]
    xp = jnp.concatenate([prev.astype(x.dtype), x], axis=1)
    y = xp[:, 0:T] * w[0]
    for j in range(1, W):
        y = y + xp[:, j:j + T] * w[j]
    return y, xp[:, T:]


def split_cols(proj):
    out, off = [], 0
    for s in SPLIT_SIZES:
        out.append(proj[..., off:off + s])
        off += s
    return out


def trunk_layer(x, k_past, v_past, conv_prev, ffn_prev, rel_bias, lambda_init,
                norm1_g, w_in, lambda_q1, lambda_k1, lambda_q2, lambda_k2, subln_g,
                conv_w, w_proj_a, w_proj_b, w_o, norm2_g, w_up, ffn_conv_w, w_down):
    B, T, _ = x.shape
    xn = rms_norm(x, norm1_g)
    q, k, v, b_gate, c_gate, x_in, g_a, g_b = split_cols(xn @ w_in)
    q = q.reshape(B, T, N_HEADS_A, 2 * D_QK)
    k = k.reshape(B, T, N_HEADS_A, 2 * D_QK)
    v = v.reshape(B, T, N_HEADS_A, D_V)
    lam = (jnp.exp(jnp.sum(lambda_q1.astype(jnp.float32) * lambda_k1.astype(jnp.float32)))
           - jnp.exp(jnp.sum(lambda_q2.astype(jnp.float32) * lambda_k2.astype(jnp.float32)))
           + lambda_init)
    if k_past is None:
        o = diff_attn_prompt(q, k, v, rel_bias, lam)
    else:
        P = k_past.shape[1]
        keys = jnp.concatenate([k_past.astype(k.dtype), k], axis=1)
        vals = jnp.concatenate([v_past.astype(v.dtype), v], axis=1)
        o = diff_attn_core(q, keys, vals, P + jnp.arange(T), jnp.arange(P + T), rel_bias, lam)
    o = rms_norm(o, subln_g) * (1.0 - lambda_init)
    o = o.reshape(B, T, WIDTH_A)
    u = c_gate * x_in
    if conv_prev is None:
        conv_prev = jnp.zeros((B, CONV_W - 1, WIDTH_B), u.dtype)
    z, conv_state = causal_dwconv(u, conv_prev, conv_w)
    ob = b_gate * z
    merged = jax.nn.sigmoid(g_a) * (o @ w_proj_a) + jax.nn.sigmoid(g_b) * (ob @ w_proj_b)
    h = x + merged @ w_o
    hn = rms_norm(h, norm2_g)
    up = hn @ w_up
    if ffn_prev is None:
        ffn_prev = jnp.zeros((B, FFN_CONV_W - 1, 2 * D_FF), up.dtype)
    upc, ffn_state = causal_dwconv(up, ffn_prev, ffn_conv_w)
    gate, val = jnp.split(upc, 2, axis=-1)
    h = h + (jax.nn.silu(gate) * val) @ w_down
    return h, k, v, conv_state, ffn_state


def setup_inputs(seed: int = 0) -> dict:
    key = jax.random.key(seed)
    ks = jax.random.split(key, 24)
    f32 = jnp.float32
    nrm = lambda k, shp, s: jax.random.normal(k, shp, f32) * s
    return {
        "x_prompt": nrm(ks[0], (BATCH, SEQ, D_MODEL), 1.0),
        "x_sample": nrm(ks[1], (DEC_BATCH, DEC_SEQ, D_MODEL), 1.0),
        "cache_k": nrm(ks[2], (DEPTH, DEC_BATCH, PAST_LEN, N_HEADS_A, 2 * D_QK), 1.0),
        "cache_v": nrm(ks[3], (DEPTH, DEC_BATCH, PAST_LEN, N_HEADS_A, D_V), 1.0),
        "state_conv_mix": nrm(ks[4], (DEPTH, DEC_BATCH, CONV_W - 1, WIDTH_B), 1.0),
        "state_conv_ffn": nrm(ks[5], (DEPTH, DEC_BATCH, FFN_CONV_W - 1, 2 * D_FF), 1.0),
        "rel_bias": nrm(ks[6], (N_BUCKETS, N_HEADS_A), 0.5),
        "norm1_g": 1.0 + nrm(ks[7], (DEPTH, D_MODEL), 0.02),
        "w_in": nrm(ks[8], (DEPTH, D_MODEL, IN_WIDTH), D_MODEL ** -0.5),
        "lambda_q1": nrm(ks[9], (DEPTH, D_QK), 0.1),
        "lambda_k1": nrm(ks[10], (DEPTH, D_QK), 0.1),
        "lambda_q2": nrm(ks[11], (DEPTH, D_QK), 0.1),
        "lambda_k2": nrm(ks[12], (DEPTH, D_QK), 0.1),
        "subln_g": 1.0 + nrm(ks[13], (DEPTH, D_V), 0.02),
        "conv_w": nrm(ks[14], (DEPTH, CONV_W, WIDTH_B), CONV_W ** -0.5),
        "w_proj_a": nrm(ks[15], (DEPTH, WIDTH_A, D_MODEL), WIDTH_A ** -0.5),
        "w_proj_b": nrm(ks[16], (DEPTH, WIDTH_B, D_MODEL), WIDTH_B ** -0.5),
        "w_o": nrm(ks[17], (DEPTH, D_MODEL, D_MODEL), D_MODEL ** -0.5),
        "norm2_g": 1.0 + nrm(ks[18], (DEPTH, D_MODEL), 0.02),
        "w_up": nrm(ks[19], (DEPTH, D_MODEL, 2 * D_FF), D_MODEL ** -0.5),
        "ffn_conv_w": nrm(ks[20], (DEPTH, FFN_CONV_W, 2 * D_FF), FFN_CONV_W ** -0.5),
        "w_down": nrm(ks[21], (DEPTH, D_FF, D_MODEL), D_FF ** -0.5),
        "final_g": 1.0 + nrm(ks[22], (D_MODEL,), 0.02),
    }


def reference(x_prompt, x_sample, cache_k, cache_v, state_conv_mix, state_conv_ffn, rel_bias,
              norm1_g, w_in, lambda_q1, lambda_k1, lambda_q2, lambda_k2, subln_g, conv_w,
              w_proj_a, w_proj_b, w_o, norm2_g, w_up, ffn_conv_w, w_down, final_g):
    hp, hs = x_prompt, x_sample
    kp_l, vp_l, cmp_l, cfp_l = [], [], [], []
    ks_l, vs_l, cms_l, cfs_l = [], [], [], []
    for d in range(DEPTH):
        lp = (norm1_g[d], w_in[d], lambda_q1[d], lambda_k1[d], lambda_q2[d], lambda_k2[d], subln_g[d],
              conv_w[d], w_proj_a[d], w_proj_b[d], w_o[d], norm2_g[d], w_up[d], ffn_conv_w[d], w_down[d])
        li = lambda_init_for(d)
        hp, kp, vp, cmp_, cfp = trunk_layer(hp, None, None, None, None, rel_bias, li, *lp)
        hs, ks_, vs_, cms, cfs = trunk_layer(hs, cache_k[d], cache_v[d], state_conv_mix[d],
                                             state_conv_ffn[d], rel_bias, li, *lp)
        kp_l.append(kp); vp_l.append(vp); cmp_l.append(cmp_); cfp_l.append(cfp)
        ks_l.append(ks_); vs_l.append(vs_); cms_l.append(cms); cfs_l.append(cfs)
    y_prompt = rms_norm(hp, final_g)
    y_sample = rms_norm(hs, final_g)
    return (y_prompt, y_sample,
            jnp.stack(kp_l), jnp.stack(vp_l), jnp.stack(cmp_l), jnp.stack(cfp_l),
            jnp.stack(ks_l), jnp.stack(vs_l), jnp.stack(cms_l), jnp.stack(cfs_l))
```

```python
import functools
import math

import jax
import jax.numpy as jnp
from jax import lax
from jax.experimental import pallas as pl
from jax.experimental.pallas import tpu as pltpu

D_MODEL = 2048
N_HEADS = 8
D_QK = 64
D_HEAD = 2 * D_QK
WIDTH_A = N_HEADS * D_HEAD
WIDTH_B = D_MODEL // 2
D_FF = 5632
CONV_W = 3
CHUNK = 64
N_BUCKETS = 32
MAX_DIST = 128
EPS = 1e-6
NEG_INF = -1e30

SUBLANE = 8
LANE = 128
VMEM_LIMIT_BYTES = 56 * 1024 * 1024

F32 = jnp.float32
BF16 = jnp.bfloat16


def _bucket_thresholds():
    nb = N_BUCKETS // 2
    max_exact = nb // 2
    steps = nb - max_exact
    out = []
    for k in range(1, steps):
        n = max_exact
        while n ** steps * max_exact ** k < MAX_DIST ** k * max_exact ** steps:
            n += 1
        out.append(n)
    return tuple(out)


_BUCKET_THRESHOLDS = _bucket_thresholds()
FAR_DIST = _BUCKET_THRESHOLDS[-1]
FAR_BUCKET = N_BUCKETS // 2 - 1
_CHUNK_SHIFT = CHUNK.bit_length() - 1
assert 1 << _CHUNK_SHIFT == CHUNK


def _lambda_init(layer_idx):
    return 0.8 - 0.6 * math.exp(-0.3 * layer_idx)


def _rms(x, g):
    return x * lax.rsqrt(jnp.mean(x * x, axis=-1, keepdims=True) + EPS) * g


def _params(n_axes):
    return pltpu.CompilerParams(
        dimension_semantics=("arbitrary",) * n_axes, vmem_limit_bytes=VMEM_LIMIT_BYTES)


def _bias_tile_kernel(specs, table_ref, *out_refs):
    h = pl.program_id(0)
    nb = N_BUCKETS // 2
    max_exact = nb // 2
    for (rows, cols, q0, k0), out_ref in zip(specs, out_refs):
        qpos = q0 + lax.broadcasted_iota(jnp.int32, (rows, cols), 0)
        kpos = k0 + lax.broadcasted_iota(jnp.int32, (rows, cols), 1)
        rel = kpos - qpos
        n = jnp.abs(rel)
        large = jnp.full((rows, cols), max_exact, jnp.int32)
        for thr in _BUCKET_THRESHOLDS:
            large = large + jnp.where(n >= thr, 1, 0)
        bucket = jnp.where(rel > 0, nb, 0) + jnp.where(n < max_exact, n, large)
        bias = jnp.zeros((rows, cols), F32)
        for b in range(N_BUCKETS):
            bias = jnp.where(bucket == b, table_ref[b, h], bias)
        visible = (lax.shift_right_logical(kpos, _CHUNK_SHIFT)
                   <= lax.shift_right_logical(qpos, _CHUNK_SHIFT))
        out_ref[...] = jnp.where(visible, bias, NEG_INF)


def _bias_tiles(rel_bias, specs):
    return pl.pallas_call(
        functools.partial(_bias_tile_kernel, specs),
        grid=(N_HEADS,),
        in_specs=[pl.BlockSpec(memory_space=pltpu.SMEM)],
        out_specs=[pl.BlockSpec((None, r, c), lambda h: (h, 0, 0)) for r, c, _, _ in specs],
        out_shape=[jax.ShapeDtypeStruct((N_HEADS, r, c), F32) for r, c, _, _ in specs],
        compiler_params=_params(1),
        name="bias_tiles",
    )(rel_bias)


IN_TILE_N = 1024
_N_QKV_TILES = 3
_N_BCX_TILES = 3
_N_GAB_TILES = 4
_N_IN_TILES = _N_QKV_TILES + _N_BCX_TILES + _N_GAB_TILES


def _in_proj_kernel(x_ref, g_ref, w_ref, qkv_ref, kf_ref, vf_ref, bcx_ref, gab_ref, xn_ref):
    j = pl.program_id(1)

    @pl.when(j == 0)
    def _():
        xn_ref[...] = _rms(x_ref[...], g_ref[...]).astype(BF16)

    y = jnp.dot(xn_ref[...], w_ref[...], preferred_element_type=F32)

    @pl.when(j == 0)
    def _():
        qkv_ref[...] = (y * (D_QK ** -0.5)).astype(BF16)

    @pl.when(j == 1)
    def _():
        qkv_ref[...] = y.astype(BF16)
        kf_ref[...] = y

    @pl.when(j == 2)
    def _():
        qkv_ref[...] = y.astype(BF16)
        vf_ref[...] = y

    @pl.when(jnp.logical_and(j >= _N_QKV_TILES, j < _N_QKV_TILES + _N_BCX_TILES))
    def _():
        bcx_ref[...] = y.astype(BF16)

    @pl.when(j >= _N_QKV_TILES + _N_BCX_TILES)
    def _():
        gab_ref[...] = y.astype(BF16)


def _in_proj(x2d, norm_g, w_in_bf, tm):
    t = x2d.shape[0]
    tn = IN_TILE_N
    assert t % tm == 0 and w_in_bf.shape == (D_MODEL, _N_IN_TILES * tn)
    return pl.pallas_call(
        _in_proj_kernel,
        grid=(t // tm, _N_IN_TILES),
        in_specs=[
            pl.BlockSpec((tm, D_MODEL), lambda i, j: (i, 0)),
            pl.BlockSpec((1, D_MODEL), lambda i, j: (0, 0)),
            pl.BlockSpec((D_MODEL, tn), lambda i, j: (0, j)),
        ],
        out_specs=[
            pl.BlockSpec((tm, tn), lambda i, j: (i, jnp.minimum(j, _N_QKV_TILES - 1))),
            pl.BlockSpec((tm, tn), lambda i, j: (i, 0)),
            pl.BlockSpec((tm, tn), lambda i, j: (i, 0)),
            pl.BlockSpec((tm, tn), lambda i, j: (i, jnp.clip(j - _N_QKV_TILES, 0, _N_BCX_TILES - 1))),
            pl.BlockSpec((tm, tn), lambda i, j: (i, jnp.maximum(j - _N_QKV_TILES - _N_BCX_TILES, 0))),
        ],
        out_shape=[
            jax.ShapeDtypeStruct((t, _N_QKV_TILES * tn), BF16),
            jax.ShapeDtypeStruct((t, tn), F32),
            jax.ShapeDtypeStruct((t, tn), F32),
            jax.ShapeDtypeStruct((t, _N_BCX_TILES * tn), BF16),
            jax.ShapeDtypeStruct((t, _N_GAB_TILES * tn), BF16),
        ],
        scratch_shapes=[pltpu.VMEM((tm, D_MODEL), BF16)],
        compiler_params=_params(2),
        name="in_proj",
    )(x2d, norm_g.reshape(1, D_MODEL), w_in_bf)


def _scores(qm, k):
    return lax.dot_general(qm, k, (((1,), (1,)), ((), ())), preferred_element_type=F32)


def _softmax_step(s, v, state, shift=None, init=False):
    m_ref, l_ref, acc_ref = state
    row_max = jnp.max(s, axis=-1, keepdims=True)
    if shift is not None:
        row_max = row_max + shift
    if init:
        m_new = row_max
    else:
        m_old = m_ref[...]
        m_new = jnp.maximum(m_old, row_max)
    p = jnp.exp(s - (m_new if shift is None else m_new - shift))
    p_sum = jnp.sum(p, axis=-1, keepdims=True)
    pv = jnp.dot(p.astype(BF16), v, preferred_element_type=F32)
    if init:
        l_ref[...] = p_sum
        acc_ref[...] = pv
    else:
        alpha = jnp.exp(m_old - m_new)
        l_ref[...] = alpha * l_ref[...] + p_sum
        acc_ref[...] = alpha * acc_ref[...] + pv
    m_ref[...] = m_new


def _split_maps(q):
    lane = lax.broadcasted_iota(jnp.int32, q.shape, 1)
    zero = jnp.zeros_like(q)
    return jnp.where(lane < D_QK, q, zero), jnp.where(lane >= D_QK, q, zero)


def _lambda(lq1_ref, lk1_ref, lq2_ref, lk2_ref, lambda_init):
    s1 = jnp.sum(lq1_ref[...] * lk1_ref[...], axis=-1, keepdims=True)
    s2 = jnp.sum(lq2_ref[...] * lk2_ref[...], axis=-1, keepdims=True)
    return jnp.exp(s1) - jnp.exp(s2) + lambda_init


def _attn_finish(state1, state2, lam, g, lambda_init):
    o = state1[2][...] / state1[1][...] - lam * (state2[2][...] / state2[1][...])
    return _rms(o, g) * (1.0 - lambda_init)


def _attn_prompt_kernel(tq, nq, lambda_init, table_ref, lq1_ref, lk1_ref, lq2_ref, lk2_ref, g_ref,
                        q_ref, k_ref, v_ref, bnear_ref, bdiag_ref, o_ref,
                        m1_ref, l1_ref, a1_ref, m2_ref, l2_ref, a2_ref):
    h = pl.program_id(1)
    far_bias = table_ref[FAR_BUCKET, h]
    lam = _lambda(lq1_ref, lk1_ref, lq2_ref, lk2_ref, lambda_init)
    st1 = (m1_ref, l1_ref, a1_ref)
    st2 = (m2_ref, l2_ref, a2_ref)

    def q_tile(qi, carry):
        q0 = pl.multiple_of(qi * tq, tq)
        q1, q2 = _split_maps(q_ref[pl.ds(q0, tq), :])

        kd = k_ref[pl.ds(q0, tq), :]
        vd = v_ref[pl.ds(q0, tq), :]
        bd = bdiag_ref[...]
        _softmax_step(_scores(q1, kd) + bd, vd, st1, init=True)
        _softmax_step(_scores(q2, kd) + bd, vd, st2, init=True)

        @pl.when(qi > 0)
        def _():
            k0 = pl.multiple_of(q0 - tq, tq)
            kn = k_ref[pl.ds(k0, tq), :]
            vn = v_ref[pl.ds(k0, tq), :]
            bn = bnear_ref[...]
            _softmax_step(_scores(q1, kn) + bn, vn, st1)
            _softmax_step(_scores(q2, kn) + bn, vn, st2)

        def far(j, c):
            k0 = pl.multiple_of(j * tq, tq)
            kf = k_ref[pl.ds(k0, tq), :]
            vf = v_ref[pl.ds(k0, tq), :]
            _softmax_step(_scores(q1, kf), vf, st1, shift=far_bias)
            _softmax_step(_scores(q2, kf), vf, st2, shift=far_bias)
            return c

        lax.fori_loop(0, jnp.maximum(qi - 1, 0), far, 0)

        o = _attn_finish(st1, st2, lam, g_ref[...], lambda_init)
        o_ref[pl.ds(q0, tq), :] = o.astype(BF16)
        return carry

    lax.fori_loop(0, nq, q_tile, 0)


def _attn_prompt(qkv, rel_bias, lam_vecs, subln_g, bias_near, bias_diag, nb, seq, tq, lambda_init):
    assert seq % tq == 0 and tq % CHUNK == 0 and tq >= FAR_DIST
    qkv3 = qkv.reshape(nb, seq, 3 * WIDTH_A)
    vec = pl.BlockSpec((1, D_QK), lambda b, h: (0, 0))
    head_cols = lambda off: pl.BlockSpec((None, seq, D_HEAD), lambda b, h: (b, 0, off + h))
    tile = pl.BlockSpec((None, tq, tq), lambda b, h: (h, 0, 0))
    stat = pltpu.VMEM((tq, 1), F32)
    acc = pltpu.VMEM((tq, D_HEAD), F32)
    o = pl.pallas_call(
        functools.partial(_attn_prompt_kernel, tq, seq // tq, lambda_init),
        grid=(nb, N_HEADS),
        in_specs=[pl.BlockSpec(memory_space=pltpu.SMEM), vec, vec, vec, vec,
                  pl.BlockSpec((1, D_HEAD), lambda b, h: (0, 0)),
                  head_cols(0), head_cols(N_HEADS), head_cols(2 * N_HEADS), tile, tile],
        out_specs=pl.BlockSpec((None, seq, D_HEAD), lambda b, h: (b, 0, h)),
        out_shape=jax.ShapeDtypeStruct((nb, seq, WIDTH_A), BF16),
        scratch_shapes=[stat, stat, acc, stat, stat, acc],
        compiler_params=_params(2),
        name="attn_prompt",
    )(rel_bias, *lam_vecs, subln_g.reshape(1, D_HEAD), qkv3, qkv3, qkv3, bias_near, bias_diag)
    return o.reshape(nb * seq, WIDTH_A)


def _attn_sample_kernel(lambda_init, lq1_ref, lk1_ref, lq2_ref, lk2_ref, g_ref,
                        q_ref, kn_ref, vn_ref, kc_ref, vc_ref, bc_ref, bn_ref, o_ref,
                        m1_ref, l1_ref, a1_ref, m2_ref, l2_ref, a2_ref):
    lam = _lambda(lq1_ref, lk1_ref, lq2_ref, lk2_ref, lambda_init)
    st1 = (m1_ref, l1_ref, a1_ref)
    st2 = (m2_ref, l2_ref, a2_ref)
    q1, q2 = _split_maps(q_ref[...])
    kc = kc_ref[...].astype(BF16)
    vc = vc_ref[...].astype(BF16)
    bc = bc_ref[...]
    _softmax_step(_scores(q1, kc) + bc, vc, st1, init=True)
    _softmax_step(_scores(q2, kc) + bc, vc, st2, init=True)
    kn = kn_ref[...]
    vn = vn_ref[...]
    bn = bn_ref[...]
    _softmax_step(_scores(q1, kn) + bn, vn, st1)
    _softmax_step(_scores(q2, kn) + bn, vn, st2)
    o_ref[...] = _attn_finish(st1, st2, lam, g_ref[...], lambda_init).astype(BF16)


def _attn_sample(qkv, cache_k, cache_v, lam_vecs, subln_g, bias_cache, bias_new, nb, seq, past,
                 lambda_init):
    qkv3 = qkv.reshape(nb, seq, 3 * WIDTH_A)
    kc = cache_k.reshape(nb, past, WIDTH_A)
    vc = cache_v.reshape(nb, past, WIDTH_A)
    vec = pl.BlockSpec((1, D_QK), lambda b, h: (0, 0))
    new_cols = lambda off: pl.BlockSpec((None, seq, D_HEAD), lambda b, h: (b, 0, off + h))
    past_cols = pl.BlockSpec((None, past, D_HEAD), lambda b, h: (b, 0, h))
    stat = pltpu.VMEM((seq, 1), F32)
    acc = pltpu.VMEM((seq, D_HEAD), F32)
    o = pl.pallas_call(
        functools.partial(_attn_sample_kernel, lambda_init),
        grid=(nb, N_HEADS),
        in_specs=[vec, vec, vec, vec, pl.BlockSpec((1, D_HEAD), lambda b, h: (0, 0)),
                  new_cols(0), new_cols(N_HEADS), new_cols(2 * N_HEADS), past_cols, past_cols,
                  pl.BlockSpec((None, seq, past), lambda b, h: (h, 0, 0)),
                  pl.BlockSpec((None, seq, seq), lambda b, h: (h, 0, 0))],
        out_specs=pl.BlockSpec((None, seq, D_HEAD), lambda b, h: (b, 0, h)),
        out_shape=jax.ShapeDtypeStruct((nb, seq, WIDTH_A), BF16),
        scratch_shapes=[stat, stat, acc, stat, stat, acc],
        compiler_params=_params(2),
        name="attn_sample",
    )(*lam_vecs, subln_g.reshape(1, D_HEAD), qkv3, qkv3, qkv3, kc, vc, bias_cache, bias_new)
    return o.reshape(nb * seq, WIDTH_A)


def _causal_conv3(x3, prev8, w):
    p0 = prev8[:, SUBLANE - 2:SUBLANE - 1, :]
    p1 = prev8[:, SUBLANE - 1:SUBLANE, :]
    row = lax.broadcasted_iota(jnp.int32, x3.shape, 1)
    x1 = jnp.where(row == 0, p1, pltpu.roll(x3, 1, axis=1))
    x2 = jnp.where(row == 0, p0, jnp.where(row == 1, p1, pltpu.roll(x3, 2, axis=1)))
    return x2 * w[0:1, :] + x1 * w[1:2, :] + x3 * w[2:3, :]


def _conv_with_state(x, i, nseq, tps, st_ref, so_ref, halo):
    rows = x.shape[0] // nseq
    x3 = x.reshape(nseq, rows, x.shape[1])
    last8 = x3[:, rows - SUBLANE:, :]
    if tps == 1:
        prev8 = st_ref[...]
    else:
        @pl.when(i % tps == 0)
        def _():
            halo[...] = st_ref[...]

        prev8 = halo[...]
        halo[...] = last8
    so_ref[...] = last8
    return x3, prev8


def _merge_kernel(nseq, tps, o_ref, b_ref, c_ref, xi_ref, ga_ref, gb_ref, x_ref, st_ref, cw_ref,
                  wa_ref, wb_ref, wo_ref, g2_ref, h_ref, hn_ref, so_ref, *scratch):
    i = pl.program_id(0)
    u = c_ref[...].astype(F32) * xi_ref[...].astype(F32)
    u3, prev8 = _conv_with_state(u, i, nseq, tps, st_ref, so_ref, scratch[0] if tps > 1 else None)
    z = _causal_conv3(u3, prev8, cw_ref[...]).reshape(u.shape)
    ob = (b_ref[...].astype(F32) * z).astype(BF16)
    ya = jnp.dot(o_ref[...], wa_ref[...], preferred_element_type=F32)
    yb = jnp.dot(ob, wb_ref[...], preferred_element_type=F32)
    merged = jax.nn.sigmoid(ga_ref[...].astype(F32)) * ya + jax.nn.sigmoid(gb_ref[...].astype(F32)) * yb
    hres = x_ref[...] + jnp.dot(merged.astype(BF16), wo_ref[...], preferred_element_type=F32)
    h_ref[...] = hres
    hn_ref[...] = _rms(hres, g2_ref[...]).astype(BF16)


def _merge(o, bcx, gab, x2d, state8, conv_w, wa, wb, wo, norm2_g, nb, seq, rows):
    t = x2d.shape[0]
    nseq = max(rows // seq, 1)
    tps = max(seq // rows, 1)
    assert t % rows == 0 and (rows % seq == 0 or seq % rows == 0)
    tok = lambda width, col: pl.BlockSpec((rows, width), lambda i: (i, col))
    full = lambda a: pl.BlockSpec(a.shape, lambda i: (0,) * a.ndim)
    state_spec = pl.BlockSpec((nseq, SUBLANE, WIDTH_B), lambda i: (i // tps, 0, 0))
    scratch = [pltpu.VMEM((nseq, SUBLANE, WIDTH_B), F32)] if tps > 1 else []
    g2 = norm2_g.reshape(1, D_MODEL)
    return pl.pallas_call(
        functools.partial(_merge_kernel, nseq, tps),
        grid=(t // rows,),
        in_specs=[tok(WIDTH_A, 0), tok(WIDTH_B, 0), tok(WIDTH_B, 1), tok(WIDTH_B, 2),
                  tok(D_MODEL, 0), tok(D_MODEL, 1), tok(D_MODEL, 0), state_spec,
                  full(conv_w), full(wa), full(wb), full(wo), full(g2)],
        out_specs=[tok(D_MODEL, 0), tok(D_MODEL, 0), state_spec],
        out_shape=[jax.ShapeDtypeStruct((t, D_MODEL), F32),
                   jax.ShapeDtypeStruct((t, D_MODEL), BF16),
                   jax.ShapeDtypeStruct((nb, SUBLANE, WIDTH_B), F32)],
        scratch_shapes=scratch,
        compiler_params=_params(1),
        name="merge",
    )(o, bcx, bcx, bcx, gab, gab, x2d, state8, conv_w, wa, wb, wo, g2)


FFN_TILE_F = 512


def _ffn_kernel(nseq, tps, nf, hn_ref, h_ref, wg_ref, wv_ref, cwg_ref, cwv_ref, stg_ref, stv_ref,
                wd_ref, fg_ref, y_ref, sog_ref, sov_ref, acc_ref, *scratch):
    i = pl.program_id(0)
    f = pl.program_id(1)
    hn = hn_ref[...]

    def branch(w_ref, cw_ref, st_ref, so_ref, slot):
        up = jnp.dot(hn, w_ref[...], preferred_element_type=F32)
        halo = scratch[0].at[2 * f + slot] if tps > 1 else None
        up3, prev8 = _conv_with_state(up, i, nseq, tps, st_ref, so_ref, halo)
        return _causal_conv3(up3, prev8, cw_ref[...]).reshape(up.shape)

    gate = branch(wg_ref, cwg_ref, stg_ref, sog_ref, 0)
    val = branch(wv_ref, cwv_ref, stv_ref, sov_ref, 1)
    act = (jax.nn.silu(gate) * val).astype(BF16)
    part = jnp.dot(act, wd_ref[...], preferred_element_type=F32)

    @pl.when(f == 0)
    def _():
        acc_ref[...] = part

    @pl.when(f > 0)
    def _():
        acc_ref[...] += part

    @pl.when(f == nf - 1)
    def _():
        y_ref[...] = _rms(h_ref[...] + acc_ref[...], fg_ref[...])


def _ffn(hn, hres, w_up, ffn_conv_w, state8, w_down, final_g, nb, seq, rows):
    t = hn.shape[0]
    tf = FFN_TILE_F
    nf = D_FF // tf
    nseq = max(rows // seq, 1)
    tps = max(seq // rows, 1)
    assert D_FF % tf == 0 and t % rows == 0 and (rows % seq == 0 or seq % rows == 0)
    tok = pl.BlockSpec((rows, D_MODEL), lambda i, f: (i, 0))
    gate_cols = lambda r: pl.BlockSpec((r, tf), lambda i, f: (0, f))
    val_cols = lambda r: pl.BlockSpec((r, tf), lambda i, f: (0, nf + f))
    st_gate = pl.BlockSpec((nseq, SUBLANE, tf), lambda i, f: (i // tps, 0, f))
    st_val = pl.BlockSpec((nseq, SUBLANE, tf), lambda i, f: (i // tps, 0, nf + f))
    scratch = [pltpu.VMEM((rows, D_MODEL), F32)]
    if tps > 1:
        scratch.append(pltpu.VMEM((2 * nf, nseq, SUBLANE, tf), F32))
    tile_rows = pl.BlockSpec((nseq, SUBLANE, tf), lambda i, f: (i, 0, f))
    fg = final_g.reshape(1, D_MODEL)
    y, tail_g, tail_v = pl.pallas_call(
        functools.partial(_ffn_kernel, nseq, tps, nf),
        grid=(t // rows, nf),
        in_specs=[tok, tok, gate_cols(D_MODEL), val_cols(D_MODEL), gate_cols(CONV_W), val_cols(CONV_W),
                  st_gate, st_val,
                  pl.BlockSpec((tf, D_MODEL), lambda i, f: (f, 0)),
                  pl.BlockSpec((1, D_MODEL), lambda i, f: (0, 0))],
        out_specs=[tok, tile_rows, tile_rows],
        out_shape=[jax.ShapeDtypeStruct((t, D_MODEL), F32),
                   jax.ShapeDtypeStruct((nb * tps, SUBLANE, D_FF), F32),
                   jax.ShapeDtypeStruct((nb * tps, SUBLANE, D_FF), F32)],
        scratch_shapes=scratch,
        compiler_params=_params(2),
        name="ffn",
    )(hn, hres, w_up, w_up, ffn_conv_w, ffn_conv_w, state8, state8, w_down, fg)
    return y, tail_g[tps - 1::tps], tail_v[tps - 1::tps]


def _state8(state):
    return jnp.pad(state, ((0, 0), (SUBLANE - (CONV_W - 1), 0), (0, 0)))


def _state_out(state8):
    return state8[:, SUBLANE - (CONV_W - 1):, :]


def _layer(x2d, nb, seq, attn_fn, conv8, ffn8, lp, lambda_init, tiles):
    qkv, k_f32, v_f32, bcx, gab = _in_proj(x2d, lp["norm1_g"], lp["w_in"], tiles["in"])
    o = attn_fn(qkv)
    hres, hn, conv_out8 = _merge(o, bcx, gab, x2d, conv8, lp["conv_w"], lp["w_proj_a"], lp["w_proj_b"],
                                 lp["w_o"], lp["norm2_g"], nb, seq, tiles["merge"])
    y, ffn_g8, ffn_v8 = _ffn(hn, hres, lp["w_up"], lp["ffn_conv_w"], ffn8, lp["w_down"], lp["final_g"],
                             nb, seq, tiles["ffn"])
    k = k_f32.reshape(nb, seq, N_HEADS, D_HEAD)
    v = v_f32.reshape(nb, seq, N_HEADS, D_HEAD)
    conv_state = _state_out(conv_out8)
    ffn_state = jnp.concatenate([_state_out(ffn_g8), _state_out(ffn_v8)], axis=-1)
    return y, k, v, conv_state, ffn_state


PROMPT_TILES = {"in": 512, "merge": 256, "ffn": 512}
SAMPLE_TILES = {"in": 512, "merge": 256, "ffn": 512}
ATTN_TILE_Q = 256


def kernel(x_prompt, x_sample, cache_k, cache_v, state_conv_mix, state_conv_ffn, rel_bias, norm1_g, w_in,
           lambda_q1, lambda_k1, lambda_q2, lambda_k2, subln_g, conv_w, w_proj_a, w_proj_b, w_o, norm2_g,
           w_up, ffn_conv_w, w_down, final_g):
    depth = w_in.shape[0]
    nbp, seq_p, _ = x_prompt.shape
    nbs, seq_s, _ = x_sample.shape
    past = cache_k.shape[2]
    tq = ATTN_TILE_Q

    bias_near, bias_diag, bias_cache, bias_new = _bias_tiles(
        rel_bias,
        ((tq, tq, tq, 0), (tq, tq, tq, tq), (seq_s, past, past, 0), (seq_s, seq_s, past, past)))

    hp = x_prompt.reshape(nbp * seq_p, D_MODEL)
    hs = x_sample.reshape(nbs * seq_s, D_MODEL)
    outs_p, outs_s = [], []
    for d in range(depth):
        assert depth == 1, "the ffn kernel ends with the final RMSNorm, so it is written for a single layer"
        lp = {
            "norm1_g": norm1_g[d], "w_in": w_in[d].astype(BF16), "conv_w": conv_w[d],
            "w_proj_a": w_proj_a[d].astype(BF16), "w_proj_b": w_proj_b[d].astype(BF16),
            "w_o": w_o[d].astype(BF16), "norm2_g": norm2_g[d], "w_up": w_up[d].astype(BF16),
            "ffn_conv_w": ffn_conv_w[d], "w_down": w_down[d].astype(BF16),
            "final_g": final_g,
        }
        li = _lambda_init(d)
        lam_vecs = tuple(a[d].reshape(1, D_QK) for a in (lambda_q1, lambda_k1, lambda_q2, lambda_k2))

        attn_p = lambda qkv: _attn_prompt(qkv, rel_bias, lam_vecs, subln_g[d], bias_near, bias_diag,
                                          nbp, seq_p, tq, li)
        zeros_mix = jnp.zeros((nbp, SUBLANE, WIDTH_B), F32)
        zeros_ffn = jnp.zeros((nbp, SUBLANE, 2 * D_FF), F32)
        hp, kp, vp, cmp_, cfp = _layer(hp, nbp, seq_p, attn_p, zeros_mix, zeros_ffn, lp, li, PROMPT_TILES)

        attn_s = lambda qkv: _attn_sample(qkv, cache_k[d], cache_v[d], lam_vecs, subln_g[d], bias_cache,
                                          bias_new, nbs, seq_s, past, li)
        hs, ks, vs, cms, cfs = _layer(hs, nbs, seq_s, attn_s, _state8(state_conv_mix[d]),
                                      _state8(state_conv_ffn[d]), lp, li, SAMPLE_TILES)
        outs_p.append((kp, vp, cmp_, cfp))
        outs_s.append((ks, vs, cms, cfs))

    stack = lambda outs, idx: jnp.stack([o[idx] for o in outs])
    return (hp.reshape(nbp, seq_p, D_MODEL), hs.reshape(nbs, seq_s, D_MODEL),
            stack(outs_p, 0), stack(outs_p, 1), stack(outs_p, 2), stack(outs_p, 3),
            stack(outs_s, 0), stack(outs_s, 1), stack(outs_s, 2), stack(outs_s, 3))
```

```python
import functools
import math

import jax
import jax.numpy as jnp
from jax import lax
from jax.experimental import pallas as pl
from jax.experimental.pallas import tpu as pltpu

D_MODEL = 2048
N_HEADS = 8
D_QK = 64
D_HEAD = 2 * D_QK
WIDTH_A = N_HEADS * D_HEAD
WIDTH_B = D_MODEL // 2
D_FF = 5632
CONV_W = 3
CHUNK = 64
N_BUCKETS = 32
MAX_DIST = 128
EPS = 1e-6
NEG_INF = -1e30

SUBLANE = 8
LANE = 128
VMEM_LIMIT_BYTES = 56 * 1024 * 1024

F32 = jnp.float32
BF16 = jnp.bfloat16


def _bucket_thresholds():
    nb = N_BUCKETS // 2
    max_exact = nb // 2
    steps = nb - max_exact
    out = []
    for k in range(1, steps):
        n = max_exact
        while n ** steps * max_exact ** k < MAX_DIST ** k * max_exact ** steps:
            n += 1
        out.append(n)
    return tuple(out)


_BUCKET_THRESHOLDS = _bucket_thresholds()
FAR_DIST = _BUCKET_THRESHOLDS[-1]
FAR_BUCKET = N_BUCKETS // 2 - 1
_CHUNK_SHIFT = CHUNK.bit_length() - 1
assert 1 << _CHUNK_SHIFT == CHUNK


def _lambda_init(layer_idx):
    return 0.8 - 0.6 * math.exp(-0.3 * layer_idx)


def _rms(x, g):
    return x * lax.rsqrt(jnp.mean(x * x, axis=-1, keepdims=True) + EPS) * g


def _params(n_axes):
    return pltpu.CompilerParams(
        dimension_semantics=("arbitrary",) * n_axes, vmem_limit_bytes=VMEM_LIMIT_BYTES)


def _bias_tile_kernel(specs, table_ref, *out_refs):
    h = pl.program_id(0)
    nb = N_BUCKETS // 2
    max_exact = nb // 2
    for (rows, cols, q0, k0), out_ref in zip(specs, out_refs):
        qpos = q0 + lax.broadcasted_iota(jnp.int32, (rows, cols), 0)
        kpos = k0 + lax.broadcasted_iota(jnp.int32, (rows, cols), 1)
        rel = kpos - qpos
        n = jnp.abs(rel)
        large = jnp.full((rows, cols), max_exact, jnp.int32)
        for thr in _BUCKET_THRESHOLDS:
            large = large + jnp.where(n >= thr, 1, 0)
        bucket = jnp.where(rel > 0, nb, 0) + jnp.where(n < max_exact, n, large)
        bias = jnp.zeros((rows, cols), F32)
        for b in range(N_BUCKETS):
            bias = jnp.where(bucket == b, table_ref[b, h], bias)
        visible = (lax.shift_right_logical(kpos, _CHUNK_SHIFT)
                   <= lax.shift_right_logical(qpos, _CHUNK_SHIFT))
        out_ref[...] = jnp.where(visible, bias, NEG_INF)


def _bias_tiles(rel_bias, specs):
    return pl.pallas_call(
        functools.partial(_bias_tile_kernel, specs),
        grid=(N_HEADS,),
        in_specs=[pl.BlockSpec(memory_space=pltpu.SMEM)],
        out_specs=[pl.BlockSpec((None, r, c), lambda h: (h, 0, 0)) for r, c, _, _ in specs],
        out_shape=[jax.ShapeDtypeStruct((N_HEADS, r, c), F32) for r, c, _, _ in specs],
        compiler_params=_params(1),
        name="bias_tiles",
    )(rel_bias)


IN_TILE_N = 1024
_N_QKV_TILES = 3
_N_BCX_TILES = 3
_N_GAB_TILES = 4
_N_IN_TILES = _N_QKV_TILES + _N_BCX_TILES + _N_GAB_TILES


def _in_proj_kernel(x_ref, g_ref, w_ref, qkv_ref, kf_ref, vf_ref, bcx_ref, gab_ref, xn_ref):
    j = pl.program_id(1)

    @pl.when(j == 0)
    def _():
        xn_ref[...] = _rms(x_ref[...], g_ref[...]).astype(BF16)

    y = jnp.dot(xn_ref[...], w_ref[...], preferred_element_type=F32)

    @pl.when(j == 0)
    def _():
        qkv_ref[...] = (y * (D_QK ** -0.5)).astype(BF16)

    @pl.when(j == 1)
    def _():
        qkv_ref[...] = y.astype(BF16)
        kf_ref[...] = y

    @pl.when(j == 2)
    def _():
        qkv_ref[...] = y.astype(BF16)
        vf_ref[...] = y

    @pl.when(jnp.logical_and(j >= _N_QKV_TILES, j < _N_QKV_TILES + _N_BCX_TILES))
    def _():
        bcx_ref[...] = y.astype(BF16)

    @pl.when(j >= _N_QKV_TILES + _N_BCX_TILES)
    def _():
        gab_ref[...] = y.astype(BF16)


def _in_proj(x2d, norm_g, w_in_bf, tm):
    t = x2d.shape[0]
    tn = IN_TILE_N
    assert t % tm == 0 and w_in_bf.shape == (D_MODEL, _N_IN_TILES * tn)
    return pl.pallas_call(
        _in_proj_kernel,
        grid=(t // tm, _N_IN_TILES),
        in_specs=[
            pl.BlockSpec((tm, D_MODEL), lambda i, j: (i, 0)),
            pl.BlockSpec((1, D_MODEL), lambda i, j: (0, 0)),
            pl.BlockSpec((D_MODEL, tn), lambda i, j: (0, j)),
        ],
        out_specs=[
            pl.BlockSpec((tm, tn), lambda i, j: (i, jnp.minimum(j, _N_QKV_TILES - 1))),
            pl.BlockSpec((tm, tn), lambda i, j: (i, 0)),
            pl.BlockSpec((tm, tn), lambda i, j: (i, 0)),
            pl.BlockSpec((tm, tn), lambda i, j: (i, jnp.clip(j - _N_QKV_TILES, 0, _N_BCX_TILES - 1))),
            pl.BlockSpec((tm, tn), lambda i, j: (i, jnp.maximum(j - _N_QKV_TILES - _N_BCX_TILES, 0))),
        ],
        out_shape=[
            jax.ShapeDtypeStruct((t, _N_QKV_TILES * tn), BF16),
            jax.ShapeDtypeStruct((t, tn), F32),
            jax.ShapeDtypeStruct((t, tn), F32),
            jax.ShapeDtypeStruct((t, _N_BCX_TILES * tn), BF16),
            jax.ShapeDtypeStruct((t, _N_GAB_TILES * tn), BF16),
        ],
        scratch_shapes=[pltpu.VMEM((tm, D_MODEL), BF16)],
        compiler_params=_params(2),
        name="in_proj",
    )(x2d, norm_g.reshape(1, D_MODEL), w_in_bf)


def _scores(qm, k):
    return lax.dot_general(qm, k, (((1,), (1,)), ((), ())), preferred_element_type=F32)


def _with_ones(v):
    return jnp.concatenate([v, jnp.ones_like(v)], axis=1)


def _softmax_step(s, v_ext, state, shift=None, init=False):
    m_ref, acc_ref = state
    rows, keys = s.shape
    row_max = jnp.max(s, axis=-1, keepdims=True)
    if shift is not None:
        row_max = row_max + shift
    if init:
        m_new = jnp.broadcast_to(row_max, (rows, LANE))
    else:
        m_old = m_ref[...]
        m_new = jnp.maximum(m_old, row_max)
    sub = m_new if shift is None else m_new - shift
    assert keys % LANE == 0 or keys < LANE
    p = jnp.exp(s - (pltpu.repeat(sub, keys // LANE, axis=1) if keys >= LANE else sub[:, :keys]))
    pv = jnp.dot(p.astype(BF16), v_ext, preferred_element_type=F32)
    if init:
        acc_ref[...] = pv
    else:
        alpha = jnp.exp(m_old - m_new)
        acc_ref[...] = pltpu.repeat(alpha, 2 * D_HEAD // LANE, axis=1) * acc_ref[...] + pv
    m_ref[...] = m_new


def _split_maps(q):
    lane = lax.broadcasted_iota(jnp.int32, q.shape, 1)
    zero = jnp.zeros_like(q)
    return jnp.where(lane < D_QK, q, zero), jnp.where(lane >= D_QK, q, zero)


def _lambda(lq1_ref, lk1_ref, lq2_ref, lk2_ref, lambda_init):
    s1 = jnp.sum(lq1_ref[...] * lk1_ref[...], axis=-1, keepdims=True)
    s2 = jnp.sum(lq2_ref[...] * lk2_ref[...], axis=-1, keepdims=True)
    return jnp.exp(s1) - jnp.exp(s2) + lambda_init


def _attn_finish(state1, state2, lam, g, lambda_init):
    a1 = state1[1][...]
    a2 = state2[1][...]
    o = a1[:, :D_HEAD] / a1[:, D_HEAD:] - lam * (a2[:, :D_HEAD] / a2[:, D_HEAD:])
    return _rms(o, g) * (1.0 - lambda_init)


def _attn_prompt_kernel(tq, nq, lambda_init, table_ref, lq1_ref, lk1_ref, lq2_ref, lk2_ref, g_ref,
                        q_ref, k_ref, v_ref, bnear_ref, bdiag_ref, o_ref,
                        m1_ref, a1_ref, m2_ref, a2_ref):
    h = pl.program_id(1)
    far_bias = table_ref[FAR_BUCKET, h]
    lam = _lambda(lq1_ref, lk1_ref, lq2_ref, lk2_ref, lambda_init)
    st1 = (m1_ref, a1_ref)
    st2 = (m2_ref, a2_ref)

    def q_tile(qi, carry):
        q0 = pl.multiple_of(qi * tq, tq)
        q1, q2 = _split_maps(q_ref[pl.ds(q0, tq), :])

        kd = k_ref[pl.ds(q0, tq), :]
        vd = _with_ones(v_ref[pl.ds(q0, tq), :])
        bd = bdiag_ref[...]
        _softmax_step(_scores(q1, kd) + bd, vd, st1, init=True)
        _softmax_step(_scores(q2, kd) + bd, vd, st2, init=True)

        @pl.when(qi > 0)
        def _():
            k0 = pl.multiple_of(q0 - tq, tq)
            kn = k_ref[pl.ds(k0, tq), :]
            vn = _with_ones(v_ref[pl.ds(k0, tq), :])
            bn = bnear_ref[...]
            _softmax_step(_scores(q1, kn) + bn, vn, st1)
            _softmax_step(_scores(q2, kn) + bn, vn, st2)

        def far(j, c):
            k0 = pl.multiple_of(j * tq, tq)
            kf = k_ref[pl.ds(k0, tq), :]
            vf = _with_ones(v_ref[pl.ds(k0, tq), :])
            _softmax_step(_scores(q1, kf), vf, st1, shift=far_bias)
            _softmax_step(_scores(q2, kf), vf, st2, shift=far_bias)
            return c

        lax.fori_loop(0, jnp.maximum(qi - 1, 0), far, 0)

        o = _attn_finish(st1, st2, lam, g_ref[...], lambda_init)
        o_ref[pl.ds(q0, tq), :] = o.astype(BF16)
        return carry

    lax.fori_loop(0, nq, q_tile, 0)


def _attn_prompt(qkv, rel_bias, lam_vecs, subln_g, bias_near, bias_diag, nb, seq, tq, lambda_init):
    assert seq % tq == 0 and tq % CHUNK == 0 and tq >= FAR_DIST
    qkv3 = qkv.reshape(nb, seq, 3 * WIDTH_A)
    vec = pl.BlockSpec((1, D_QK), lambda b, h: (0, 0))
    head_cols = lambda off: pl.BlockSpec((None, seq, D_HEAD), lambda b, h: (b, 0, off + h))
    tile = pl.BlockSpec((None, tq, tq), lambda b, h: (h, 0, 0))
    stat = pltpu.VMEM((tq, LANE), F32)
    acc = pltpu.VMEM((tq, 2 * D_HEAD), F32)
    o = pl.pallas_call(
        functools.partial(_attn_prompt_kernel, tq, seq // tq, lambda_init),
        grid=(nb, N_HEADS),
        in_specs=[pl.BlockSpec(memory_space=pltpu.SMEM), vec, vec, vec, vec,
                  pl.BlockSpec((1, D_HEAD), lambda b, h: (0, 0)),
                  head_cols(0), head_cols(N_HEADS), head_cols(2 * N_HEADS), tile, tile],
        out_specs=pl.BlockSpec((None, seq, D_HEAD), lambda b, h: (b, 0, h)),
        out_shape=jax.ShapeDtypeStruct((nb, seq, WIDTH_A), BF16),
        scratch_shapes=[stat, acc, stat, acc],
        compiler_params=_params(2),
        name="attn_prompt",
    )(rel_bias, *lam_vecs, subln_g.reshape(1, D_HEAD), qkv3, qkv3, qkv3, bias_near, bias_diag)
    return o.reshape(nb * seq, WIDTH_A)


def _attn_sample_kernel(lambda_init, lq1_ref, lk1_ref, lq2_ref, lk2_ref, g_ref,
                        q_ref, kn_ref, vn_ref, kc_ref, vc_ref, bc_ref, bn_ref, o_ref,
                        m1_ref, a1_ref, m2_ref, a2_ref):
    lam = _lambda(lq1_ref, lk1_ref, lq2_ref, lk2_ref, lambda_init)
    st1 = (m1_ref, a1_ref)
    st2 = (m2_ref, a2_ref)
    q1, q2 = _split_maps(q_ref[...])
    kc = kc_ref[...].astype(BF16)
    vc = _with_ones(vc_ref[...].astype(BF16))
    bc = bc_ref[...]
    _softmax_step(_scores(q1, kc) + bc, vc, st1, init=True)
    _softmax_step(_scores(q2, kc) + bc, vc, st2, init=True)
    kn = kn_ref[...]
    vn = _with_ones(vn_ref[...])
    bn = bn_ref[...]
    _softmax_step(_scores(q1, kn) + bn, vn, st1)
    _softmax_step(_scores(q2, kn) + bn, vn, st2)
    o_ref[...] = _attn_finish(st1, st2, lam, g_ref[...], lambda_init).astype(BF16)


def _attn_sample(qkv, cache_k, cache_v, layer, lam_vecs, subln_g, bias_cache, bias_new, nb, seq, past,
                 lambda_init):
    qkv3 = qkv.reshape(nb, seq, 3 * WIDTH_A)
    kc = cache_k.reshape(-1, past, WIDTH_A)
    vc = cache_v.reshape(-1, past, WIDTH_A)
    vec = pl.BlockSpec((1, D_QK), lambda b, h: (0, 0))
    new_cols = lambda off: pl.BlockSpec((None, seq, D_HEAD), lambda b, h: (b, 0, off + h))
    past_cols = pl.BlockSpec((None, past, D_HEAD), lambda b, h: (layer * nb + b, 0, h))
    stat = pltpu.VMEM((seq, LANE), F32)
    acc = pltpu.VMEM((seq, 2 * D_HEAD), F32)
    o = pl.pallas_call(
        functools.partial(_attn_sample_kernel, lambda_init),
        grid=(nb, N_HEADS),
        in_specs=[vec, vec, vec, vec, pl.BlockSpec((1, D_HEAD), lambda b, h: (0, 0)),
                  new_cols(0), new_cols(N_HEADS), new_cols(2 * N_HEADS), past_cols, past_cols,
                  pl.BlockSpec((None, seq, past), lambda b, h: (h, 0, 0)),
                  pl.BlockSpec((None, seq, seq), lambda b, h: (h, 0, 0))],
        out_specs=pl.BlockSpec((None, seq, D_HEAD), lambda b, h: (b, 0, h)),
        out_shape=jax.ShapeDtypeStruct((nb, seq, WIDTH_A), BF16),
        scratch_shapes=[stat, acc, stat, acc],
        compiler_params=_params(2),
        name="attn_sample",
    )(*lam_vecs, subln_g.reshape(1, D_HEAD), qkv3, qkv3, qkv3, kc, vc, bias_cache, bias_new)
    return o.reshape(nb * seq, WIDTH_A)


def _causal_conv3(x3, prev8, w):
    p0 = prev8[:, SUBLANE - 2:SUBLANE - 1, :]
    p1 = prev8[:, SUBLANE - 1:SUBLANE, :]
    row = lax.broadcasted_iota(jnp.int32, x3.shape, 1)
    x1 = jnp.where(row == 0, p1, pltpu.roll(x3, 1, axis=1))
    x2 = jnp.where(row == 0, p0, jnp.where(row == 1, p1, pltpu.roll(x3, 2, axis=1)))
    return x2 * w[0:1, :] + x1 * w[1:2, :] + x3 * w[2:3, :]


def _conv_with_state(x, i, nseq, tps, st_ref, so_ref, halo):
    rows = x.shape[0] // nseq
    x3 = x.reshape(nseq, rows, x.shape[1])
    last8 = x3[:, rows - SUBLANE:, :]
    if tps == 1:
        prev8 = st_ref[...]
    else:
        @pl.when(i % tps == 0)
        def _():
            halo[...] = st_ref[...]

        prev8 = halo[...]
        halo[...] = last8
    so_ref[...] = last8
    return x3, prev8


def _merge_kernel(nseq, tps, o_ref, b_ref, c_ref, xi_ref, ga_ref, gb_ref, x_ref, st_ref, cw_ref,
                  wa_ref, wb_ref, wo_ref, g2_ref, h_ref, hn_ref, so_ref, *scratch):
    i = pl.program_id(0)
    u = c_ref[...].astype(F32) * xi_ref[...].astype(F32)
    u3, prev8 = _conv_with_state(u, i, nseq, tps, st_ref, so_ref, scratch[0] if tps > 1 else None)
    z = _causal_conv3(u3, prev8, cw_ref[...]).reshape(u.shape)
    ob = (b_ref[...].astype(F32) * z).astype(BF16)
    ya = jnp.dot(o_ref[...], wa_ref[...], preferred_element_type=F32)
    yb = jnp.dot(ob, wb_ref[...], preferred_element_type=F32)
    merged = jax.nn.sigmoid(ga_ref[...].astype(F32)) * ya + jax.nn.sigmoid(gb_ref[...].astype(F32)) * yb
    hres = x_ref[...] + jnp.dot(merged.astype(BF16), wo_ref[...], preferred_element_type=F32)
    h_ref[...] = hres
    hn_ref[...] = _rms(hres, g2_ref[...]).astype(BF16)


def _merge(o, bcx, gab, x2d, state8, conv_w, wa, wb, wo, norm2_g, nb, seq, rows):
    t = x2d.shape[0]
    nseq = max(rows // seq, 1)
    tps = max(seq // rows, 1)
    assert t % rows == 0 and (rows % seq == 0 or seq % rows == 0)
    tok = lambda width, col: pl.BlockSpec((rows, width), lambda i: (i, col))
    full = lambda a: pl.BlockSpec(a.shape, lambda i: (0,) * a.ndim)
    state_spec = pl.BlockSpec((nseq, SUBLANE, WIDTH_B), lambda i: (i // tps, 0, 0))
    scratch = [pltpu.VMEM((nseq, SUBLANE, WIDTH_B), F32)] if tps > 1 else []
    g2 = norm2_g.reshape(1, D_MODEL)
    return pl.pallas_call(
        functools.partial(_merge_kernel, nseq, tps),
        grid=(t // rows,),
        in_specs=[tok(WIDTH_A, 0), tok(WIDTH_B, 0), tok(WIDTH_B, 1), tok(WIDTH_B, 2),
                  tok(D_MODEL, 0), tok(D_MODEL, 1), tok(D_MODEL, 0), state_spec,
                  full(conv_w), full(wa), full(wb), full(wo), full(g2)],
        out_specs=[tok(D_MODEL, 0), tok(D_MODEL, 0), state_spec],
        out_shape=[jax.ShapeDtypeStruct((t, D_MODEL), F32),
                   jax.ShapeDtypeStruct((t, D_MODEL), BF16),
                   jax.ShapeDtypeStruct((nb, SUBLANE, WIDTH_B), F32)],
        scratch_shapes=scratch,
        compiler_params=_params(1),
        name="merge",
    )(o, bcx, bcx, bcx, gab, gab, x2d, state8, conv_w, wa, wb, wo, g2)


FFN_TILE_F = 512


def _ffn_kernel(nseq, tps, nf, hn_ref, h_ref, wg_ref, wv_ref, cwg_ref, cwv_ref, stg_ref, stv_ref,
                wd_ref, fg_ref, y_ref, sog_ref, sov_ref, acc_ref, *scratch):
    i = pl.program_id(0)
    f = pl.program_id(1)
    hn = hn_ref[...]

    def branch(w_ref, cw_ref, st_ref, so_ref, slot):
        up = jnp.dot(hn, w_ref[...], preferred_element_type=F32)
        halo = scratch[0].at[2 * f + slot] if tps > 1 else None
        up3, prev8 = _conv_with_state(up, i, nseq, tps, st_ref, so_ref, halo)
        return _causal_conv3(up3, prev8, cw_ref[...]).reshape(up.shape)

    gate = branch(wg_ref, cwg_ref, stg_ref, sog_ref, 0)
    val = branch(wv_ref, cwv_ref, stv_ref, sov_ref, 1)
    act = (jax.nn.silu(gate) * val).astype(BF16)
    part = jnp.dot(act, wd_ref[...], preferred_element_type=F32)

    @pl.when(f == 0)
    def _():
        acc_ref[...] = part

    @pl.when(f > 0)
    def _():
        acc_ref[...] += part

    @pl.when(f == nf - 1)
    def _():
        y_ref[...] = _rms(h_ref[...] + acc_ref[...], fg_ref[...])


def _ffn(hn, hres, w_up, ffn_conv_w, state8, w_down, final_g, nb, seq, rows):
    t = hn.shape[0]
    tf = FFN_TILE_F
    nf = D_FF // tf
    nseq = max(rows // seq, 1)
    tps = max(seq // rows, 1)
    assert D_FF % tf == 0 and t % rows == 0 and (rows % seq == 0 or seq % rows == 0)
    tok = pl.BlockSpec((rows, D_MODEL), lambda i, f: (i, 0))
    gate_cols = lambda r: pl.BlockSpec((r, tf), lambda i, f: (0, f))
    val_cols = lambda r: pl.BlockSpec((r, tf), lambda i, f: (0, nf + f))
    st_gate = pl.BlockSpec((nseq, SUBLANE, tf), lambda i, f: (i // tps, 0, f))
    st_val = pl.BlockSpec((nseq, SUBLANE, tf), lambda i, f: (i // tps, 0, nf + f))
    scratch = [pltpu.VMEM((rows, D_MODEL), F32)]
    if tps > 1:
        scratch.append(pltpu.VMEM((2 * nf, nseq, SUBLANE, tf), F32))
    tile_rows = pl.BlockSpec((nseq, SUBLANE, tf), lambda i, f: (i, 0, f))
    fg = final_g.reshape(1, D_MODEL)
    y, tail_g, tail_v = pl.pallas_call(
        functools.partial(_ffn_kernel, nseq, tps, nf),
        grid=(t // rows, nf),
        in_specs=[tok, tok, gate_cols(D_MODEL), val_cols(D_MODEL), gate_cols(CONV_W), val_cols(CONV_W),
                  st_gate, st_val,
                  pl.BlockSpec((tf, D_MODEL), lambda i, f: (f, 0)),
                  pl.BlockSpec((1, D_MODEL), lambda i, f: (0, 0))],
        out_specs=[tok, tile_rows, tile_rows],
        out_shape=[jax.ShapeDtypeStruct((t, D_MODEL), F32),
                   jax.ShapeDtypeStruct((nb * tps, SUBLANE, D_FF), F32),
                   jax.ShapeDtypeStruct((nb * tps, SUBLANE, D_FF), F32)],
        scratch_shapes=scratch,
        compiler_params=_params(2),
        name="ffn",
    )(hn, hres, w_up, w_up, ffn_conv_w, ffn_conv_w, state8, state8, w_down, fg)
    return y, tail_g[tps - 1::tps], tail_v[tps - 1::tps]


def _state8(state):
    return jnp.pad(state, ((0, 0), (SUBLANE - (CONV_W - 1), 0), (0, 0)))


def _state_out(state8):
    return state8[:, SUBLANE - (CONV_W - 1):, :]


def _layer(x2d, nb, seq, attn_fn, conv8, ffn8, lp, lambda_init, tiles):
    qkv, k_f32, v_f32, bcx, gab = _in_proj(x2d, lp["norm1_g"], lp["w_in"], tiles["in"])
    o = attn_fn(qkv)
    hres, hn, conv_out8 = _merge(o, bcx, gab, x2d, conv8, lp["conv_w"], lp["w_proj_a"], lp["w_proj_b"],
                                 lp["w_o"], lp["norm2_g"], nb, seq, tiles["merge"])
    y, ffn_g8, ffn_v8 = _ffn(hn, hres, lp["w_up"], lp["ffn_conv_w"], ffn8, lp["w_down"], lp["final_g"],
                             nb, seq, tiles["ffn"])
    k = k_f32.reshape(nb, seq, N_HEADS, D_HEAD)
    v = v_f32.reshape(nb, seq, N_HEADS, D_HEAD)
    conv_state = _state_out(conv_out8)
    ffn_state = jnp.concatenate([_state_out(ffn_g8), _state_out(ffn_v8)], axis=-1)
    return y, k, v, conv_state, ffn_state


PROMPT_TILES = {"in": 512, "merge": 256, "ffn": 512}
SAMPLE_TILES = {"in": 512, "merge": 256, "ffn": 512}
ATTN_TILE_Q = 512


def kernel(x_prompt, x_sample, cache_k, cache_v, state_conv_mix, state_conv_ffn, rel_bias, norm1_g, w_in,
           lambda_q1, lambda_k1, lambda_q2, lambda_k2, subln_g, conv_w, w_proj_a, w_proj_b, w_o, norm2_g,
           w_up, ffn_conv_w, w_down, final_g):
    depth = w_in.shape[0]
    nbp, seq_p, _ = x_prompt.shape
    nbs, seq_s, _ = x_sample.shape
    past = cache_k.shape[2]
    tq = ATTN_TILE_Q

    bias_near, bias_diag, bias_cache, bias_new = _bias_tiles(
        rel_bias,
        ((tq, tq, tq, 0), (tq, tq, tq, tq), (seq_s, past, past, 0), (seq_s, seq_s, past, past)))

    hp = x_prompt.reshape(nbp * seq_p, D_MODEL)
    hs = x_sample.reshape(nbs * seq_s, D_MODEL)
    outs_p, outs_s = [], []
    for d in range(depth):
        assert depth == 1, "the ffn kernel ends with the final RMSNorm, so it is written for a single layer"
        lp = {
            "norm1_g": norm1_g[d], "w_in": w_in[d].astype(BF16), "conv_w": conv_w[d],
            "w_proj_a": w_proj_a[d].astype(BF16), "w_proj_b": w_proj_b[d].astype(BF16),
            "w_o": w_o[d].astype(BF16), "norm2_g": norm2_g[d], "w_up": w_up[d].astype(BF16),
            "ffn_conv_w": ffn_conv_w[d], "w_down": w_down[d].astype(BF16),
            "final_g": final_g,
        }
        li = _lambda_init(d)
        lam_vecs = tuple(a[d].reshape(1, D_QK) for a in (lambda_q1, lambda_k1, lambda_q2, lambda_k2))

        attn_p = lambda qkv: _attn_prompt(qkv, rel_bias, lam_vecs, subln_g[d], bias_near, bias_diag,
                                          nbp, seq_p, tq, li)
        zeros_mix = jnp.zeros((nbp, SUBLANE, WIDTH_B), F32)
        zeros_ffn = jnp.zeros((nbp, SUBLANE, 2 * D_FF), F32)
        hp, kp, vp, cmp_, cfp = _layer(hp, nbp, seq_p, attn_p, zeros_mix, zeros_ffn, lp, li, PROMPT_TILES)

        attn_s = lambda qkv: _attn_sample(qkv, cache_k, cache_v, d, lam_vecs, subln_g[d], bias_cache,
                                          bias_new, nbs, seq_s, past, li)
        hs, ks, vs, cms, cfs = _layer(hs, nbs, seq_s, attn_s, _state8(state_conv_mix[d]),
                                      _state8(state_conv_ffn[d]), lp, li, SAMPLE_TILES)
        outs_p.append((kp, vp, cmp_, cfp))
        outs_s.append((ks, vs, cms, cfs))

    stack = lambda outs, idx: jnp.stack([o[idx] for o in outs])
    return (hp.reshape(nbp, seq_p, D_MODEL), hs.reshape(nbs, seq_s, D_MODEL),
            stack(outs_p, 0), stack(outs_p, 1), stack(outs_p, 2), stack(outs_p, 3),
            stack(outs_s, 0), stack(outs_s, 1), stack(outs_s, 2), stack(outs_s, 3))
```

```python
import functools
import math

import jax
import jax.numpy as jnp
from jax import lax
from jax.experimental import pallas as pl
from jax.experimental.pallas import tpu as pltpu

D_MODEL = 2048
N_HEADS = 8
D_QK = 64
D_HEAD = 2 * D_QK
WIDTH_A = N_HEADS * D_HEAD
WIDTH_B = D_MODEL // 2
D_FF = 5632
CONV_W = 3
CHUNK = 64
N_BUCKETS = 32
MAX_DIST = 128
EPS = 1e-6
NEG_INF = -1e30

SUBLANE = 8
LANE = 128
VMEM_LIMIT_BYTES = 56 * 1024 * 1024

F32 = jnp.float32
BF16 = jnp.bfloat16


def _bucket_thresholds():
    nb = N_BUCKETS // 2
    max_exact = nb // 2
    steps = nb - max_exact
    out = []
    for k in range(1, steps):
        n = max_exact
        while n ** steps * max_exact ** k < MAX_DIST ** k * max_exact ** steps:
            n += 1
        out.append(n)
    return tuple(out)


_BUCKET_THRESHOLDS = _bucket_thresholds()
FAR_DIST = _BUCKET_THRESHOLDS[-1]
FAR_BUCKET = N_BUCKETS // 2 - 1
_CHUNK_SHIFT = CHUNK.bit_length() - 1
assert 1 << _CHUNK_SHIFT == CHUNK


def _lambda_init(layer_idx):
    return 0.8 - 0.6 * math.exp(-0.3 * layer_idx)


def _rms(x, g):
    return x * lax.rsqrt(jnp.mean(x * x, axis=-1, keepdims=True) + EPS) * g


def _params(n_axes):
    return pltpu.CompilerParams(
        dimension_semantics=("arbitrary",) * n_axes, vmem_limit_bytes=VMEM_LIMIT_BYTES)


def _bias_tile_kernel(specs, table_ref, *out_refs):
    h = pl.program_id(0)
    nb = N_BUCKETS // 2
    max_exact = nb // 2
    for (rows, cols, q0, k0), out_ref in zip(specs, out_refs):
        qpos = q0 + lax.broadcasted_iota(jnp.int32, (rows, cols), 0)
        kpos = k0 + lax.broadcasted_iota(jnp.int32, (rows, cols), 1)
        rel = kpos - qpos
        n = jnp.abs(rel)
        large = jnp.full((rows, cols), max_exact, jnp.int32)
        for thr in _BUCKET_THRESHOLDS:
            large = large + jnp.where(n >= thr, 1, 0)
        bucket = jnp.where(rel > 0, nb, 0) + jnp.where(n < max_exact, n, large)
        bias = jnp.zeros((rows, cols), F32)
        for b in range(N_BUCKETS):
            bias = jnp.where(bucket == b, table_ref[b, h], bias)
        visible = (lax.shift_right_logical(kpos, _CHUNK_SHIFT)
                   <= lax.shift_right_logical(qpos, _CHUNK_SHIFT))
        out_ref[...] = jnp.where(visible, bias, NEG_INF)


def _bias_tiles(rel_bias, specs):
    return pl.pallas_call(
        functools.partial(_bias_tile_kernel, specs),
        grid=(N_HEADS,),
        in_specs=[pl.BlockSpec(memory_space=pltpu.SMEM)],
        out_specs=[pl.BlockSpec((None, r, c), lambda h: (h, 0, 0)) for r, c, _, _ in specs],
        out_shape=[jax.ShapeDtypeStruct((N_HEADS, r, c), F32) for r, c, _, _ in specs],
        compiler_params=_params(1),
        name="bias_tiles",
    )(rel_bias)


IN_TILE_N = 1024
_N_QKV_TILES = 3
_N_BCX_TILES = 3
_N_GAB_TILES = 4
_N_IN_TILES = _N_QKV_TILES + _N_BCX_TILES + _N_GAB_TILES


def _in_proj_kernel(x_ref, g_ref, w_ref, qkv_ref, kf_ref, vf_ref, bcx_ref, gab_ref, xn_ref):
    j = pl.program_id(1)

    @pl.when(j == 0)
    def _():
        xn_ref[...] = _rms(x_ref[...], g_ref[...]).astype(BF16)

    y = jnp.dot(xn_ref[...], w_ref[...], preferred_element_type=F32)

    @pl.when(j == 0)
    def _():
        qkv_ref[...] = (y * (D_QK ** -0.5)).astype(BF16)

    @pl.when(j == 1)
    def _():
        qkv_ref[...] = y.astype(BF16)
        kf_ref[...] = y

    @pl.when(j == 2)
    def _():
        qkv_ref[...] = y.astype(BF16)
        vf_ref[...] = y

    @pl.when(jnp.logical_and(j >= _N_QKV_TILES, j < _N_QKV_TILES + _N_BCX_TILES))
    def _():
        bcx_ref[...] = y.astype(BF16)

    @pl.when(j >= _N_QKV_TILES + _N_BCX_TILES)
    def _():
        gab_ref[...] = y.astype(BF16)


def _in_proj(x2d, norm_g, w_in_bf, tm):
    t = x2d.shape[0]
    tn = IN_TILE_N
    assert t % tm == 0 and w_in_bf.shape == (D_MODEL, _N_IN_TILES * tn)
    return pl.pallas_call(
        _in_proj_kernel,
        grid=(t // tm, _N_IN_TILES),
        in_specs=[
            pl.BlockSpec((tm, D_MODEL), lambda i, j: (i, 0)),
            pl.BlockSpec((1, D_MODEL), lambda i, j: (0, 0)),
            pl.BlockSpec((D_MODEL, tn), lambda i, j: (0, j)),
        ],
        out_specs=[
            pl.BlockSpec((tm, tn), lambda i, j: (i, jnp.minimum(j, _N_QKV_TILES - 1))),
            pl.BlockSpec((tm, tn), lambda i, j: (i, 0)),
            pl.BlockSpec((tm, tn), lambda i, j: (i, 0)),
            pl.BlockSpec((tm, tn), lambda i, j: (i, jnp.clip(j - _N_QKV_TILES, 0, _N_BCX_TILES - 1))),
            pl.BlockSpec((tm, tn), lambda i, j: (i, jnp.maximum(j - _N_QKV_TILES - _N_BCX_TILES, 0))),
        ],
        out_shape=[
            jax.ShapeDtypeStruct((t, _N_QKV_TILES * tn), BF16),
            jax.ShapeDtypeStruct((t, tn), F32),
            jax.ShapeDtypeStruct((t, tn), F32),
            jax.ShapeDtypeStruct((t, _N_BCX_TILES * tn), BF16),
            jax.ShapeDtypeStruct((t, _N_GAB_TILES * tn), BF16),
        ],
        scratch_shapes=[pltpu.VMEM((tm, D_MODEL), BF16)],
        compiler_params=_params(2),
        name="in_proj",
    )(x2d, norm_g.reshape(1, D_MODEL), w_in_bf)


def _scores(qm, k):
    return lax.dot_general(qm, k, (((1,), (1,)), ((), ())), preferred_element_type=F32)


def _tile_lanes(x, n):
    return x if n == 1 else jnp.concatenate([x] * n, axis=1)


def _with_ones(v):
    return jnp.concatenate([v, jnp.ones_like(v)], axis=1)


def _softmax_step(s, v_ext, state, shift=None, init=False):
    m_ref, acc_ref = state
    rows, keys = s.shape
    row_max = jnp.max(s, axis=-1, keepdims=True)
    if shift is not None:
        row_max = row_max + shift
    if init:
        m_new = jnp.broadcast_to(row_max, (rows, LANE))
    else:
        m_old = m_ref[...]
        m_new = jnp.maximum(m_old, row_max)
    sub = m_new if shift is None else m_new - shift
    assert keys % LANE == 0 or keys < LANE
    p = jnp.exp(s - (_tile_lanes(sub, keys // LANE) if keys >= LANE else sub[:, :keys]))
    pv = jnp.dot(p.astype(BF16), v_ext, preferred_element_type=F32)
    if init:
        acc_ref[...] = pv
    else:
        alpha = jnp.exp(m_old - m_new)
        acc_ref[...] = _tile_lanes(alpha, 2 * D_HEAD // LANE) * acc_ref[...] + pv
    m_ref[...] = m_new


def _split_maps(q):
    lane = lax.broadcasted_iota(jnp.int32, q.shape, 1)
    zero = jnp.zeros_like(q)
    return jnp.where(lane < D_QK, q, zero), jnp.where(lane >= D_QK, q, zero)


def _lambda(lq1_ref, lk1_ref, lq2_ref, lk2_ref, lambda_init):
    s1 = jnp.sum(lq1_ref[...] * lk1_ref[...], axis=-1, keepdims=True)
    s2 = jnp.sum(lq2_ref[...] * lk2_ref[...], axis=-1, keepdims=True)
    return jnp.exp(s1) - jnp.exp(s2) + lambda_init


def _attn_finish(state1, state2, lam, g, lambda_init):
    a1 = state1[1][...]
    a2 = state2[1][...]
    o = a1[:, :D_HEAD] / a1[:, D_HEAD:] - lam * (a2[:, :D_HEAD] / a2[:, D_HEAD:])
    return _rms(o, g) * (1.0 - lambda_init)


def _attn_prompt_kernel(tq, nq, lambda_init, table_ref, lq1_ref, lk1_ref, lq2_ref, lk2_ref, g_ref,
                        q_ref, k_ref, v_ref, bnear_ref, bdiag_ref, o_ref,
                        m1_ref, a1_ref, m2_ref, a2_ref):
    h = pl.program_id(1)
    far_bias = table_ref[FAR_BUCKET, h]
    lam = _lambda(lq1_ref, lk1_ref, lq2_ref, lk2_ref, lambda_init)
    st1 = (m1_ref, a1_ref)
    st2 = (m2_ref, a2_ref)

    def q_tile(qi, carry):
        q0 = pl.multiple_of(qi * tq, tq)
        q1, q2 = _split_maps(q_ref[pl.ds(q0, tq), :])

        kd = k_ref[pl.ds(q0, tq), :]
        vd = _with_ones(v_ref[pl.ds(q0, tq), :])
        bd = bdiag_ref[...]
        _softmax_step(_scores(q1, kd) + bd, vd, st1, init=True)
        _softmax_step(_scores(q2, kd) + bd, vd, st2, init=True)

        @pl.when(qi > 0)
        def _():
            k0 = pl.multiple_of(q0 - tq, tq)
            kn = k_ref[pl.ds(k0, tq), :]
            vn = _with_ones(v_ref[pl.ds(k0, tq), :])
            bn = bnear_ref[...]
            _softmax_step(_scores(q1, kn) + bn, vn, st1)
            _softmax_step(_scores(q2, kn) + bn, vn, st2)

        def far(j, c):
            k0 = pl.multiple_of(j * tq, tq)
            kf = k_ref[pl.ds(k0, tq), :]
            vf = _with_ones(v_ref[pl.ds(k0, tq), :])
            _softmax_step(_scores(q1, kf), vf, st1, shift=far_bias)
            _softmax_step(_scores(q2, kf), vf, st2, shift=far_bias)
            return c

        lax.fori_loop(0, jnp.maximum(qi - 1, 0), far, 0)

        o = _attn_finish(st1, st2, lam, g_ref[...], lambda_init)
        o_ref[pl.ds(q0, tq), :] = o.astype(BF16)
        return carry

    lax.fori_loop(0, nq, q_tile, 0)


def _attn_prompt(qkv, rel_bias, lam_vecs, subln_g, bias_near, bias_diag, nb, seq, tq, lambda_init):
    assert seq % tq == 0 and tq % CHUNK == 0 and tq >= FAR_DIST
    qkv3 = qkv.reshape(nb, seq, 3 * WIDTH_A)
    vec = pl.BlockSpec((1, D_QK), lambda b, h: (0, 0))
    head_cols = lambda off: pl.BlockSpec((None, seq, D_HEAD), lambda b, h: (b, 0, off + h))
    tile = pl.BlockSpec((None, tq, tq), lambda b, h: (h, 0, 0))
    stat = pltpu.VMEM((tq, LANE), F32)
    acc = pltpu.VMEM((tq, 2 * D_HEAD), F32)
    o = pl.pallas_call(
        functools.partial(_attn_prompt_kernel, tq, seq // tq, lambda_init),
        grid=(nb, N_HEADS),
        in_specs=[pl.BlockSpec(memory_space=pltpu.SMEM), vec, vec, vec, vec,
                  pl.BlockSpec((1, D_HEAD), lambda b, h: (0, 0)),
                  head_cols(0), head_cols(N_HEADS), head_cols(2 * N_HEADS), tile, tile],
        out_specs=pl.BlockSpec((None, seq, D_HEAD), lambda b, h: (b, 0, h)),
        out_shape=jax.ShapeDtypeStruct((nb, seq, WIDTH_A), BF16),
        scratch_shapes=[stat, acc, stat, acc],
        compiler_params=_params(2),
        name="attn_prompt",
    )(rel_bias, *lam_vecs, subln_g.reshape(1, D_HEAD), qkv3, qkv3, qkv3, bias_near, bias_diag)
    return o.reshape(nb * seq, WIDTH_A)


def _attn_sample_kernel(lambda_init, lq1_ref, lk1_ref, lq2_ref, lk2_ref, g_ref,
                        q_ref, kn_ref, vn_ref, kc_ref, vc_ref, bc_ref, bn_ref, o_ref,
                        m1_ref, a1_ref, m2_ref, a2_ref):
    lam = _lambda(lq1_ref, lk1_ref, lq2_ref, lk2_ref, lambda_init)
    st1 = (m1_ref, a1_ref)
    st2 = (m2_ref, a2_ref)
    q1, q2 = _split_maps(q_ref[...])
    kc = kc_ref[...].astype(BF16)
    vc = _with_ones(vc_ref[...].astype(BF16))
    bc = bc_ref[...]
    _softmax_step(_scores(q1, kc) + bc, vc, st1, init=True)
    _softmax_step(_scores(q2, kc) + bc, vc, st2, init=True)
    kn = kn_ref[...]
    vn = _with_ones(vn_ref[...])
    bn = bn_ref[...]
    _softmax_step(_scores(q1, kn) + bn, vn, st1)
    _softmax_step(_scores(q2, kn) + bn, vn, st2)
    o_ref[...] = _attn_finish(st1, st2, lam, g_ref[...], lambda_init).astype(BF16)


def _attn_sample(qkv, cache_k, cache_v, layer, lam_vecs, subln_g, bias_cache, bias_new, nb, seq, past,
                 lambda_init):
    qkv3 = qkv.reshape(nb, seq, 3 * WIDTH_A)
    kc = cache_k.reshape(-1, past, WIDTH_A)
    vc = cache_v.reshape(-1, past, WIDTH_A)
    vec = pl.BlockSpec((1, D_QK), lambda b, h: (0, 0))
    new_cols = lambda off: pl.BlockSpec((None, seq, D_HEAD), lambda b, h: (b, 0, off + h))
    past_cols = pl.BlockSpec((None, past, D_HEAD), lambda b, h: (layer * nb + b, 0, h))
    stat = pltpu.VMEM((seq, LANE), F32)
    acc = pltpu.VMEM((seq, 2 * D_HEAD), F32)
    o = pl.pallas_call(
        functools.partial(_attn_sample_kernel, lambda_init),
        grid=(nb, N_HEADS),
        in_specs=[vec, vec, vec, vec, pl.BlockSpec((1, D_HEAD), lambda b, h: (0, 0)),
                  new_cols(0), new_cols(N_HEADS), new_cols(2 * N_HEADS), past_cols, past_cols,
                  pl.BlockSpec((None, seq, past), lambda b, h: (h, 0, 0)),
                  pl.BlockSpec((None, seq, seq), lambda b, h: (h, 0, 0))],
        out_specs=pl.BlockSpec((None, seq, D_HEAD), lambda b, h: (b, 0, h)),
        out_shape=jax.ShapeDtypeStruct((nb, seq, WIDTH_A), BF16),
        scratch_shapes=[stat, acc, stat, acc],
        compiler_params=_params(2),
        name="attn_sample",
    )(*lam_vecs, subln_g.reshape(1, D_HEAD), qkv3, qkv3, qkv3, kc, vc, bias_cache, bias_new)
    return o.reshape(nb * seq, WIDTH_A)


def _causal_conv3(x3, prev8, w):
    p0 = prev8[:, SUBLANE - 2:SUBLANE - 1, :]
    p1 = prev8[:, SUBLANE - 1:SUBLANE, :]
    row = lax.broadcasted_iota(jnp.int32, x3.shape, 1)
    x1 = jnp.where(row == 0, p1, pltpu.roll(x3, 1, axis=1))
    x2 = jnp.where(row == 0, p0, jnp.where(row == 1, p1, pltpu.roll(x3, 2, axis=1)))
    return x2 * w[0:1, :] + x1 * w[1:2, :] + x3 * w[2:3, :]


def _conv_with_state(x, i, nseq, tps, st_ref, so_ref, halo):
    rows = x.shape[0] // nseq
    x3 = x.reshape(nseq, rows, x.shape[1])
    last8 = x3[:, rows - SUBLANE:, :]
    if tps == 1:
        prev8 = st_ref[...]
    else:
        @pl.when(i % tps == 0)
        def _():
            halo[...] = st_ref[...]

        prev8 = halo[...]
        halo[...] = last8
    so_ref[...] = last8
    return x3, prev8


def _merge_kernel(nseq, tps, o_ref, b_ref, c_ref, xi_ref, ga_ref, gb_ref, x_ref, st_ref, cw_ref,
                  wa_ref, wb_ref, wo_ref, g2_ref, h_ref, hn_ref, so_ref, *scratch):
    i = pl.program_id(0)
    u = c_ref[...].astype(F32) * xi_ref[...].astype(F32)
    u3, prev8 = _conv_with_state(u, i, nseq, tps, st_ref, so_ref, scratch[0] if tps > 1 else None)
    z = _causal_conv3(u3, prev8, cw_ref[...]).reshape(u.shape)
    ob = (b_ref[...].astype(F32) * z).astype(BF16)
    ya = jnp.dot(o_ref[...], wa_ref[...], preferred_element_type=F32)
    yb = jnp.dot(ob, wb_ref[...], preferred_element_type=F32)
    merged = jax.nn.sigmoid(ga_ref[...].astype(F32)) * ya + jax.nn.sigmoid(gb_ref[...].astype(F32)) * yb
    hres = x_ref[...] + jnp.dot(merged.astype(BF16), wo_ref[...], preferred_element_type=F32)
    h_ref[...] = hres
    hn_ref[...] = _rms(hres, g2_ref[...]).astype(BF16)


def _merge(o, bcx, gab, x2d, state8, conv_w, wa, wb, wo, norm2_g, nb, seq, rows):
    t = x2d.shape[0]
    nseq = max(rows // seq, 1)
    tps = max(seq // rows, 1)
    assert t % rows == 0 and (rows % seq == 0 or seq % rows == 0)
    tok = lambda width, col: pl.BlockSpec((rows, width), lambda i: (i, col))
    full = lambda a: pl.BlockSpec(a.shape, lambda i: (0,) * a.ndim)
    state_spec = pl.BlockSpec((nseq, SUBLANE, WIDTH_B), lambda i: (i // tps, 0, 0))
    scratch = [pltpu.VMEM((nseq, SUBLANE, WIDTH_B), F32)] if tps > 1 else []
    g2 = norm2_g.reshape(1, D_MODEL)
    return pl.pallas_call(
        functools.partial(_merge_kernel, nseq, tps),
        grid=(t // rows,),
        in_specs=[tok(WIDTH_A, 0), tok(WIDTH_B, 0), tok(WIDTH_B, 1), tok(WIDTH_B, 2),
                  tok(D_MODEL, 0), tok(D_MODEL, 1), tok(D_MODEL, 0), state_spec,
                  full(conv_w), full(wa), full(wb), full(wo), full(g2)],
        out_specs=[tok(D_MODEL, 0), tok(D_MODEL, 0), state_spec],
        out_shape=[jax.ShapeDtypeStruct((t, D_MODEL), F32),
                   jax.ShapeDtypeStruct((t, D_MODEL), BF16),
                   jax.ShapeDtypeStruct((nb, SUBLANE, WIDTH_B), F32)],
        scratch_shapes=scratch,
        compiler_params=_params(1),
        name="merge",
    )(o, bcx, bcx, bcx, gab, gab, x2d, state8, conv_w, wa, wb, wo, g2)


FFN_TILE_F = 512
FFN_SUB_TILES = 2


def _ffn_kernel(nseq, tps, nf, hn_ref, h_ref, wg_ref, wv_ref, cwg_ref, cwv_ref, stg_ref, stv_ref,
                wd_ref, fg_ref, y_ref, sog_ref, sov_ref, acc_ref, *scratch):
    i = pl.program_id(0)
    f = pl.program_id(1)
    rows = hn_ref.shape[0]
    sub = rows // FFN_SUB_TILES
    branches = ((wg_ref, cwg_ref, stg_ref, sog_ref), (wv_ref, cwv_ref, stv_ref, sov_ref))

    @pl.when(jnp.logical_and(i == 0, f == 0))
    def _():
        acc_ref[...] = jnp.zeros_like(acc_ref)

    if tps > 1:
        halos = [scratch[0].at[2 * f + slot] for slot in range(2)]

        @pl.when(i % tps == 0)
        def _():
            for halo, (_, _, st_ref, _) in zip(halos, branches):
                halo[...] = st_ref[...]

        prev = [halo[...] for halo in halos]
    else:
        assert sub % (rows // nseq) == 0
        seqs = nseq // FFN_SUB_TILES

    for r in range(FFN_SUB_TILES):
        hn = hn_ref[r * sub:(r + 1) * sub, :]
        convs = []
        for slot, (w_ref, cw_ref, st_ref, so_ref) in enumerate(branches):
            up = jnp.dot(hn, w_ref[...], preferred_element_type=F32)
            if tps > 1:
                up3 = up.reshape(1, sub, up.shape[1])
                prev8 = prev[slot]
                prev[slot] = up3[:, sub - SUBLANE:, :]
            else:
                up3 = up.reshape(seqs, sub // seqs, up.shape[1])
                prev8 = st_ref[r * seqs:(r + 1) * seqs]
                so_ref[r * seqs:(r + 1) * seqs] = up3[:, sub // seqs - SUBLANE:, :]
            convs.append(_causal_conv3(up3, prev8, cw_ref[...]).reshape(up.shape))
        act = (jax.nn.silu(convs[0]) * convs[1]).astype(BF16)
        part = jnp.dot(act, wd_ref[...], preferred_element_type=F32)
        acc_rows = acc_ref.at[r * sub:(r + 1) * sub, :]
        acc_rows[...] = jnp.where(f == 0, part, acc_rows[...] + part)

    if tps > 1:
        for halo, last8, (_, _, _, so_ref) in zip(halos, prev, branches):
            halo[...] = last8
            so_ref[...] = last8

    @pl.when(f == nf - 1)
    def _():
        y_ref[...] = _rms(h_ref[...] + acc_ref[...], fg_ref[...])


def _ffn(hn, hres, w_up, ffn_conv_w, state8, w_down, final_g, nb, seq, rows):
    t = hn.shape[0]
    tf = FFN_TILE_F
    nf = D_FF // tf
    nseq = max(rows // seq, 1)
    tps = max(seq // rows, 1)
    assert D_FF % tf == 0 and t % rows == 0 and (rows % seq == 0 or seq % rows == 0)
    tok = pl.BlockSpec((rows, D_MODEL), lambda i, f: (i, 0))
    gate_cols = lambda r: pl.BlockSpec((r, tf), lambda i, f: (0, f))
    val_cols = lambda r: pl.BlockSpec((r, tf), lambda i, f: (0, nf + f))
    st_gate = pl.BlockSpec((nseq, SUBLANE, tf), lambda i, f: (i // tps, 0, f))
    st_val = pl.BlockSpec((nseq, SUBLANE, tf), lambda i, f: (i // tps, 0, nf + f))
    scratch = [pltpu.VMEM((rows, D_MODEL), F32)]
    if tps > 1:
        scratch.append(pltpu.VMEM((2 * nf, nseq, SUBLANE, tf), F32))
    tile_rows = pl.BlockSpec((nseq, SUBLANE, tf), lambda i, f: (i, 0, f))
    fg = final_g.reshape(1, D_MODEL)
    y, tail_g, tail_v = pl.pallas_call(
        functools.partial(_ffn_kernel, nseq, tps, nf),
        grid=(t // rows, nf),
        in_specs=[tok, tok, gate_cols(D_MODEL), val_cols(D_MODEL), gate_cols(CONV_W), val_cols(CONV_W),
                  st_gate, st_val,
                  pl.BlockSpec((tf, D_MODEL), lambda i, f: (f, 0)),
                  pl.BlockSpec((1, D_MODEL), lambda i, f: (0, 0))],
        out_specs=[tok, tile_rows, tile_rows],
        out_shape=[jax.ShapeDtypeStruct((t, D_MODEL), F32),
                   jax.ShapeDtypeStruct((nb * tps, SUBLANE, D_FF), F32),
                   jax.ShapeDtypeStruct((nb * tps, SUBLANE, D_FF), F32)],
        scratch_shapes=scratch,
        compiler_params=_params(2),
        name="ffn",
    )(hn, hres, w_up, w_up, ffn_conv_w, ffn_conv_w, state8, state8, w_down, fg)
    return y, tail_g[tps - 1::tps], tail_v[tps - 1::tps]


def _state8(state):
    return jnp.pad(state, ((0, 0), (SUBLANE - (CONV_W - 1), 0), (0, 0)))


def _state_out(state8):
    return state8[:, SUBLANE - (CONV_W - 1):, :]


def _layer(x2d, nb, seq, attn_fn, conv8, ffn8, lp, lambda_init, tiles):
    qkv, k_f32, v_f32, bcx, gab = _in_proj(x2d, lp["norm1_g"], lp["w_in"], tiles["in"])
    o = attn_fn(qkv)
    hres, hn, conv_out8 = _merge(o, bcx, gab, x2d, conv8, lp["conv_w"], lp["w_proj_a"], lp["w_proj_b"],
                                 lp["w_o"], lp["norm2_g"], nb, seq, tiles["merge"])
    y, ffn_g8, ffn_v8 = _ffn(hn, hres, lp["w_up"], lp["ffn_conv_w"], ffn8, lp["w_down"], lp["final_g"],
                             nb, seq, tiles["ffn"])
    k = k_f32.reshape(nb, seq, N_HEADS, D_HEAD)
    v = v_f32.reshape(nb, seq, N_HEADS, D_HEAD)
    conv_state = _state_out(conv_out8)
    ffn_state = jnp.concatenate([_state_out(ffn_g8), _state_out(ffn_v8)], axis=-1)
    return y, k, v, conv_state, ffn_state


PROMPT_TILES = {"in": 512, "merge": 256, "ffn": 512}
SAMPLE_TILES = {"in": 512, "merge": 256, "ffn": 512}
ATTN_TILE_Q = 512


def kernel(x_prompt, x_sample, cache_k, cache_v, state_conv_mix, state_conv_ffn, rel_bias, norm1_g, w_in,
           lambda_q1, lambda_k1, lambda_q2, lambda_k2, subln_g, conv_w, w_proj_a, w_proj_b, w_o, norm2_g,
           w_up, ffn_conv_w, w_down, final_g):
    depth = w_in.shape[0]
    nbp, seq_p, _ = x_prompt.shape
    nbs, seq_s, _ = x_sample.shape
    past = cache_k.shape[2]
    tq = ATTN_TILE_Q

    bias_near, bias_diag, bias_cache, bias_new = _bias_tiles(
        rel_bias,
        ((tq, tq, tq, 0), (tq, tq, tq, tq), (seq_s, past, past, 0), (seq_s, seq_s, past, past)))

    hp = x_prompt.reshape(nbp * seq_p, D_MODEL)
    hs = x_sample.reshape(nbs * seq_s, D_MODEL)
    outs_p, outs_s = [], []
    for d in range(depth):
        assert depth == 1, "the ffn kernel ends with the final RMSNorm, so it is written for a single layer"
        lp = {
            "norm1_g": norm1_g[d], "w_in": w_in[d].astype(BF16), "conv_w": conv_w[d],
            "w_proj_a": w_proj_a[d].astype(BF16), "w_proj_b": w_proj_b[d].astype(BF16),
            "w_o": w_o[d].astype(BF16), "norm2_g": norm2_g[d], "w_up": w_up[d].astype(BF16),
            "ffn_conv_w": ffn_conv_w[d], "w_down": w_down[d].astype(BF16),
            "final_g": final_g,
        }
        li = _lambda_init(d)
        lam_vecs = tuple(a[d].reshape(1, D_QK) for a in (lambda_q1, lambda_k1, lambda_q2, lambda_k2))

        attn_p = lambda qkv: _attn_prompt(qkv, rel_bias, lam_vecs, subln_g[d], bias_near, bias_diag,
                                          nbp, seq_p, tq, li)
        zeros_mix = jnp.zeros((nbp, SUBLANE, WIDTH_B), F32)
        zeros_ffn = jnp.zeros((nbp, SUBLANE, 2 * D_FF), F32)
        hp, kp, vp, cmp_, cfp = _layer(hp, nbp, seq_p, attn_p, zeros_mix, zeros_ffn, lp, li, PROMPT_TILES)

        attn_s = lambda qkv: _attn_sample(qkv, cache_k, cache_v, d, lam_vecs, subln_g[d], bias_cache,
                                          bias_new, nbs, seq_s, past, li)
        hs, ks, vs, cms, cfs = _layer(hs, nbs, seq_s, attn_s, _state8(state_conv_mix[d]),
                                      _state8(state_conv_ffn[d]), lp, li, SAMPLE_TILES)
        outs_p.append((kp, vp, cmp_, cfp))
        outs_s.append((ks, vs, cms, cfs))

    stack = lambda outs, idx: jnp.stack([o[idx] for o in outs])
    return (hp.reshape(nbp, seq_p, D_MODEL), hs.reshape(nbs, seq_s, D_MODEL),
            stack(outs_p, 0), stack(outs_p, 1), stack(outs_p, 2), stack(outs_p, 3),
            stack(outs_s, 0), stack(outs_s, 1), stack(outs_s, 2), stack(outs_s, 3))
```

```python
import functools
import math

import jax
import jax.numpy as jnp
from jax import lax
from jax.experimental import pallas as pl
from jax.experimental.pallas import tpu as pltpu

D_MODEL = 2048
N_HEADS = 8
D_QK = 64
D_HEAD = 2 * D_QK
WIDTH_A = N_HEADS * D_HEAD
WIDTH_B = D_MODEL // 2
D_FF = 5632
CONV_W = 3
CHUNK = 64
N_BUCKETS = 32
MAX_DIST = 128
EPS = 1e-6
NEG_INF = -1e30

SUBLANE = 8
LANE = 128
VMEM_LIMIT_BYTES = 56 * 1024 * 1024

F32 = jnp.float32
BF16 = jnp.bfloat16


def _bucket_thresholds():
    nb = N_BUCKETS // 2
    max_exact = nb // 2
    steps = nb - max_exact
    out = []
    for k in range(1, steps):
        n = max_exact
        while n ** steps * max_exact ** k < MAX_DIST ** k * max_exact ** steps:
            n += 1
        out.append(n)
    return tuple(out)


_BUCKET_THRESHOLDS = _bucket_thresholds()
FAR_DIST = _BUCKET_THRESHOLDS[-1]
FAR_BUCKET = N_BUCKETS // 2 - 1
_CHUNK_SHIFT = CHUNK.bit_length() - 1
assert 1 << _CHUNK_SHIFT == CHUNK


def _lambda_init(layer_idx):
    return 0.8 - 0.6 * math.exp(-0.3 * layer_idx)


def _rms(x, g):
    return x * lax.rsqrt(jnp.mean(x * x, axis=-1, keepdims=True) + EPS) * g


def _params(n_axes):
    return pltpu.CompilerParams(
        dimension_semantics=("arbitrary",) * n_axes, vmem_limit_bytes=VMEM_LIMIT_BYTES)


def _bias_tile_kernel(specs, table_ref, *out_refs):
    h = pl.program_id(0)
    nb = N_BUCKETS // 2
    max_exact = nb // 2
    for (rows, cols, q0, k0), out_ref in zip(specs, out_refs):
        qpos = q0 + lax.broadcasted_iota(jnp.int32, (rows, cols), 0)
        kpos = k0 + lax.broadcasted_iota(jnp.int32, (rows, cols), 1)
        rel = kpos - qpos
        n = jnp.abs(rel)
        large = jnp.full((rows, cols), max_exact, jnp.int32)
        for thr in _BUCKET_THRESHOLDS:
            large = large + jnp.where(n >= thr, 1, 0)
        bucket = jnp.where(rel > 0, nb, 0) + jnp.where(n < max_exact, n, large)
        bias = jnp.zeros((rows, cols), F32)
        for b in range(N_BUCKETS):
            bias = jnp.where(bucket == b, table_ref[b, h], bias)
        visible = (lax.shift_right_logical(kpos, _CHUNK_SHIFT)
                   <= lax.shift_right_logical(qpos, _CHUNK_SHIFT))
        out_ref[...] = jnp.where(visible, bias, NEG_INF)


def _bias_tiles(rel_bias, specs):
    return pl.pallas_call(
        functools.partial(_bias_tile_kernel, specs),
        grid=(N_HEADS,),
        in_specs=[pl.BlockSpec(memory_space=pltpu.SMEM)],
        out_specs=[pl.BlockSpec((None, r, c), lambda h: (h, 0, 0)) for r, c, _, _ in specs],
        out_shape=[jax.ShapeDtypeStruct((N_HEADS, r, c), F32) for r, c, _, _ in specs],
        compiler_params=_params(1),
        name="bias_tiles",
    )(rel_bias)


IN_TILE_N = 1024
_N_QKV_TILES = 3
_N_BCX_TILES = 3
_N_GAB_TILES = 4
_N_IN_TILES = _N_QKV_TILES + _N_BCX_TILES + _N_GAB_TILES


def _in_proj_kernel(x_ref, g_ref, w_ref, qkv_ref, kf_ref, vf_ref, bcx_ref, gab_ref, xn_ref):
    j = pl.program_id(1)

    @pl.when(j == 0)
    def _():
        xn_ref[...] = _rms(x_ref[...], g_ref[...]).astype(BF16)

    y = jnp.dot(xn_ref[...], w_ref[...], preferred_element_type=F32)

    @pl.when(j == 0)
    def _():
        qkv_ref[...] = (y * (D_QK ** -0.5)).astype(BF16)

    @pl.when(j == 1)
    def _():
        qkv_ref[...] = y.astype(BF16)
        kf_ref[...] = y

    @pl.when(j == 2)
    def _():
        qkv_ref[...] = y.astype(BF16)
        vf_ref[...] = y

    @pl.when(jnp.logical_and(j >= _N_QKV_TILES, j < _N_QKV_TILES + _N_BCX_TILES))
    def _():
        bcx_ref[...] = y.astype(BF16)

    @pl.when(j >= _N_QKV_TILES + _N_BCX_TILES)
    def _():
        gab_ref[...] = y.astype(BF16)


def _in_proj(x2d, norm_g, w_in_bf, tm):
    t = x2d.shape[0]
    tn = IN_TILE_N
    assert t % tm == 0 and w_in_bf.shape == (D_MODEL, _N_IN_TILES * tn)
    return pl.pallas_call(
        _in_proj_kernel,
        grid=(t // tm, _N_IN_TILES),
        in_specs=[
            pl.BlockSpec((tm, D_MODEL), lambda i, j: (i, 0)),
            pl.BlockSpec((1, D_MODEL), lambda i, j: (0, 0)),
            pl.BlockSpec((D_MODEL, tn), lambda i, j: (0, j)),
        ],
        out_specs=[
            pl.BlockSpec((tm, tn), lambda i, j: (i, jnp.minimum(j, _N_QKV_TILES - 1))),
            pl.BlockSpec((tm, tn), lambda i, j: (i, 0)),
            pl.BlockSpec((tm, tn), lambda i, j: (i, 0)),
            pl.BlockSpec((tm, tn), lambda i, j: (i, jnp.clip(j - _N_QKV_TILES, 0, _N_BCX_TILES - 1))),
            pl.BlockSpec((tm, tn), lambda i, j: (i, jnp.maximum(j - _N_QKV_TILES - _N_BCX_TILES, 0))),
        ],
        out_shape=[
            jax.ShapeDtypeStruct((t, _N_QKV_TILES * tn), BF16),
            jax.ShapeDtypeStruct((t, tn), F32),
            jax.ShapeDtypeStruct((t, tn), F32),
            jax.ShapeDtypeStruct((t, _N_BCX_TILES * tn), BF16),
            jax.ShapeDtypeStruct((t, _N_GAB_TILES * tn), BF16),
        ],
        scratch_shapes=[pltpu.VMEM((tm, D_MODEL), BF16)],
        compiler_params=_params(2),
        name="in_proj",
    )(x2d, norm_g.reshape(1, D_MODEL), w_in_bf)


def _scores(qm, k):
    return lax.dot_general(qm, k, (((1,), (1,)), ((), ())), preferred_element_type=F32)


def _tile_lanes(x, n):
    return x if n == 1 else jnp.concatenate([x] * n, axis=1)


def _with_ones(v):
    return jnp.concatenate([v, jnp.ones_like(v)], axis=1)


def _softmax_step(s, v_ext, state, shift=None):
    m_ref, acc_ref = state
    keys = s.shape[1]
    assert keys % LANE == 0
    row_max = jnp.max(s, axis=-1, keepdims=True)
    if shift is not None:
        row_max = row_max + shift
    m_old = m_ref[...]
    m_new = jnp.maximum(m_old, row_max)
    sub = m_new if shift is None else m_new - shift
    p = jnp.exp(s - _tile_lanes(sub, keys // LANE))
    pv = jnp.dot(p.astype(BF16), v_ext, preferred_element_type=F32)
    alpha = jnp.exp(m_old - m_new)
    acc_ref[...] = _tile_lanes(alpha, 2 * D_HEAD // LANE) * acc_ref[...] + pv
    m_ref[...] = m_new


def _split_maps(q):
    lane = lax.broadcasted_iota(jnp.int32, q.shape, 1)
    zero = jnp.zeros_like(q)
    return jnp.where(lane < D_QK, q, zero), jnp.where(lane >= D_QK, q, zero)


def _lambda(lq1_ref, lk1_ref, lq2_ref, lk2_ref, lambda_init):
    s1 = jnp.sum(lq1_ref[...] * lk1_ref[...], axis=-1, keepdims=True)
    s2 = jnp.sum(lq2_ref[...] * lk2_ref[...], axis=-1, keepdims=True)
    return jnp.exp(s1) - jnp.exp(s2) + lambda_init


def _attn_finish(state1, state2, lam, g, lambda_init):
    a1 = state1[1][...]
    a2 = state2[1][...]
    o = a1[:, :D_HEAD] / a1[:, D_HEAD:] - lam * (a2[:, :D_HEAD] / a2[:, D_HEAD:])
    return _rms(o, g) * (1.0 - lambda_init)


def _attn_prompt_kernel(tq, nq, lambda_init, table_ref, lq1_ref, lk1_ref, lq2_ref, lk2_ref, g_ref,
                        q_ref, k_ref, v_ref, bnear_ref, bdiag_ref, o_ref,
                        m1_ref, a1_ref, m2_ref, a2_ref, s1_ref, s2_ref):
    h = pl.program_id(1)
    far_bias = table_ref[FAR_BUCKET, h]
    lam = _lambda(lq1_ref, lk1_ref, lq2_ref, lk2_ref, lambda_init)
    st1 = (m1_ref, a1_ref)
    st2 = (m2_ref, a2_ref)

    def q_tile(qi, carry):
        q0 = pl.multiple_of(qi * tq, tq)
        q1, q2 = _split_maps(q_ref[pl.ds(q0, tq), :])

        def score_block(j):
            k = k_ref[pl.ds(pl.multiple_of(j * tq, tq), tq), :]
            s1_ref[...] = _scores(q1, k)
            s2_ref[...] = _scores(q2, k)

        def consume(j, bias=None, shift=None):
            v = _with_ones(v_ref[pl.ds(pl.multiple_of(j * tq, tq), tq), :])
            s1 = s1_ref[...]
            s2 = s2_ref[...]
            if bias is not None:
                s1 = s1 + bias
                s2 = s2 + bias
            _softmax_step(s1, v, st1, shift=shift)
            _softmax_step(s2, v, st2, shift=shift)

        for m_ref, a_ref in (st1, st2):
            m_ref[...] = jnp.full(m_ref.shape, NEG_INF, F32)
            a_ref[...] = jnp.zeros(a_ref.shape, F32)
        score_block(0)

        def far(j, c):
            consume(j, shift=far_bias)
            score_block(j + 1)
            return c

        lax.fori_loop(0, jnp.maximum(qi - 1, 0), far, 0)

        @pl.when(qi > 0)
        def _():
            consume(qi - 1, bias=bnear_ref[...])
            score_block(qi)

        consume(qi, bias=bdiag_ref[...])
        o = _attn_finish(st1, st2, lam, g_ref[...], lambda_init)
        o_ref[pl.ds(q0, tq), :] = o.astype(BF16)
        return carry

    lax.fori_loop(0, nq, q_tile, 0)


def _attn_prompt(qkv, rel_bias, lam_vecs, subln_g, bias_near, bias_diag, nb, seq, tq, lambda_init):
    assert seq % tq == 0 and tq % CHUNK == 0 and tq >= FAR_DIST
    qkv3 = qkv.reshape(nb, seq, 3 * WIDTH_A)
    vec = pl.BlockSpec((1, D_QK), lambda b, h: (0, 0))
    head_cols = lambda off: pl.BlockSpec((None, seq, D_HEAD), lambda b, h: (b, 0, off + h))
    tile = pl.BlockSpec((None, tq, tq), lambda b, h: (h, 0, 0))
    stat = pltpu.VMEM((tq, LANE), F32)
    acc = pltpu.VMEM((tq, 2 * D_HEAD), F32)
    o = pl.pallas_call(
        functools.partial(_attn_prompt_kernel, tq, seq // tq, lambda_init),
        grid=(nb, N_HEADS),
        in_specs=[pl.BlockSpec(memory_space=pltpu.SMEM), vec, vec, vec, vec,
                  pl.BlockSpec((1, D_HEAD), lambda b, h: (0, 0)),
                  head_cols(0), head_cols(N_HEADS), head_cols(2 * N_HEADS), tile, tile],
        out_specs=pl.BlockSpec((None, seq, D_HEAD), lambda b, h: (b, 0, h)),
        out_shape=jax.ShapeDtypeStruct((nb, seq, WIDTH_A), BF16),
        scratch_shapes=[stat, acc, stat, acc, pltpu.VMEM((tq, tq), F32), pltpu.VMEM((tq, tq), F32)],
        compiler_params=_params(2),
        name="attn_prompt",
    )(rel_bias, *lam_vecs, subln_g.reshape(1, D_HEAD), qkv3, qkv3, qkv3, bias_near, bias_diag)
    return o.reshape(nb * seq, WIDTH_A)


def _softmax_two_blocks(qm, k_a, v_a, bias_a, k_b, v_b, bias_b):
    s_a = _scores(qm, k_a) + bias_a
    s_b = _scores(qm, k_b) + bias_b
    m = jnp.maximum(jnp.max(s_a, axis=-1, keepdims=True), jnp.max(s_b, axis=-1, keepdims=True))
    p_a = jnp.exp(s_a - m).astype(BF16)
    p_b = jnp.exp(s_b - m).astype(BF16)
    return jnp.dot(p_a, v_a, preferred_element_type=F32) + jnp.dot(p_b, v_b, preferred_element_type=F32)


def _attn_sample_kernel(lambda_init, past, lq1_ref, lk1_ref, lq2_ref, lk2_ref, g_ref,
                        qkv_ref, kc_ref, vc_ref, bc_ref, bn_ref, o_ref):
    lam = _lambda(lq1_ref, lk1_ref, lq2_ref, lk2_ref, lambda_init)
    for h in range(N_HEADS):
        cols = lambda part: slice((part * N_HEADS + h) * D_HEAD, (part * N_HEADS + h + 1) * D_HEAD)
        q1, q2 = _split_maps(qkv_ref[:, cols(0)])
        kn = qkv_ref[:, cols(1)]
        vn = _with_ones(qkv_ref[:, cols(2)])
        kc = kc_ref[pl.ds(h, past, stride=N_HEADS), :].astype(BF16)
        vc = _with_ones(vc_ref[pl.ds(h, past, stride=N_HEADS), :].astype(BF16))
        a1 = _softmax_two_blocks(q1, kc, vc, bc_ref[h], kn, vn, bn_ref[h])
        a2 = _softmax_two_blocks(q2, kc, vc, bc_ref[h], kn, vn, bn_ref[h])
        o = a1[:, :D_HEAD] / a1[:, D_HEAD:] - lam * (a2[:, :D_HEAD] / a2[:, D_HEAD:])
        o_ref[:, h * D_HEAD:(h + 1) * D_HEAD] = (_rms(o, g_ref[...]) * (1.0 - lambda_init)).astype(BF16)


def _attn_sample(qkv, cache_k, cache_v, layer, lam_vecs, subln_g, bias_cache, bias_new, nb, seq, past,
                 lambda_init):
    qkv3 = qkv.reshape(nb, seq, 3 * WIDTH_A)
    kc = cache_k.reshape(-1, past * N_HEADS, D_HEAD)
    vc = cache_v.reshape(-1, past * N_HEADS, D_HEAD)
    vec = pl.BlockSpec((1, D_QK), lambda b: (0, 0))
    past_rows = pl.BlockSpec((None, past * N_HEADS, D_HEAD), lambda b: (layer * nb + b, 0, 0))
    full = lambda a: pl.BlockSpec(a.shape, lambda b: (0,) * a.ndim)
    o = pl.pallas_call(
        functools.partial(_attn_sample_kernel, lambda_init, past),
        grid=(nb,),
        in_specs=[vec, vec, vec, vec, pl.BlockSpec((1, D_HEAD), lambda b: (0, 0)),
                  pl.BlockSpec((None, seq, 3 * WIDTH_A), lambda b: (b, 0, 0)), past_rows, past_rows,
                  full(bias_cache), full(bias_new)],
        out_specs=pl.BlockSpec((None, seq, WIDTH_A), lambda b: (b, 0, 0)),
        out_shape=jax.ShapeDtypeStruct((nb, seq, WIDTH_A), BF16),
        compiler_params=_params(1),
        name="attn_sample",
    )(*lam_vecs, subln_g.reshape(1, D_HEAD), qkv3, kc, vc, bias_cache, bias_new)
    return o.reshape(nb * seq, WIDTH_A)


def _causal_conv3(x3, prev8, w):
    p0 = prev8[:, SUBLANE - 2:SUBLANE - 1, :]
    p1 = prev8[:, SUBLANE - 1:SUBLANE, :]
    row = lax.broadcasted_iota(jnp.int32, x3.shape, 1)
    x1 = jnp.where(row == 0, p1, pltpu.roll(x3, 1, axis=1))
    x2 = jnp.where(row == 0, p0, jnp.where(row == 1, p1, pltpu.roll(x3, 2, axis=1)))
    return x2 * w[0:1, :] + x1 * w[1:2, :] + x3 * w[2:3, :]


def _conv_with_state(x, i, nseq, tps, st_ref, so_ref, halo):
    rows = x.shape[0] // nseq
    x3 = x.reshape(nseq, rows, x.shape[1])
    last8 = x3[:, rows - SUBLANE:, :]
    if tps == 1:
        prev8 = st_ref[...]
    else:
        @pl.when(i % tps == 0)
        def _():
            halo[...] = st_ref[...]

        prev8 = halo[...]
        halo[...] = last8
    so_ref[...] = last8
    return x3, prev8


def _merge_kernel(nseq, tps, o_ref, b_ref, c_ref, xi_ref, ga_ref, gb_ref, x_ref, st_ref, cw_ref,
                  wa_ref, wb_ref, wo_ref, g2_ref, h_ref, hn_ref, so_ref, *scratch):
    i = pl.program_id(0)
    u = c_ref[...].astype(F32) * xi_ref[...].astype(F32)
    u3, prev8 = _conv_with_state(u, i, nseq, tps, st_ref, so_ref, scratch[0] if tps > 1 else None)
    z = _causal_conv3(u3, prev8, cw_ref[...]).reshape(u.shape)
    ob = (b_ref[...].astype(F32) * z).astype(BF16)
    ya = jnp.dot(o_ref[...], wa_ref[...], preferred_element_type=F32)
    yb = jnp.dot(ob, wb_ref[...], preferred_element_type=F32)
    merged = jax.nn.sigmoid(ga_ref[...].astype(F32)) * ya + jax.nn.sigmoid(gb_ref[...].astype(F32)) * yb
    hres = x_ref[...] + jnp.dot(merged.astype(BF16), wo_ref[...], preferred_element_type=F32)
    h_ref[...] = hres
    hn_ref[...] = _rms(hres, g2_ref[...]).astype(BF16)


def _merge(o, bcx, gab, x2d, state8, conv_w, wa, wb, wo, norm2_g, nb, seq, rows):
    t = x2d.shape[0]
    nseq = max(rows // seq, 1)
    tps = max(seq // rows, 1)
    assert t % rows == 0 and (rows % seq == 0 or seq % rows == 0)
    tok = lambda width, col: pl.BlockSpec((rows, width), lambda i: (i, col))
    full = lambda a: pl.BlockSpec(a.shape, lambda i: (0,) * a.ndim)
    state_spec = pl.BlockSpec((nseq, SUBLANE, WIDTH_B), lambda i: (i // tps, 0, 0))
    scratch = [pltpu.VMEM((nseq, SUBLANE, WIDTH_B), F32)] if tps > 1 else []
    g2 = norm2_g.reshape(1, D_MODEL)
    return pl.pallas_call(
        functools.partial(_merge_kernel, nseq, tps),
        grid=(t // rows,),
        in_specs=[tok(WIDTH_A, 0), tok(WIDTH_B, 0), tok(WIDTH_B, 1), tok(WIDTH_B, 2),
                  tok(D_MODEL, 0), tok(D_MODEL, 1), tok(D_MODEL, 0), state_spec,
                  full(conv_w), full(wa), full(wb), full(wo), full(g2)],
        out_specs=[tok(D_MODEL, 0), tok(D_MODEL, 0), state_spec],
        out_shape=[jax.ShapeDtypeStruct((t, D_MODEL), F32),
                   jax.ShapeDtypeStruct((t, D_MODEL), BF16),
                   jax.ShapeDtypeStruct((nb, SUBLANE, WIDTH_B), F32)],
        scratch_shapes=scratch,
        compiler_params=_params(1),
        name="merge",
    )(o, bcx, bcx, bcx, gab, gab, x2d, state8, conv_w, wa, wb, wo, g2)


FFN_TILE_F = 512
FFN_SUB_TILES = 2


def _ffn_kernel(nseq, tps, nf, hn_ref, h_ref, wg_ref, wv_ref, cwg_ref, cwv_ref, stg_ref, stv_ref,
                wd_ref, fg_ref, y_ref, sog_ref, sov_ref, acc_ref, *scratch):
    i = pl.program_id(0)
    f = pl.program_id(1)
    rows = hn_ref.shape[0]
    sub = rows // FFN_SUB_TILES
    branches = ((wg_ref, cwg_ref, stg_ref, sog_ref), (wv_ref, cwv_ref, stv_ref, sov_ref))

    @pl.when(jnp.logical_and(i == 0, f == 0))
    def _():
        acc_ref[...] = jnp.zeros_like(acc_ref)

    if tps > 1:
        halos = [scratch[0].at[2 * f + slot] for slot in range(2)]

        @pl.when(i % tps == 0)
        def _():
            for halo, (_, _, st_ref, _) in zip(halos, branches):
                halo[...] = st_ref[...]

        prev = [halo[...] for halo in halos]
    else:
        assert sub % (rows // nseq) == 0
        seqs = nseq // FFN_SUB_TILES

    for r in range(FFN_SUB_TILES):
        hn = hn_ref[r * sub:(r + 1) * sub, :]
        convs = []
        for slot, (w_ref, cw_ref, st_ref, so_ref) in enumerate(branches):
            up = jnp.dot(hn, w_ref[...], preferred_element_type=F32)
            if tps > 1:
                up3 = up.reshape(1, sub, up.shape[1])
                prev8 = prev[slot]
                prev[slot] = up3[:, sub - SUBLANE:, :]
            else:
                up3 = up.reshape(seqs, sub // seqs, up.shape[1])
                prev8 = st_ref[r * seqs:(r + 1) * seqs]
                so_ref[r * seqs:(r + 1) * seqs] = up3[:, sub // seqs - SUBLANE:, :]
            convs.append(_causal_conv3(up3, prev8, cw_ref[...]).reshape(up.shape))
        act = (jax.nn.silu(convs[0]) * convs[1]).astype(BF16)
        part = jnp.dot(act, wd_ref[...], preferred_element_type=F32)
        acc_rows = acc_ref.at[r * sub:(r + 1) * sub, :]
        acc_rows[...] = jnp.where(f == 0, part, acc_rows[...] + part)

    if tps > 1:
        for halo, last8, (_, _, _, so_ref) in zip(halos, prev, branches):
            halo[...] = last8
            so_ref[...] = last8

    @pl.when(f == nf - 1)
    def _():
        y_ref[...] = _rms(h_ref[...] + acc_ref[...], fg_ref[...])


def _ffn(hn, hres, w_up, ffn_conv_w, state8, w_down, final_g, nb, seq, rows):
    t = hn.shape[0]
    tf = FFN_TILE_F
    nf = D_FF // tf
    nseq = max(rows // seq, 1)
    tps = max(seq // rows, 1)
    assert D_FF % tf == 0 and t % rows == 0 and (rows % seq == 0 or seq % rows == 0)
    tok = pl.BlockSpec((rows, D_MODEL), lambda i, f: (i, 0))
    gate_cols = lambda r: pl.BlockSpec((r, tf), lambda i, f: (0, f))
    val_cols = lambda r: pl.BlockSpec((r, tf), lambda i, f: (0, nf + f))
    st_gate = pl.BlockSpec((nseq, SUBLANE, tf), lambda i, f: (i // tps, 0, f))
    st_val = pl.BlockSpec((nseq, SUBLANE, tf), lambda i, f: (i // tps, 0, nf + f))
    scratch = [pltpu.VMEM((rows, D_MODEL), F32)]
    if tps > 1:
        scratch.append(pltpu.VMEM((2 * nf, nseq, SUBLANE, tf), F32))
    tile_rows = pl.BlockSpec((nseq, SUBLANE, tf), lambda i, f: (i, 0, f))
    fg = final_g.reshape(1, D_MODEL)
    y, tail_g, tail_v = pl.pallas_call(
        functools.partial(_ffn_kernel, nseq, tps, nf),
        grid=(t // rows, nf),
        in_specs=[tok, tok, gate_cols(D_MODEL), val_cols(D_MODEL), gate_cols(CONV_W), val_cols(CONV_W),
                  st_gate, st_val,
                  pl.BlockSpec((tf, D_MODEL), lambda i, f: (f, 0)),
                  pl.BlockSpec((1, D_MODEL), lambda i, f: (0, 0))],
        out_specs=[tok, tile_rows, tile_rows],
        out_shape=[jax.ShapeDtypeStruct((t, D_MODEL), F32),
                   jax.ShapeDtypeStruct((nb * tps, SUBLANE, D_FF), F32),
                   jax.ShapeDtypeStruct((nb * tps, SUBLANE, D_FF), F32)],
        scratch_shapes=scratch,
        compiler_params=_params(2),
        name="ffn",
    )(hn, hres, w_up, w_up, ffn_conv_w, ffn_conv_w, state8, state8, w_down, fg)
    return y, tail_g[tps - 1::tps], tail_v[tps - 1::tps]


def _state8(state):
    return jnp.pad(state, ((0, 0), (SUBLANE - (CONV_W - 1), 0), (0, 0)))


def _state_out(state8):
    return state8[:, SUBLANE - (CONV_W - 1):, :]


def _layer(x2d, nb, seq, attn_fn, conv8, ffn8, lp, lambda_init, tiles):
    qkv, k_f32, v_f32, bcx, gab = _in_proj(x2d, lp["norm1_g"], lp["w_in"], tiles["in"])
    o = attn_fn(qkv)
    hres, hn, conv_out8 = _merge(o, bcx, gab, x2d, conv8, lp["conv_w"], lp["w_proj_a"], lp["w_proj_b"],
                                 lp["w_o"], lp["norm2_g"], nb, seq, tiles["merge"])
    y, ffn_g8, ffn_v8 = _ffn(hn, hres, lp["w_up"], lp["ffn_conv_w"], ffn8, lp["w_down"], lp["final_g"],
                             nb, seq, tiles["ffn"])
    k = k_f32.reshape(nb, seq, N_HEADS, D_HEAD)
    v = v_f32.reshape(nb, seq, N_HEADS, D_HEAD)
    conv_state = _state_out(conv_out8)
    ffn_state = jnp.concatenate([_state_out(ffn_g8), _state_out(ffn_v8)], axis=-1)
    return y, k, v, conv_state, ffn_state


PROMPT_TILES = {"in": 512, "merge": 256, "ffn": 512}
SAMPLE_TILES = {"in": 512, "merge": 256, "ffn": 512}
ATTN_TILE_Q = 512


def kernel(x_prompt, x_sample, cache_k, cache_v, state_conv_mix, state_conv_ffn, rel_bias, norm1_g, w_in,
           lambda_q1, lambda_k1, lambda_q2, lambda_k2, subln_g, conv_w, w_proj_a, w_proj_b, w_o, norm2_g,
           w_up, ffn_conv_w, w_down, final_g):
    depth = w_in.shape[0]
    nbp, seq_p, _ = x_prompt.shape
    nbs, seq_s, _ = x_sample.shape
    past = cache_k.shape[2]
    tq = ATTN_TILE_Q

    bias_near, bias_diag, bias_cache, bias_new = _bias_tiles(
        rel_bias,
        ((tq, tq, tq, 0), (tq, tq, tq, tq), (seq_s, past, past, 0), (seq_s, seq_s, past, past)))

    hp = x_prompt.reshape(nbp * seq_p, D_MODEL)
    hs = x_sample.reshape(nbs * seq_s, D_MODEL)
    outs_p, outs_s = [], []
    for d in range(depth):
        assert depth == 1, "the ffn kernel ends with the final RMSNorm, so it is written for a single layer"
        lp = {
            "norm1_g": norm1_g[d], "w_in": w_in[d].astype(BF16), "conv_w": conv_w[d],
            "w_proj_a": w_proj_a[d].astype(BF16), "w_proj_b": w_proj_b[d].astype(BF16),
            "w_o": w_o[d].astype(BF16), "norm2_g": norm2_g[d], "w_up": w_up[d].astype(BF16),
            "ffn_conv_w": ffn_conv_w[d], "w_down": w_down[d].astype(BF16),
            "final_g": final_g,
        }
        li = _lambda_init(d)
        lam_vecs = tuple(a[d].reshape(1, D_QK) for a in (lambda_q1, lambda_k1, lambda_q2, lambda_k2))

        attn_p = lambda qkv: _attn_prompt(qkv, rel_bias, lam_vecs, subln_g[d], bias_near, bias_diag,
                                          nbp, seq_p, tq, li)
        zeros_mix = jnp.zeros((nbp, SUBLANE, WIDTH_B), F32)
        zeros_ffn = jnp.zeros((nbp, SUBLANE, 2 * D_FF), F32)
        hp, kp, vp, cmp_, cfp = _layer(hp, nbp, seq_p, attn_p, zeros_mix, zeros_ffn, lp, li, PROMPT_TILES)

        attn_s = lambda qkv: _attn_sample(qkv, cache_k, cache_v, d, lam_vecs, subln_g[d], bias_cache,
                                          bias_new, nbs, seq_s, past, li)
        hs, ks, vs, cms, cfs = _layer(hs, nbs, seq_s, attn_s, _state8(state_conv_mix[d]),
                                      _state8(state_conv_ffn[d]), lp, li, SAMPLE_TILES)
        outs_p.append((kp, vp, cmp_, cfp))
        outs_s.append((ks, vs, cms, cfs))

    stack = lambda outs, idx: jnp.stack([o[idx] for o in outs])
    return (hp.reshape(nbp, seq_p, D_MODEL), hs.reshape(nbs, seq_s, D_MODEL),
            stack(outs_p, 0), stack(outs_p, 1), stack(outs_p, 2), stack(outs_p, 3),
            stack(outs_s, 0), stack(outs_s, 1), stack(outs_s, 2), stack(outs_s, 3))
```

```python
import functools
import math

import jax
import jax.numpy as jnp
from jax import lax
from jax.experimental import pallas as pl
from jax.experimental.pallas import tpu as pltpu

D_MODEL = 2048
N_HEADS = 8
D_QK = 64
D_HEAD = 2 * D_QK
WIDTH_A = N_HEADS * D_HEAD
WIDTH_B = D_MODEL // 2
D_FF = 5632
CONV_W = 3
CHUNK = 64
N_BUCKETS = 32
MAX_DIST = 128
EPS = 1e-6
NEG_INF = -1e30

SUBLANE = 8
LANE = 128
VMEM_LIMIT_BYTES = 56 * 1024 * 1024

F32 = jnp.float32
BF16 = jnp.bfloat16


def _bucket_thresholds():
    nb = N_BUCKETS // 2
    max_exact = nb // 2
    steps = nb - max_exact
    out = []
    for k in range(1, steps):
        n = max_exact
        while n ** steps * max_exact ** k < MAX_DIST ** k * max_exact ** steps:
            n += 1
        out.append(n)
    return tuple(out)


_BUCKET_THRESHOLDS = _bucket_thresholds()
FAR_DIST = _BUCKET_THRESHOLDS[-1]
FAR_BUCKET = N_BUCKETS // 2 - 1
_CHUNK_SHIFT = CHUNK.bit_length() - 1
assert 1 << _CHUNK_SHIFT == CHUNK


def _bucket_of(rel):
    nb = N_BUCKETS // 2
    max_exact = nb // 2
    n = abs(rel)
    large = max_exact + sum(n >= thr for thr in _BUCKET_THRESHOLDS)
    return (nb if rel > 0 else 0) + (n if n < max_exact else large)


def _lambda_init(layer_idx):
    return 0.8 - 0.6 * math.exp(-0.3 * layer_idx)


def _rms(x, g):
    return x * lax.rsqrt(jnp.mean(x * x, axis=-1, keepdims=True) + EPS) * g


def _params(n_axes):
    return pltpu.CompilerParams(
        dimension_semantics=("arbitrary",) * n_axes, vmem_limit_bytes=VMEM_LIMIT_BYTES)


def _bias_tile_kernel(specs, table_ref, *out_refs):
    h = pl.program_id(0)
    nb = N_BUCKETS // 2
    max_exact = nb // 2
    for (rows, cols, q0, k0), out_ref in zip(specs, out_refs):
        qpos = q0 + lax.broadcasted_iota(jnp.int32, (rows, cols), 0)
        kpos = k0 + lax.broadcasted_iota(jnp.int32, (rows, cols), 1)
        rel = kpos - qpos
        n = jnp.abs(rel)
        large = jnp.full((rows, cols), max_exact, jnp.int32)
        for thr in _BUCKET_THRESHOLDS:
            large = large + jnp.where(n >= thr, 1, 0)
        bucket = jnp.where(rel > 0, nb, 0) + jnp.where(n < max_exact, n, large)
        bias = jnp.zeros((rows, cols), F32)
        rel_lo = k0 - (q0 + rows - 1)
        rel_hi = min(k0 + cols - 1 - q0, CHUNK - 1)
        for b in sorted({_bucket_of(r) for r in range(rel_lo, rel_hi + 1)}):
            bias = jnp.where(bucket == b, table_ref[b, h], bias)
        visible = (lax.shift_right_logical(kpos, _CHUNK_SHIFT)
                   <= lax.shift_right_logical(qpos, _CHUNK_SHIFT))
        out_ref[...] = jnp.where(visible, bias, NEG_INF)


def _bias_tiles(rel_bias, specs):
    return pl.pallas_call(
        functools.partial(_bias_tile_kernel, specs),
        grid=(N_HEADS,),
        in_specs=[pl.BlockSpec(memory_space=pltpu.SMEM)],
        out_specs=[pl.BlockSpec((None, r, c), lambda h: (h, 0, 0)) for r, c, _, _ in specs],
        out_shape=[jax.ShapeDtypeStruct((N_HEADS, r, c), F32) for r, c, _, _ in specs],
        compiler_params=_params(1),
        name="bias_tiles",
    )(rel_bias)


IN_TILE_N = 1024
_N_QKV_TILES = 3
_N_BCX_TILES = 3
_N_GAB_TILES = 4
_N_IN_TILES = _N_QKV_TILES + _N_BCX_TILES + _N_GAB_TILES


def _in_proj_kernel(x_ref, g_ref, w_ref, qkv_ref, kf_ref, vf_ref, bcx_ref, gab_ref, xn_ref):
    j = pl.program_id(1)

    @pl.when(j == 0)
    def _():
        xn_ref[...] = _rms(x_ref[...], g_ref[...]).astype(BF16)

    y = jnp.dot(xn_ref[...], w_ref[...], preferred_element_type=F32)

    @pl.when(j == 0)
    def _():
        qkv_ref[...] = (y * (D_QK ** -0.5)).astype(BF16)

    @pl.when(j == 1)
    def _():
        qkv_ref[...] = y.astype(BF16)
        kf_ref[...] = y

    @pl.when(j == 2)
    def _():
        qkv_ref[...] = y.astype(BF16)
        vf_ref[...] = y

    @pl.when(jnp.logical_and(j >= _N_QKV_TILES, j < _N_QKV_TILES + _N_BCX_TILES))
    def _():
        bcx_ref[...] = y.astype(BF16)

    @pl.when(j >= _N_QKV_TILES + _N_BCX_TILES)
    def _():
        gab_ref[...] = y.astype(BF16)


def _in_proj(x2d, norm_g, w_in_bf, tm):
    t = x2d.shape[0]
    tn = IN_TILE_N
    assert t % tm == 0 and w_in_bf.shape == (D_MODEL, _N_IN_TILES * tn)
    return pl.pallas_call(
        _in_proj_kernel,
        grid=(t // tm, _N_IN_TILES),
        in_specs=[
            pl.BlockSpec((tm, D_MODEL), lambda i, j: (i, 0)),
            pl.BlockSpec((1, D_MODEL), lambda i, j: (0, 0)),
            pl.BlockSpec((D_MODEL, tn), lambda i, j: (0, j)),
        ],
        out_specs=[
            pl.BlockSpec((tm, tn), lambda i, j: (i, jnp.minimum(j, _N_QKV_TILES - 1))),
            pl.BlockSpec((tm, tn), lambda i, j: (i, 0)),
            pl.BlockSpec((tm, tn), lambda i, j: (i, 0)),
            pl.BlockSpec((tm, tn), lambda i, j: (i, jnp.clip(j - _N_QKV_TILES, 0, _N_BCX_TILES - 1))),
            pl.BlockSpec((tm, tn), lambda i, j: (i, jnp.maximum(j - _N_QKV_TILES - _N_BCX_TILES, 0))),
        ],
        out_shape=[
            jax.ShapeDtypeStruct((t, _N_QKV_TILES * tn), BF16),
            jax.ShapeDtypeStruct((t, tn), F32),
            jax.ShapeDtypeStruct((t, tn), F32),
            jax.ShapeDtypeStruct((t, _N_BCX_TILES * tn), BF16),
            jax.ShapeDtypeStruct((t, _N_GAB_TILES * tn), BF16),
        ],
        scratch_shapes=[pltpu.VMEM((tm, D_MODEL), BF16)],
        compiler_params=_params(2),
        name="in_proj",
    )(x2d, norm_g.reshape(1, D_MODEL), w_in_bf)


def _scores(qm, k):
    return lax.dot_general(qm, k, (((1,), (1,)), ((), ())), preferred_element_type=F32)


def _tile_lanes(x, n):
    return x if n == 1 else jnp.concatenate([x] * n, axis=1)


def _with_ones(v):
    return jnp.concatenate([v, jnp.ones_like(v)], axis=1)


def _softmax_step(s, v_ext, state, shift=None):
    m_ref, acc_ref = state
    keys = s.shape[1]
    assert keys % LANE == 0
    row_max = jnp.max(s, axis=-1, keepdims=True)
    if shift is not None:
        row_max = row_max + shift
    m_old = m_ref[...]
    m_new = jnp.maximum(m_old, row_max)
    sub = m_new if shift is None else m_new - shift
    p = jnp.exp(s - _tile_lanes(sub, keys // LANE))
    pv = jnp.dot(p.astype(BF16), v_ext, preferred_element_type=F32)
    alpha = jnp.exp(m_old - m_new)
    acc_ref[...] = _tile_lanes(alpha, 2 * D_HEAD // LANE) * acc_ref[...] + pv
    m_ref[...] = m_new


def _split_maps(q):
    lane = lax.broadcasted_iota(jnp.int32, q.shape, 1)
    zero = jnp.zeros_like(q)
    return jnp.where(lane < D_QK, q, zero), jnp.where(lane >= D_QK, q, zero)


def _lambda(lq1_ref, lk1_ref, lq2_ref, lk2_ref, lambda_init):
    s1 = jnp.sum(lq1_ref[...] * lk1_ref[...], axis=-1, keepdims=True)
    s2 = jnp.sum(lq2_ref[...] * lk2_ref[...], axis=-1, keepdims=True)
    return jnp.exp(s1) - jnp.exp(s2) + lambda_init


def _attn_finish(state1, state2, lam, g, lambda_init):
    a1 = state1[1][...]
    a2 = state2[1][...]
    o = a1[:, :D_HEAD] / a1[:, D_HEAD:] - lam * (a2[:, :D_HEAD] / a2[:, D_HEAD:])
    return _rms(o, g) * (1.0 - lambda_init)


def _attn_prompt_kernel(tq, nq, lambda_init, table_ref, lq1_ref, lk1_ref, lq2_ref, lk2_ref, g_ref,
                        q_ref, k_ref, v_ref, bnear_ref, bdiag_ref, o_ref,
                        m1_ref, a1_ref, m2_ref, a2_ref, sa1_ref, sa2_ref, sb1_ref, sb2_ref):
    h = pl.program_id(1)
    far_bias = table_ref[FAR_BUCKET, h]
    lam = _lambda(lq1_ref, lk1_ref, lq2_ref, lk2_ref, lambda_init)
    st1 = (m1_ref, a1_ref)
    st2 = (m2_ref, a2_ref)
    buf_a = (sa1_ref, sa2_ref)
    buf_b = (sb1_ref, sb2_ref)

    def q_tile(qi, carry):
        q0 = pl.multiple_of(qi * tq, tq)
        q1, q2 = _split_maps(q_ref[pl.ds(q0, tq), :])

        def score_block(j, buf):
            k = k_ref[pl.ds(pl.multiple_of(j * tq, tq), tq), :]
            buf[0][...] = _scores(q1, k)
            buf[1][...] = _scores(q2, k)

        def step(j, buf, nxt=None, bias=None, shift=None):
            if nxt is not None:
                score_block(j + 1, nxt)
            v = _with_ones(v_ref[pl.ds(pl.multiple_of(j * tq, tq), tq), :])
            s1 = buf[0][...]
            s2 = buf[1][...]
            if bias is not None:
                s1 = s1 + bias
                s2 = s2 + bias
            _softmax_step(s1, v, st1, shift=shift)
            _softmax_step(s2, v, st2, shift=shift)

        for m_ref, a_ref in (st1, st2):
            m_ref[...] = jnp.full(m_ref.shape, NEG_INF, F32)
            a_ref[...] = jnp.zeros(a_ref.shape, F32)

        n_far = jnp.maximum(qi - 1, 0)
        odd_far = jnp.bitwise_and(n_far, 1)
        qi_even = jnp.bitwise_and(qi, 1) == 0

        @pl.when(qi_even)
        def _():
            score_block(0, buf_a)

        @pl.when(jnp.logical_not(qi_even))
        def _():
            score_block(0, buf_b)

        @pl.when(odd_far == 1)
        def _():
            step(0, buf_a, nxt=buf_b, shift=far_bias)

        def far_pair(t, c):
            j = odd_far + 2 * t
            step(j, buf_b, nxt=buf_a, shift=far_bias)
            step(j + 1, buf_a, nxt=buf_b, shift=far_bias)
            return c

        lax.fori_loop(0, lax.shift_right_logical(n_far, 1), far_pair, 0)

        @pl.when(qi > 0)
        def _():
            step(qi - 1, buf_b, nxt=buf_a, bias=bnear_ref[...])

        step(qi, buf_a, bias=bdiag_ref[...])
        o = _attn_finish(st1, st2, lam, g_ref[...], lambda_init)
        o_ref[pl.ds(q0, tq), :] = o.astype(BF16)
        return carry

    lax.fori_loop(0, nq, q_tile, 0)


def _attn_prompt(qkv, rel_bias, lam_vecs, subln_g, bias_near, bias_diag, nb, seq, tq, lambda_init):
    assert seq % tq == 0 and tq % CHUNK == 0 and tq >= FAR_DIST
    qkv3 = qkv.reshape(nb, seq, 3 * WIDTH_A)
    vec = pl.BlockSpec((1, D_QK), lambda b, h: (0, 0))
    head_cols = lambda off: pl.BlockSpec((None, seq, D_HEAD), lambda b, h: (b, 0, off + h))
    tile = pl.BlockSpec((None, tq, tq), lambda b, h: (h, 0, 0))
    stat = pltpu.VMEM((tq, LANE), F32)
    acc = pltpu.VMEM((tq, 2 * D_HEAD), F32)
    o = pl.pallas_call(
        functools.partial(_attn_prompt_kernel, tq, seq // tq, lambda_init),
        grid=(nb, N_HEADS),
        in_specs=[pl.BlockSpec(memory_space=pltpu.SMEM), vec, vec, vec, vec,
                  pl.BlockSpec((1, D_HEAD), lambda b, h: (0, 0)),
                  head_cols(0), head_cols(N_HEADS), head_cols(2 * N_HEADS), tile, tile],
        out_specs=pl.BlockSpec((None, seq, D_HEAD), lambda b, h: (b, 0, h)),
        out_shape=jax.ShapeDtypeStruct((nb, seq, WIDTH_A), BF16),
        scratch_shapes=[stat, acc, stat, acc] + [pltpu.VMEM((tq, tq), F32)] * 4,
        compiler_params=_params(2),
        name="attn_prompt",
    )(rel_bias, *lam_vecs, subln_g.reshape(1, D_HEAD), qkv3, qkv3, qkv3, bias_near, bias_diag)
    return o.reshape(nb * seq, WIDTH_A)


def _softmax_two_blocks(qm, k_a, v_a, bias_a, k_b, v_b, bias_b):
    s_a = _scores(qm, k_a) + bias_a
    s_b = _scores(qm, k_b) + bias_b
    m = jnp.maximum(jnp.max(s_a, axis=-1, keepdims=True), jnp.max(s_b, axis=-1, keepdims=True))
    p_a = jnp.exp(s_a - m).astype(BF16)
    p_b = jnp.exp(s_b - m).astype(BF16)
    return jnp.dot(p_a, v_a, preferred_element_type=F32) + jnp.dot(p_b, v_b, preferred_element_type=F32)


def _attn_sample_kernel(lambda_init, past, lq1_ref, lk1_ref, lq2_ref, lk2_ref, g_ref,
                        qkv_ref, kc_ref, vc_ref, bc_ref, bn_ref, o_ref):
    lam = _lambda(lq1_ref, lk1_ref, lq2_ref, lk2_ref, lambda_init)
    for h in range(N_HEADS):
        cols = lambda part: slice((part * N_HEADS + h) * D_HEAD, (part * N_HEADS + h + 1) * D_HEAD)
        q1, q2 = _split_maps(qkv_ref[:, cols(0)])
        kn = qkv_ref[:, cols(1)]
        vn = _with_ones(qkv_ref[:, cols(2)])
        kc = kc_ref[pl.ds(h, past, stride=N_HEADS), :].astype(BF16)
        vc = _with_ones(vc_ref[pl.ds(h, past, stride=N_HEADS), :].astype(BF16))
        a1 = _softmax_two_blocks(q1, kc, vc, bc_ref[h], kn, vn, bn_ref[h])
        a2 = _softmax_two_blocks(q2, kc, vc, bc_ref[h], kn, vn, bn_ref[h])
        o = a1[:, :D_HEAD] / a1[:, D_HEAD:] - lam * (a2[:, :D_HEAD] / a2[:, D_HEAD:])
        o_ref[:, h * D_HEAD:(h + 1) * D_HEAD] = (_rms(o, g_ref[...]) * (1.0 - lambda_init)).astype(BF16)


def _attn_sample(qkv, cache_k, cache_v, layer, lam_vecs, subln_g, bias_cache, bias_new, nb, seq, past,
                 lambda_init):
    qkv3 = qkv.reshape(nb, seq, 3 * WIDTH_A)
    kc = cache_k.reshape(-1, past * N_HEADS, D_HEAD)
    vc = cache_v.reshape(-1, past * N_HEADS, D_HEAD)
    vec = pl.BlockSpec((1, D_QK), lambda b: (0, 0))
    past_rows = pl.BlockSpec((None, past * N_HEADS, D_HEAD), lambda b: (layer * nb + b, 0, 0))
    full = lambda a: pl.BlockSpec(a.shape, lambda b: (0,) * a.ndim)
    o = pl.pallas_call(
        functools.partial(_attn_sample_kernel, lambda_init, past),
        grid=(nb,),
        in_specs=[vec, vec, vec, vec, pl.BlockSpec((1, D_HEAD), lambda b: (0, 0)),
                  pl.BlockSpec((None, seq, 3 * WIDTH_A), lambda b: (b, 0, 0)), past_rows, past_rows,
                  full(bias_cache), full(bias_new)],
        out_specs=pl.BlockSpec((None, seq, WIDTH_A), lambda b: (b, 0, 0)),
        out_shape=jax.ShapeDtypeStruct((nb, seq, WIDTH_A), BF16),
        compiler_params=_params(1),
        name="attn_sample",
    )(*lam_vecs, subln_g.reshape(1, D_HEAD), qkv3, kc, vc, bias_cache, bias_new)
    return o.reshape(nb * seq, WIDTH_A)


def _causal_conv3(x3, prev8, w):
    p0 = prev8[:, SUBLANE - 2:SUBLANE - 1, :]
    p1 = prev8[:, SUBLANE - 1:SUBLANE, :]
    row = lax.broadcasted_iota(jnp.int32, x3.shape, 1)
    x1 = jnp.where(row == 0, p1, pltpu.roll(x3, 1, axis=1))
    x2 = jnp.where(row == 0, p0, jnp.where(row == 1, p1, pltpu.roll(x3, 2, axis=1)))
    return x2 * w[0:1, :] + x1 * w[1:2, :] + x3 * w[2:3, :]


def _conv_with_state(x, i, nseq, tps, st_ref, so_ref, halo):
    rows = x.shape[0] // nseq
    x3 = x.reshape(nseq, rows, x.shape[1])
    last8 = x3[:, rows - SUBLANE:, :]
    if tps == 1:
        prev8 = st_ref[...]
    else:
        @pl.when(i % tps == 0)
        def _():
            halo[...] = st_ref[...]

        prev8 = halo[...]
        halo[...] = last8
    so_ref[...] = last8
    return x3, prev8


def _merge_kernel(nseq, tps, o_ref, b_ref, c_ref, xi_ref, ga_ref, gb_ref, x_ref, st_ref, cw_ref,
                  wa_ref, wb_ref, wo_ref, g2_ref, h_ref, hn_ref, so_ref, *scratch):
    i = pl.program_id(0)
    u = c_ref[...].astype(F32) * xi_ref[...].astype(F32)
    u3, prev8 = _conv_with_state(u, i, nseq, tps, st_ref, so_ref, scratch[0] if tps > 1 else None)
    z = _causal_conv3(u3, prev8, cw_ref[...]).reshape(u.shape)
    ob = (b_ref[...].astype(F32) * z).astype(BF16)
    ya = jnp.dot(o_ref[...], wa_ref[...], preferred_element_type=F32)
    yb = jnp.dot(ob, wb_ref[...], preferred_element_type=F32)
    merged = jax.nn.sigmoid(ga_ref[...].astype(F32)) * ya + jax.nn.sigmoid(gb_ref[...].astype(F32)) * yb
    hres = x_ref[...] + jnp.dot(merged.astype(BF16), wo_ref[...], preferred_element_type=F32)
    h_ref[...] = hres
    hn_ref[...] = _rms(hres, g2_ref[...]).astype(BF16)


def _merge(o, bcx, gab, x2d, state8, conv_w, wa, wb, wo, norm2_g, nb, seq, rows):
    t = x2d.shape[0]
    nseq = max(rows // seq, 1)
    tps = max(seq // rows, 1)
    assert t % rows == 0 and (rows % seq == 0 or seq % rows == 0)
    tok = lambda width, col: pl.BlockSpec((rows, width), lambda i: (i, col))
    full = lambda a: pl.BlockSpec(a.shape, lambda i: (0,) * a.ndim)
    state_spec = pl.BlockSpec((nseq, SUBLANE, WIDTH_B), lambda i: (i // tps, 0, 0))
    scratch = [pltpu.VMEM((nseq, SUBLANE, WIDTH_B), F32)] if tps > 1 else []
    g2 = norm2_g.reshape(1, D_MODEL)
    return pl.pallas_call(
        functools.partial(_merge_kernel, nseq, tps),
        grid=(t // rows,),
        in_specs=[tok(WIDTH_A, 0), tok(WIDTH_B, 0), tok(WIDTH_B, 1), tok(WIDTH_B, 2),
                  tok(D_MODEL, 0), tok(D_MODEL, 1), tok(D_MODEL, 0), state_spec,
                  full(conv_w), full(wa), full(wb), full(wo), full(g2)],
        out_specs=[tok(D_MODEL, 0), tok(D_MODEL, 0), state_spec],
        out_shape=[jax.ShapeDtypeStruct((t, D_MODEL), F32),
                   jax.ShapeDtypeStruct((t, D_MODEL), BF16),
                   jax.ShapeDtypeStruct((nb, SUBLANE, WIDTH_B), F32)],
        scratch_shapes=scratch,
        compiler_params=_params(1),
        name="merge",
    )(o, bcx, bcx, bcx, gab, gab, x2d, state8, conv_w, wa, wb, wo, g2)


FFN_TILE_F = 512
FFN_SUB_TILES = 2


def _ffn_kernel(nseq, tps, nf, hn_ref, h_ref, wg_ref, wv_ref, cwg_ref, cwv_ref, stg_ref, stv_ref,
                wd_ref, fg_ref, y_ref, sog_ref, sov_ref, acc_ref, *scratch):
    i = pl.program_id(0)
    f = pl.program_id(1)
    rows = hn_ref.shape[0]
    sub = rows // FFN_SUB_TILES
    branches = ((wg_ref, cwg_ref, stg_ref, sog_ref), (wv_ref, cwv_ref, stv_ref, sov_ref))

    @pl.when(jnp.logical_and(i == 0, f == 0))
    def _():
        acc_ref[...] = jnp.zeros_like(acc_ref)

    if tps > 1:
        halos = [scratch[0].at[2 * f + slot] for slot in range(2)]

        @pl.when(i % tps == 0)
        def _():
            for halo, (_, _, st_ref, _) in zip(halos, branches):
                halo[...] = st_ref[...]

        prev = [halo[...] for halo in halos]
    else:
        assert sub % (rows // nseq) == 0
        seqs = nseq // FFN_SUB_TILES

    for r in range(FFN_SUB_TILES):
        tile_rows = slice(r * sub, (r + 1) * sub)
        hn = hn_ref[tile_rows, :]
        convs = []
        for slot, (w_ref, cw_ref, st_ref, so_ref) in enumerate(branches):
            up = jnp.dot(hn, w_ref[...], preferred_element_type=F32)
            if tps > 1:
                up3 = up.reshape(1, sub, up.shape[1])
                prev8 = prev[slot]
                prev[slot] = up3[:, sub - SUBLANE:, :]
            else:
                up3 = up.reshape(seqs, sub // seqs, up.shape[1])
                prev8 = st_ref[r * seqs:(r + 1) * seqs]
                so_ref[r * seqs:(r + 1) * seqs] = up3[:, sub // seqs - SUBLANE:, :]
            convs.append(_causal_conv3(up3, prev8, cw_ref[...]).reshape(up.shape))
        act = (jax.nn.silu(convs[0]) * convs[1]).astype(BF16)
        part = jnp.dot(act, wd_ref[...], preferred_element_type=F32)
        acc_rows = acc_ref.at[tile_rows, :]
        acc_rows[...] = jnp.where(f == 0, part, acc_rows[...] + part)

    if tps > 1:
        for halo, last8, (_, _, _, so_ref) in zip(halos, prev, branches):
            halo[...] = last8
            so_ref[...] = last8

    @pl.when(f == nf - 1)
    def _():
        y_ref[...] = _rms(h_ref[...] + acc_ref[...], fg_ref[...])


def _ffn(hn, hres, w_up, ffn_conv_w, state8, w_down, final_g, nb, seq, rows):
    t = hn.shape[0]
    tf = FFN_TILE_F
    nf = D_FF // tf
    nseq = max(rows // seq, 1)
    tps = max(seq // rows, 1)
    assert D_FF % tf == 0 and t % rows == 0 and (rows % seq == 0 or seq % rows == 0)
    tok = pl.BlockSpec((rows, D_MODEL), lambda i, f: (i, 0))
    gate_cols = lambda r: pl.BlockSpec((r, tf), lambda i, f: (0, f))
    val_cols = lambda r: pl.BlockSpec((r, tf), lambda i, f: (0, nf + f))
    st_gate = pl.BlockSpec((nseq, SUBLANE, tf), lambda i, f: (i // tps, 0, f))
    st_val = pl.BlockSpec((nseq, SUBLANE, tf), lambda i, f: (i // tps, 0, nf + f))
    scratch = [pltpu.VMEM((rows, D_MODEL), F32)]
    if tps > 1:
        scratch.append(pltpu.VMEM((2 * nf, nseq, SUBLANE, tf), F32))
    tile_rows = pl.BlockSpec((nseq, SUBLANE, tf), lambda i, f: (i, 0, f))
    fg = final_g.reshape(1, D_MODEL)
    y, tail_g, tail_v = pl.pallas_call(
        functools.partial(_ffn_kernel, nseq, tps, nf),
        grid=(t // rows, nf),
        in_specs=[tok, tok, gate_cols(D_MODEL), val_cols(D_MODEL), gate_cols(CONV_W), val_cols(CONV_W),
                  st_gate, st_val,
                  pl.BlockSpec((tf, D_MODEL), lambda i, f: (f, 0)),
                  pl.BlockSpec((1, D_MODEL), lambda i, f: (0, 0))],
        out_specs=[tok, tile_rows, tile_rows],
        out_shape=[jax.ShapeDtypeStruct((t, D_MODEL), F32),
                   jax.ShapeDtypeStruct((nb * tps, SUBLANE, D_FF), F32),
                   jax.ShapeDtypeStruct((nb * tps, SUBLANE, D_FF), F32)],
        scratch_shapes=scratch,
        compiler_params=_params(2),
        name="ffn",
    )(hn, hres, w_up, w_up, ffn_conv_w, ffn_conv_w, state8, state8, w_down, fg)
    return y, tail_g[tps - 1::tps], tail_v[tps - 1::tps]


def _state8(state):
    return jnp.pad(state, ((0, 0), (SUBLANE - (CONV_W - 1), 0), (0, 0)))


def _state_out(state8):
    return state8[:, SUBLANE - (CONV_W - 1):, :]


def _layer(x2d, nb, seq, attn_fn, conv8, ffn8, lp, lambda_init, tiles):
    qkv, k_f32, v_f32, bcx, gab = _in_proj(x2d, lp["norm1_g"], lp["w_in"], tiles["in"])
    o = attn_fn(qkv)
    hres, hn, conv_out8 = _merge(o, bcx, gab, x2d, conv8, lp["conv_w"], lp["w_proj_a"], lp["w_proj_b"],
                                 lp["w_o"], lp["norm2_g"], nb, seq, tiles["merge"])
    y, ffn_g8, ffn_v8 = _ffn(hn, hres, lp["w_up"], lp["ffn_conv_w"], ffn8, lp["w_down"], lp["final_g"],
                             nb, seq, tiles["ffn"])
    k = k_f32.reshape(nb, seq, N_HEADS, D_HEAD)
    v = v_f32.reshape(nb, seq, N_HEADS, D_HEAD)
    conv_state = _state_out(conv_out8)
    ffn_state = jnp.concatenate([_state_out(ffn_g8), _state_out(ffn_v8)], axis=-1)
    return y, k, v, conv_state, ffn_state


PROMPT_TILES = {"in": 512, "merge": 256, "ffn": 512}
SAMPLE_TILES = {"in": 512, "merge": 256, "ffn": 512}
ATTN_TILE_Q = 512


def kernel(x_prompt, x_sample, cache_k, cache_v, state_conv_mix, state_conv_ffn, rel_bias, norm1_g, w_in,
           lambda_q1, lambda_k1, lambda_q2, lambda_k2, subln_g, conv_w, w_proj_a, w_proj_b, w_o, norm2_g,
           w_up, ffn_conv_w, w_down, final_g):
    depth = w_in.shape[0]
    nbp, seq_p, _ = x_prompt.shape
    nbs, seq_s, _ = x_sample.shape
    past = cache_k.shape[2]
    tq = ATTN_TILE_Q

    bias_near, bias_diag, bias_cache, bias_new = _bias_tiles(
        rel_bias,
        ((tq, tq, tq, 0), (tq, tq, tq, tq), (seq_s, past, past, 0), (seq_s, seq_s, past, past)))

    hp = x_prompt.reshape(nbp * seq_p, D_MODEL)
    hs = x_sample.reshape(nbs * seq_s, D_MODEL)
    outs_p, outs_s = [], []
    for d in range(depth):
        assert depth == 1, "the ffn kernel ends with the final RMSNorm, so it is written for a single layer"
        lp = {
            "norm1_g": norm1_g[d], "w_in": w_in[d].astype(BF16), "conv_w": conv_w[d],
            "w_proj_a": w_proj_a[d].astype(BF16), "w_proj_b": w_proj_b[d].astype(BF16),
            "w_o": w_o[d].astype(BF16), "norm2_g": norm2_g[d], "w_up": w_up[d].astype(BF16),
            "ffn_conv_w": ffn_conv_w[d], "w_down": w_down[d].astype(BF16),
            "final_g": final_g,
        }
        li = _lambda_init(d)
        lam_vecs = tuple(a[d].reshape(1, D_QK) for a in (lambda_q1, lambda_k1, lambda_q2, lambda_k2))

        attn_p = lambda qkv: _attn_prompt(qkv, rel_bias, lam_vecs, subln_g[d], bias_near, bias_diag,
                                          nbp, seq_p, tq, li)
        zeros_mix = jnp.zeros((nbp, SUBLANE, WIDTH_B), F32)
        zeros_ffn = jnp.zeros((nbp, SUBLANE, 2 * D_FF), F32)
        hp, kp, vp, cmp_, cfp = _layer(hp, nbp, seq_p, attn_p, zeros_mix, zeros_ffn, lp, li, PROMPT_TILES)

        attn_s = lambda qkv: _attn_sample(qkv, cache_k, cache_v, d, lam_vecs, subln_g[d], bias_cache,
                                          bias_new, nbs, seq_s, past, li)
        hs, ks, vs, cms, cfs = _layer(hs, nbs, seq_s, attn_s, _state8(state_conv_mix[d]),
                                      _state8(state_conv_ffn[d]), lp, li, SAMPLE_TILES)
        outs_p.append((kp, vp, cmp_, cfp))
        outs_s.append((ks, vs, cms, cfs))

    stack = lambda outs, idx: jnp.stack([o[idx] for o in outs])
    return (hp.reshape(nbp, seq_p, D_MODEL), hs.reshape(nbs, seq_s, D_MODEL),
            stack(outs_p, 0), stack(outs_p, 1), stack(outs_p, 2), stack(outs_p, 3),
            stack(outs_s, 0), stack(outs_s, 1), stack(outs_s, 2), stack(outs_s, 3))
```

```python
import functools
import math

import jax
import jax.numpy as jnp
from jax import lax
from jax.experimental import pallas as pl
from jax.experimental.pallas import tpu as pltpu

D_MODEL = 2048
N_HEADS = 8
D_QK = 64
D_HEAD = 2 * D_QK
WIDTH_A = N_HEADS * D_HEAD
WIDTH_B = D_MODEL // 2
D_FF = 5632
CONV_W = 3
CHUNK = 64
N_BUCKETS = 32
MAX_DIST = 128
EPS = 1e-6
NEG_INF = -1e30

SUBLANE = 8
LANE = 128
VMEM_LIMIT_BYTES = 56 * 1024 * 1024

F32 = jnp.float32
BF16 = jnp.bfloat16


def _bucket_thresholds():
    nb = N_BUCKETS // 2
    max_exact = nb // 2
    steps = nb - max_exact
    out = []
    for k in range(1, steps):
        n = max_exact
        while n ** steps * max_exact ** k < MAX_DIST ** k * max_exact ** steps:
            n += 1
        out.append(n)
    return tuple(out)


_BUCKET_THRESHOLDS = _bucket_thresholds()
FAR_DIST = _BUCKET_THRESHOLDS[-1]
FAR_BUCKET = N_BUCKETS // 2 - 1
_CHUNK_SHIFT = CHUNK.bit_length() - 1
assert 1 << _CHUNK_SHIFT == CHUNK


def _bucket_of(rel):
    nb = N_BUCKETS // 2
    max_exact = nb // 2
    n = abs(rel)
    large = max_exact + sum(n >= thr for thr in _BUCKET_THRESHOLDS)
    return (nb if rel > 0 else 0) + (n if n < max_exact else large)


def _lambda_init(layer_idx):
    return 0.8 - 0.6 * math.exp(-0.3 * layer_idx)


def _rms(x, g):
    return x * lax.rsqrt(jnp.mean(x * x, axis=-1, keepdims=True) + EPS) * g


def _params(n_axes):
    return pltpu.CompilerParams(
        dimension_semantics=("arbitrary",) * n_axes, vmem_limit_bytes=VMEM_LIMIT_BYTES)


def _bias_tile_kernel(specs, table_ref, *out_refs):
    h = pl.program_id(0)
    nb = N_BUCKETS // 2
    max_exact = nb // 2
    for (rows, cols, q0, k0), out_ref in zip(specs, out_refs):
        qpos = q0 + lax.broadcasted_iota(jnp.int32, (rows, cols), 0)
        kpos = k0 + lax.broadcasted_iota(jnp.int32, (rows, cols), 1)
        rel = kpos - qpos
        n = jnp.abs(rel)
        large = jnp.full((rows, cols), max_exact, jnp.int32)
        for thr in _BUCKET_THRESHOLDS:
            large = large + jnp.where(n >= thr, 1, 0)
        bucket = jnp.where(rel > 0, nb, 0) + jnp.where(n < max_exact, n, large)
        bias = jnp.zeros((rows, cols), F32)
        rel_lo = k0 - (q0 + rows - 1)
        rel_hi = min(k0 + cols - 1 - q0, CHUNK - 1)
        for b in sorted({_bucket_of(r) for r in range(rel_lo, rel_hi + 1)}):
            bias = jnp.where(bucket == b, table_ref[b, h], bias)
        visible = (lax.shift_right_logical(kpos, _CHUNK_SHIFT)
                   <= lax.shift_right_logical(qpos, _CHUNK_SHIFT))
        out_ref[...] = jnp.where(visible, bias, NEG_INF)


def _bias_tiles(rel_bias, specs):
    return pl.pallas_call(
        functools.partial(_bias_tile_kernel, specs),
        grid=(N_HEADS,),
        in_specs=[pl.BlockSpec(memory_space=pltpu.SMEM)],
        out_specs=[pl.BlockSpec((None, r, c), lambda h: (h, 0, 0)) for r, c, _, _ in specs],
        out_shape=[jax.ShapeDtypeStruct((N_HEADS, r, c), F32) for r, c, _, _ in specs],
        compiler_params=_params(1),
        name="bias_tiles",
    )(rel_bias)


IN_TILE_N = 1024
_N_QKV_TILES = 3
_N_BCX_TILES = 3
_N_GAB_TILES = 4
_N_IN_TILES = _N_QKV_TILES + _N_BCX_TILES + _N_GAB_TILES


def _in_proj_kernel(x_ref, g_ref, w_ref, qkv_ref, kf_ref, vf_ref, bcx_ref, gab_ref, xn_ref):
    j = pl.program_id(1)

    @pl.when(j == 0)
    def _():
        xn_ref[...] = _rms(x_ref[...], g_ref[...]).astype(BF16)

    y = jnp.dot(xn_ref[...], w_ref[...], preferred_element_type=F32)

    @pl.when(j == 0)
    def _():
        qkv_ref[...] = (y * (D_QK ** -0.5)).astype(BF16)

    @pl.when(j == 1)
    def _():
        qkv_ref[...] = y.astype(BF16)
        kf_ref[...] = y

    @pl.when(j == 2)
    def _():
        qkv_ref[...] = y.astype(BF16)
        vf_ref[...] = y

    @pl.when(jnp.logical_and(j >= _N_QKV_TILES, j < _N_QKV_TILES + _N_BCX_TILES))
    def _():
        bcx_ref[...] = y.astype(BF16)

    @pl.when(j >= _N_QKV_TILES + _N_BCX_TILES)
    def _():
        gab_ref[...] = y.astype(BF16)


def _in_proj(x2d, norm_g, w_in_bf, tm):
    t = x2d.shape[0]
    tn = IN_TILE_N
    assert t % tm == 0 and w_in_bf.shape == (D_MODEL, _N_IN_TILES * tn)
    return pl.pallas_call(
        _in_proj_kernel,
        grid=(t // tm, _N_IN_TILES),
        in_specs=[
            pl.BlockSpec((tm, D_MODEL), lambda i, j: (i, 0)),
            pl.BlockSpec((1, D_MODEL), lambda i, j: (0, 0)),
            pl.BlockSpec((D_MODEL, tn), lambda i, j: (0, j)),
        ],
        out_specs=[
            pl.BlockSpec((tm, tn), lambda i, j: (i, jnp.minimum(j, _N_QKV_TILES - 1))),
            pl.BlockSpec((tm, tn), lambda i, j: (i, 0)),
            pl.BlockSpec((tm, tn), lambda i, j: (i, 0)),
            pl.BlockSpec((tm, tn), lambda i, j: (i, jnp.clip(j - _N_QKV_TILES, 0, _N_BCX_TILES - 1))),
            pl.BlockSpec((tm, tn), lambda i, j: (i, jnp.maximum(j - _N_QKV_TILES - _N_BCX_TILES, 0))),
        ],
        out_shape=[
            jax.ShapeDtypeStruct((t, _N_QKV_TILES * tn), BF16),
            jax.ShapeDtypeStruct((t, tn), F32),
            jax.ShapeDtypeStruct((t, tn), F32),
            jax.ShapeDtypeStruct((t, _N_BCX_TILES * tn), BF16),
            jax.ShapeDtypeStruct((t, _N_GAB_TILES * tn), BF16),
        ],
        scratch_shapes=[pltpu.VMEM((tm, D_MODEL), BF16)],
        compiler_params=_params(2),
        name="in_proj",
    )(x2d, norm_g.reshape(1, D_MODEL), w_in_bf)


def _scores(qm, k):
    return lax.dot_general(qm, k, (((1,), (1,)), ((), ())), preferred_element_type=F32)


def _tile_lanes(x, n):
    return x if n == 1 else jnp.concatenate([x] * n, axis=1)


def _with_ones(v):
    return jnp.concatenate([v, jnp.ones_like(v)], axis=1)


def _softmax_step(s, v_ext, state, shift=None):
    m_ref, acc_ref = state
    keys = s.shape[1]
    assert keys % LANE == 0
    row_max = jnp.max(s, axis=-1, keepdims=True)
    if shift is not None:
        row_max = row_max + shift
    m_old = m_ref[...]
    m_new = jnp.maximum(m_old, row_max)
    sub = m_new if shift is None else m_new - shift
    p = jnp.exp(s - _tile_lanes(sub, keys // LANE))
    pv = jnp.dot(p.astype(BF16), v_ext, preferred_element_type=F32)
    alpha = jnp.exp(m_old - m_new)
    acc_ref[...] = _tile_lanes(alpha, 2 * D_HEAD // LANE) * acc_ref[...] + pv
    m_ref[...] = m_new


def _split_maps(q):
    lane = lax.broadcasted_iota(jnp.int32, q.shape, 1)
    zero = jnp.zeros_like(q)
    return jnp.where(lane < D_QK, q, zero), jnp.where(lane >= D_QK, q, zero)


def _lambda(lq1_ref, lk1_ref, lq2_ref, lk2_ref, lambda_init):
    s1 = jnp.sum(lq1_ref[...] * lk1_ref[...], axis=-1, keepdims=True)
    s2 = jnp.sum(lq2_ref[...] * lk2_ref[...], axis=-1, keepdims=True)
    return jnp.exp(s1) - jnp.exp(s2) + lambda_init


def _attn_finish(state1, state2, lam, g, lambda_init):
    a1 = state1[1][...]
    a2 = state2[1][...]
    o = a1[:, :D_HEAD] / a1[:, D_HEAD:] - lam * (a2[:, :D_HEAD] / a2[:, D_HEAD:])
    return _rms(o, g) * (1.0 - lambda_init)


def _attn_prompt_kernel(tq, nq, lambda_init, table_ref, lq1_ref, lk1_ref, lq2_ref, lk2_ref, g_ref,
                        q_ref, k_ref, v_ref, bnear_ref, bdiag_ref, o_ref,
                        m1_ref, a1_ref, m2_ref, a2_ref, sa1_ref, sa2_ref, sb1_ref, sb2_ref):
    h = pl.program_id(1)
    far_bias = table_ref[FAR_BUCKET, h]
    lam = _lambda(lq1_ref, lk1_ref, lq2_ref, lk2_ref, lambda_init)
    st1 = (m1_ref, a1_ref)
    st2 = (m2_ref, a2_ref)
    buf_a = (sa1_ref, sa2_ref)
    buf_b = (sb1_ref, sb2_ref)

    def q_tile(qi, carry):
        q0 = pl.multiple_of(qi * tq, tq)
        q1, q2 = _split_maps(q_ref[pl.ds(q0, tq), :])

        def score_block(j, buf):
            k = k_ref[pl.ds(pl.multiple_of(j * tq, tq), tq), :]
            buf[0][...] = _scores(q1, k)
            buf[1][...] = _scores(q2, k)

        def step(j, buf, nxt=None, bias=None, shift=None):
            if nxt is not None:
                score_block(j + 1, nxt)
            v = _with_ones(v_ref[pl.ds(pl.multiple_of(j * tq, tq), tq), :])
            s1 = buf[0][...]
            s2 = buf[1][...]
            if bias is not None:
                s1 = s1 + bias
                s2 = s2 + bias
            _softmax_step(s1, v, st1, shift=shift)
            _softmax_step(s2, v, st2, shift=shift)

        for m_ref, a_ref in (st1, st2):
            m_ref[...] = jnp.full(m_ref.shape, NEG_INF, F32)
            a_ref[...] = jnp.zeros(a_ref.shape, F32)

        n_far = jnp.maximum(qi - 1, 0)
        odd_far = jnp.bitwise_and(n_far, 1)
        qi_even = jnp.bitwise_and(qi, 1) == 0

        @pl.when(qi_even)
        def _():
            score_block(0, buf_a)

        @pl.when(jnp.logical_not(qi_even))
        def _():
            score_block(0, buf_b)

        @pl.when(odd_far == 1)
        def _():
            step(0, buf_a, nxt=buf_b, shift=far_bias)

        def far_pair(t, c):
            j = odd_far + 2 * t
            step(j, buf_b, nxt=buf_a, shift=far_bias)
            step(j + 1, buf_a, nxt=buf_b, shift=far_bias)
            return c

        lax.fori_loop(0, lax.shift_right_logical(n_far, 1), far_pair, 0)

        @pl.when(qi > 0)
        def _():
            step(qi - 1, buf_b, nxt=buf_a, bias=bnear_ref[...])

        step(qi, buf_a, bias=bdiag_ref[...])
        o = _attn_finish(st1, st2, lam, g_ref[...], lambda_init)
        o_ref[pl.ds(q0, tq), :] = o.astype(BF16)
        return carry

    lax.fori_loop(0, nq, q_tile, 0)


def _attn_prompt(qkv, rel_bias, lam_vecs, subln_g, bias_near, bias_diag, nb, seq, tq, lambda_init):
    assert seq % tq == 0 and tq % CHUNK == 0 and tq >= FAR_DIST
    qkv3 = qkv.reshape(nb, seq, 3 * WIDTH_A)
    vec = pl.BlockSpec((1, D_QK), lambda b, h: (0, 0))
    head_cols = lambda off: pl.BlockSpec((None, seq, D_HEAD), lambda b, h: (b, 0, off + h))
    tile = pl.BlockSpec((None, tq, tq), lambda b, h: (h, 0, 0))
    stat = pltpu.VMEM((tq, LANE), F32)
    acc = pltpu.VMEM((tq, 2 * D_HEAD), F32)
    o = pl.pallas_call(
        functools.partial(_attn_prompt_kernel, tq, seq // tq, lambda_init),
        grid=(nb, N_HEADS),
        in_specs=[pl.BlockSpec(memory_space=pltpu.SMEM), vec, vec, vec, vec,
                  pl.BlockSpec((1, D_HEAD), lambda b, h: (0, 0)),
                  head_cols(0), head_cols(N_HEADS), head_cols(2 * N_HEADS), tile, tile],
        out_specs=pl.BlockSpec((None, seq, D_HEAD), lambda b, h: (b, 0, h)),
        out_shape=jax.ShapeDtypeStruct((nb, seq, WIDTH_A), BF16),
        scratch_shapes=[stat, acc, stat, acc] + [pltpu.VMEM((tq, tq), F32)] * 4,
        compiler_params=_params(2),
        name="attn_prompt",
    )(rel_bias, *lam_vecs, subln_g.reshape(1, D_HEAD), qkv3, qkv3, qkv3, bias_near, bias_diag)
    return o.reshape(nb * seq, WIDTH_A)


def _softmax_two_blocks(qm, k_a, v_a, bias_a, k_b, v_b, bias_b):
    s_a = _scores(qm, k_a) + bias_a
    s_b = _scores(qm, k_b) + bias_b
    m = jnp.maximum(jnp.max(s_a, axis=-1, keepdims=True), jnp.max(s_b, axis=-1, keepdims=True))
    p_a = jnp.exp(s_a - m).astype(BF16)
    p_b = jnp.exp(s_b - m).astype(BF16)
    return jnp.dot(p_a, v_a, preferred_element_type=F32) + jnp.dot(p_b, v_b, preferred_element_type=F32)


def _attn_sample_kernel(lambda_init, past, lq1_ref, lk1_ref, lq2_ref, lk2_ref, g_ref,
                        qkv_ref, kc_ref, vc_ref, bc_ref, bn_ref, o_ref):
    lam = _lambda(lq1_ref, lk1_ref, lq2_ref, lk2_ref, lambda_init)
    for h in range(N_HEADS):
        cols = lambda part: slice((part * N_HEADS + h) * D_HEAD, (part * N_HEADS + h + 1) * D_HEAD)
        q1, q2 = _split_maps(qkv_ref[:, cols(0)])
        kn = qkv_ref[:, cols(1)]
        vn = _with_ones(qkv_ref[:, cols(2)])
        kc = kc_ref[pl.ds(h, past, stride=N_HEADS), :].astype(BF16)
        vc = _with_ones(vc_ref[pl.ds(h, past, stride=N_HEADS), :].astype(BF16))
        a1 = _softmax_two_blocks(q1, kc, vc, bc_ref[h], kn, vn, bn_ref[h])
        a2 = _softmax_two_blocks(q2, kc, vc, bc_ref[h], kn, vn, bn_ref[h])
        o = a1[:, :D_HEAD] / a1[:, D_HEAD:] - lam * (a2[:, :D_HEAD] / a2[:, D_HEAD:])
        o_ref[:, h * D_HEAD:(h + 1) * D_HEAD] = (_rms(o, g_ref[...]) * (1.0 - lambda_init)).astype(BF16)


def _attn_sample(qkv, cache_k, cache_v, layer, lam_vecs, subln_g, bias_cache, bias_new, nb, seq, past,
                 lambda_init):
    qkv3 = qkv.reshape(nb, seq, 3 * WIDTH_A)
    kc = cache_k.reshape(-1, past * N_HEADS, D_HEAD)
    vc = cache_v.reshape(-1, past * N_HEADS, D_HEAD)
    vec = pl.BlockSpec((1, D_QK), lambda b: (0, 0))
    past_rows = pl.BlockSpec((None, past * N_HEADS, D_HEAD), lambda b: (layer * nb + b, 0, 0))
    full = lambda a: pl.BlockSpec(a.shape, lambda b: (0,) * a.ndim)
    o = pl.pallas_call(
        functools.partial(_attn_sample_kernel, lambda_init, past),
        grid=(nb,),
        in_specs=[vec, vec, vec, vec, pl.BlockSpec((1, D_HEAD), lambda b: (0, 0)),
                  pl.BlockSpec((None, seq, 3 * WIDTH_A), lambda b: (b, 0, 0)), past_rows, past_rows,
                  full(bias_cache), full(bias_new)],
        out_specs=pl.BlockSpec((None, seq, WIDTH_A), lambda b: (b, 0, 0)),
        out_shape=jax.ShapeDtypeStruct((nb, seq, WIDTH_A), BF16),
        compiler_params=_params(1),
        name="attn_sample",
    )(*lam_vecs, subln_g.reshape(1, D_HEAD), qkv3, kc, vc, bias_cache, bias_new)
    return o.reshape(nb * seq, WIDTH_A)


def _causal_conv3(x3, prev8, w):
    p0 = prev8[:, SUBLANE - 2:SUBLANE - 1, :]
    p1 = prev8[:, SUBLANE - 1:SUBLANE, :]
    row = lax.broadcasted_iota(jnp.int32, x3.shape, 1)
    x1 = jnp.where(row == 0, p1, pltpu.roll(x3, 1, axis=1))
    x2 = jnp.where(row == 0, p0, jnp.where(row == 1, p1, pltpu.roll(x3, 2, axis=1)))
    return x2 * w[0:1, :] + x1 * w[1:2, :] + x3 * w[2:3, :]


def _conv_with_state(x, i, nseq, tps, st_ref, so_ref, halo):
    rows = x.shape[0] // nseq
    x3 = x.reshape(nseq, rows, x.shape[1])
    last8 = x3[:, rows - SUBLANE:, :]
    if tps == 1:
        prev8 = st_ref[...]
    else:
        @pl.when(i % tps == 0)
        def _():
            halo[...] = st_ref[...]

        prev8 = halo[...]
        halo[...] = last8
    so_ref[...] = last8
    return x3, prev8


def _merge_kernel(nseq, tps, o_ref, b_ref, c_ref, xi_ref, ga_ref, gb_ref, x_ref, st_ref, cw_ref,
                  wa_ref, wb_ref, wo_ref, g2_ref, h_ref, hn_ref, so_ref, *scratch):
    i = pl.program_id(0)
    u = c_ref[...].astype(F32) * xi_ref[...].astype(F32)
    u3, prev8 = _conv_with_state(u, i, nseq, tps, st_ref, so_ref, scratch[0] if tps > 1 else None)
    z = _causal_conv3(u3, prev8, cw_ref[...]).reshape(u.shape)
    ob = (b_ref[...].astype(F32) * z).astype(BF16)
    ya = jnp.dot(o_ref[...], wa_ref[...], preferred_element_type=F32)
    yb = jnp.dot(ob, wb_ref[...], preferred_element_type=F32)
    merged = jax.nn.sigmoid(ga_ref[...].astype(F32)) * ya + jax.nn.sigmoid(gb_ref[...].astype(F32)) * yb
    hres = x_ref[...] + jnp.dot(merged.astype(BF16), wo_ref[...], preferred_element_type=F32)
    h_ref[...] = hres
    hn_ref[...] = _rms(hres, g2_ref[...]).astype(BF16)


def _merge(o, bcx, gab, x2d, state8, conv_w, wa, wb, wo, norm2_g, nb, seq, rows):
    t = x2d.shape[0]
    nseq = max(rows // seq, 1)
    tps = max(seq // rows, 1)
    assert t % rows == 0 and (rows % seq == 0 or seq % rows == 0)
    tok = lambda width, col: pl.BlockSpec((rows, width), lambda i: (i, col))
    full = lambda a: pl.BlockSpec(a.shape, lambda i: (0,) * a.ndim)
    state_spec = pl.BlockSpec((nseq, SUBLANE, WIDTH_B), lambda i: (i // tps, 0, 0))
    scratch = [pltpu.VMEM((nseq, SUBLANE, WIDTH_B), F32)] if tps > 1 else []
    g2 = norm2_g.reshape(1, D_MODEL)
    return pl.pallas_call(
        functools.partial(_merge_kernel, nseq, tps),
        grid=(t // rows,),
        in_specs=[tok(WIDTH_A, 0), tok(WIDTH_B, 0), tok(WIDTH_B, 1), tok(WIDTH_B, 2),
                  tok(D_MODEL, 0), tok(D_MODEL, 1), tok(D_MODEL, 0), state_spec,
                  full(conv_w), full(wa), full(wb), full(wo), full(g2)],
        out_specs=[tok(D_MODEL, 0), tok(D_MODEL, 0), state_spec],
        out_shape=[jax.ShapeDtypeStruct((t, D_MODEL), F32),
                   jax.ShapeDtypeStruct((t, D_MODEL), BF16),
                   jax.ShapeDtypeStruct((nb, SUBLANE, WIDTH_B), F32)],
        scratch_shapes=scratch,
        compiler_params=_params(1),
        name="merge",
    )(o, bcx, bcx, bcx, gab, gab, x2d, state8, conv_w, wa, wb, wo, g2)


FFN_TILE_F = 512
FFN_SUB_TILES = 2


def _ffn_kernel(nseq, tps, nf, hn_ref, h_ref, wg_ref, wv_ref, cwg_ref, cwv_ref, stg_ref, stv_ref,
                wd_ref, fg_ref, y_ref, sog_ref, sov_ref, *scratch):
    i = pl.program_id(0)
    f = pl.program_id(1)
    rows = hn_ref.shape[0]
    sub = rows // FFN_SUB_TILES
    branches = ((wg_ref, cwg_ref, stg_ref, sog_ref), (wv_ref, cwv_ref, stv_ref, sov_ref))

    @pl.when(f == 0)
    def _():
        y_ref[...] = jnp.zeros_like(y_ref)

    if tps > 1:
        halos = [scratch[0].at[2 * f + slot] for slot in range(2)]

        @pl.when(i % tps == 0)
        def _():
            for halo, (_, _, st_ref, _) in zip(halos, branches):
                halo[...] = st_ref[...]

        prev = [halo[...] for halo in halos]
    else:
        assert sub % (rows // nseq) == 0
        seqs = nseq // FFN_SUB_TILES

    for r in range(FFN_SUB_TILES):
        tile_rows = slice(r * sub, (r + 1) * sub)
        hn = hn_ref[tile_rows, :]
        convs = []
        for slot, (w_ref, cw_ref, st_ref, so_ref) in enumerate(branches):
            up = jnp.dot(hn, w_ref[...], preferred_element_type=F32)
            if tps > 1:
                up3 = up.reshape(1, sub, up.shape[1])
                prev8 = prev[slot]
                prev[slot] = up3[:, sub - SUBLANE:, :]
            else:
                up3 = up.reshape(seqs, sub // seqs, up.shape[1])
                prev8 = st_ref[r * seqs:(r + 1) * seqs]
                so_ref[r * seqs:(r + 1) * seqs] = up3[:, sub // seqs - SUBLANE:, :]
            convs.append(_causal_conv3(up3, prev8, cw_ref[...]).reshape(up.shape))
        act = (jax.nn.silu(convs[0]) * convs[1]).astype(BF16)
        part = jnp.dot(act, wd_ref[...], preferred_element_type=F32)
        y_ref[tile_rows, :] += part

    if tps > 1:
        for halo, last8, (_, _, _, so_ref) in zip(halos, prev, branches):
            halo[...] = last8
            so_ref[...] = last8

    @pl.when(f == nf - 1)
    def _():
        y_ref[...] = _rms(h_ref[...] + y_ref[...], fg_ref[...])


def _ffn(hn, hres, w_up, ffn_conv_w, state8, w_down, final_g, nb, seq, rows):
    t = hn.shape[0]
    tf = FFN_TILE_F
    nf = D_FF // tf
    nseq = max(rows // seq, 1)
    tps = max(seq // rows, 1)
    assert D_FF % tf == 0 and t % rows == 0 and (rows % seq == 0 or seq % rows == 0)
    tok = pl.BlockSpec((rows, D_MODEL), lambda i, f: (i, 0))
    gate_cols = lambda r: pl.BlockSpec((r, tf), lambda i, f: (0, f))
    val_cols = lambda r: pl.BlockSpec((r, tf), lambda i, f: (0, nf + f))
    st_gate = pl.BlockSpec((nseq, SUBLANE, tf), lambda i, f: (i // tps, 0, f))
    st_val = pl.BlockSpec((nseq, SUBLANE, tf), lambda i, f: (i // tps, 0, nf + f))
    resid = pl.BlockSpec((rows, D_MODEL), lambda i, f: (i, 0), pipeline_mode=pl.Buffered(1))
    scratch = [pltpu.VMEM((2 * nf, nseq, SUBLANE, tf), F32)] if tps > 1 else []
    tile_rows = pl.BlockSpec((nseq, SUBLANE, tf), lambda i, f: (i, 0, f))
    fg = final_g.reshape(1, D_MODEL)
    y, tail_g, tail_v = pl.pallas_call(
        functools.partial(_ffn_kernel, nseq, tps, nf),
        grid=(t // rows, nf),
        in_specs=[tok, resid, gate_cols(D_MODEL), val_cols(D_MODEL), gate_cols(CONV_W), val_cols(CONV_W),
                  st_gate, st_val,
                  pl.BlockSpec((tf, D_MODEL), lambda i, f: (f, 0)),
                  pl.BlockSpec((1, D_MODEL), lambda i, f: (0, 0))],
        out_specs=[tok, tile_rows, tile_rows],
        out_shape=[jax.ShapeDtypeStruct((t, D_MODEL), F32),
                   jax.ShapeDtypeStruct((nb * tps, SUBLANE, D_FF), F32),
                   jax.ShapeDtypeStruct((nb * tps, SUBLANE, D_FF), F32)],
        scratch_shapes=scratch,
        compiler_params=_params(2),
        name="ffn",
    )(hn, hres, w_up, w_up, ffn_conv_w, ffn_conv_w, state8, state8, w_down, fg)
    return y, tail_g[tps - 1::tps], tail_v[tps - 1::tps]


def _state8(state):
    return jnp.pad(state, ((0, 0), (SUBLANE - (CONV_W - 1), 0), (0, 0)))


def _state_out(state8):
    return state8[:, SUBLANE - (CONV_W - 1):, :]


def _layer(x2d, nb, seq, attn_fn, conv8, ffn8, lp, lambda_init, tiles):
    qkv, k_f32, v_f32, bcx, gab = _in_proj(x2d, lp["norm1_g"], lp["w_in"], tiles["in"])
    o = attn_fn(qkv)
    hres, hn, conv_out8 = _merge(o, bcx, gab, x2d, conv8, lp["conv_w"], lp["w_proj_a"], lp["w_proj_b"],
                                 lp["w_o"], lp["norm2_g"], nb, seq, tiles["merge"])
    y, ffn_g8, ffn_v8 = _ffn(hn, hres, lp["w_up"], lp["ffn_conv_w"], ffn8, lp["w_down"], lp["final_g"],
                             nb, seq, tiles["ffn"])
    k = k_f32.reshape(nb, seq, N_HEADS, D_HEAD)
    v = v_f32.reshape(nb, seq, N_HEADS, D_HEAD)
    conv_state = _state_out(conv_out8)
    ffn_state = jnp.concatenate([_state_out(ffn_g8), _state_out(ffn_v8)], axis=-1)
    return y, k, v, conv_state, ffn_state


PROMPT_TILES = {"in": 512, "merge": 256, "ffn": 1024}
SAMPLE_TILES = {"in": 512, "merge": 256, "ffn": 1024}
ATTN_TILE_Q = 512


def kernel(x_prompt, x_sample, cache_k, cache_v, state_conv_mix, state_conv_ffn, rel_bias, norm1_g, w_in,
           lambda_q1, lambda_k1, lambda_q2, lambda_k2, subln_g, conv_w, w_proj_a, w_proj_b, w_o, norm2_g,
           w_up, ffn_conv_w, w_down, final_g):
    depth = w_in.shape[0]
    nbp, seq_p, _ = x_prompt.shape
    nbs, seq_s, _ = x_sample.shape
    past = cache_k.shape[2]
    tq = ATTN_TILE_Q

    bias_near, bias_diag, bias_cache, bias_new = _bias_tiles(
        rel_bias,
        ((tq, tq, tq, 0), (tq, tq, tq, tq), (seq_s, past, past, 0), (seq_s, seq_s, past, past)))

    hp = x_prompt.reshape(nbp * seq_p, D_MODEL)
    hs = x_sample.reshape(nbs * seq_s, D_MODEL)
    outs_p, outs_s = [], []
    for d in range(depth):
        assert depth == 1, "the ffn kernel ends with the final RMSNorm, so it is written for a single layer"
        lp = {
            "norm1_g": norm1_g[d], "w_in": w_in[d].astype(BF16), "conv_w": conv_w[d],
            "w_proj_a": w_proj_a[d].astype(BF16), "w_proj_b": w_proj_b[d].astype(BF16),
            "w_o": w_o[d].astype(BF16), "norm2_g": norm2_g[d], "w_up": w_up[d].astype(BF16),
            "ffn_conv_w": ffn_conv_w[d], "w_down": w_down[d].astype(BF16),
            "final_g": final_g,
        }
        li = _lambda_init(d)
        lam_vecs = tuple(a[d].reshape(1, D_QK) for a in (lambda_q1, lambda_k1, lambda_q2, lambda_k2))

        attn_p = lambda qkv: _attn_prompt(qkv, rel_bias, lam_vecs, subln_g[d], bias_near, bias_diag,
                                          nbp, seq_p, tq, li)
        zeros_mix = jnp.zeros((nbp, SUBLANE, WIDTH_B), F32)
        zeros_ffn = jnp.zeros((nbp, SUBLANE, 2 * D_FF), F32)
        hp, kp, vp, cmp_, cfp = _layer(hp, nbp, seq_p, attn_p, zeros_mix, zeros_ffn, lp, li, PROMPT_TILES)

        attn_s = lambda qkv: _attn_sample(qkv, cache_k, cache_v, d, lam_vecs, subln_g[d], bias_cache,
                                          bias_new, nbs, seq_s, past, li)
        hs, ks, vs, cms, cfs = _layer(hs, nbs, seq_s, attn_s, _state8(state_conv_mix[d]),
                                      _state8(state_conv_ffn[d]), lp, li, SAMPLE_TILES)
        outs_p.append((kp, vp, cmp_, cfp))
        outs_s.append((ks, vs, cms, cfs))

    stack = lambda outs, idx: jnp.stack([o[idx] for o in outs])
    return (hp.reshape(nbp, seq_p, D_MODEL), hs.reshape(nbs, seq_s, D_MODEL),
            stack(outs_p, 0), stack(outs_p, 1), stack(outs_p, 2), stack(outs_p, 3),
            stack(outs_s, 0), stack(outs_s, 1), stack(outs_s, 2), stack(outs_s, 3))
```

```python
import functools
import math

import jax
import jax.numpy as jnp
from jax import lax
from jax.experimental import pallas as pl
from jax.experimental.pallas import tpu as pltpu

D_MODEL = 2048
N_HEADS = 8
D_QK = 64
D_HEAD = 2 * D_QK
WIDTH_A = N_HEADS * D_HEAD
WIDTH_B = D_MODEL // 2
D_FF = 5632
CONV_W = 3
CHUNK = 64
N_BUCKETS = 32
MAX_DIST = 128
EPS = 1e-6
NEG_INF = -1e30

SUBLANE = 8
LANE = 128
VMEM_LIMIT_BYTES = 56 * 1024 * 1024

F32 = jnp.float32
BF16 = jnp.bfloat16


def _bucket_thresholds():
    nb = N_BUCKETS // 2
    max_exact = nb // 2
    steps = nb - max_exact
    out = []
    for k in range(1, steps):
        n = max_exact
        while n ** steps * max_exact ** k < MAX_DIST ** k * max_exact ** steps:
            n += 1
        out.append(n)
    return tuple(out)


_BUCKET_THRESHOLDS = _bucket_thresholds()
FAR_DIST = _BUCKET_THRESHOLDS[-1]
FAR_BUCKET = N_BUCKETS // 2 - 1
_CHUNK_SHIFT = CHUNK.bit_length() - 1
assert 1 << _CHUNK_SHIFT == CHUNK


def _bucket_of(rel):
    nb = N_BUCKETS // 2
    max_exact = nb // 2
    n = abs(rel)
    large = max_exact + sum(n >= thr for thr in _BUCKET_THRESHOLDS)
    return (nb if rel > 0 else 0) + (n if n < max_exact else large)


def _lambda_init(layer_idx):
    return 0.8 - 0.6 * math.exp(-0.3 * layer_idx)


def _rms(x, g):
    return x * lax.rsqrt(jnp.mean(x * x, axis=-1, keepdims=True) + EPS) * g


def _params(n_axes):
    return pltpu.CompilerParams(
        dimension_semantics=("arbitrary",) * n_axes, vmem_limit_bytes=VMEM_LIMIT_BYTES)


def _cast_slabs(in_refs, out_refs):
    for i_ref, o_ref in zip(in_refs, out_refs):
        o_ref[...] = i_ref[...].astype(BF16)


def _cast_slab_specs(weights, n_steps, step_of):
    specs, shapes = [], []
    for w in weights:
        rows, cols = w.shape
        assert rows % n_steps == 0 and (rows // n_steps) % (2 * SUBLANE) == 0 and cols % LANE == 0
        specs.append(pl.BlockSpec((rows // n_steps, cols), lambda *idx: (step_of(*idx), 0)))
        shapes.append(jax.ShapeDtypeStruct(w.shape, BF16))
    return specs, list(specs), shapes


def _bias_tile_kernel(specs, table_ref, *out_refs):
    h = pl.program_id(0)
    nb = N_BUCKETS // 2
    max_exact = nb // 2
    for (rows, cols, q0, k0), out_ref in zip(specs, out_refs):
        qpos = q0 + lax.broadcasted_iota(jnp.int32, (rows, cols), 0)
        kpos = k0 + lax.broadcasted_iota(jnp.int32, (rows, cols), 1)
        rel = kpos - qpos
        n = jnp.abs(rel)
        large = jnp.full((rows, cols), max_exact, jnp.int32)
        for thr in _BUCKET_THRESHOLDS:
            large = large + jnp.where(n >= thr, 1, 0)
        bucket = jnp.where(rel > 0, nb, 0) + jnp.where(n < max_exact, n, large)
        bias = jnp.zeros((rows, cols), F32)
        rel_lo = k0 - (q0 + rows - 1)
        rel_hi = min(k0 + cols - 1 - q0, CHUNK - 1)
        for b in sorted({_bucket_of(r) for r in range(rel_lo, rel_hi + 1)}):
            bias = jnp.where(bucket == b, table_ref[b, h], bias)
        visible = (lax.shift_right_logical(kpos, _CHUNK_SHIFT)
                   <= lax.shift_right_logical(qpos, _CHUNK_SHIFT))
        out_ref[...] = jnp.where(visible, bias, NEG_INF)


def _bias_tiles(rel_bias, specs):
    return pl.pallas_call(
        functools.partial(_bias_tile_kernel, specs),
        grid=(N_HEADS,),
        in_specs=[pl.BlockSpec(memory_space=pltpu.SMEM)],
        out_specs=[pl.BlockSpec((None, r, c), lambda h: (h, 0, 0)) for r, c, _, _ in specs],
        out_shape=[jax.ShapeDtypeStruct((N_HEADS, r, c), F32) for r, c, _, _ in specs],
        compiler_params=_params(1),
        name="bias_tiles",
    )(rel_bias)


IN_TILE_N = 1024
_N_QKV_TILES = 3
_N_BCX_TILES = 3
_N_GAB_TILES = 4
_N_IN_TILES = _N_QKV_TILES + _N_BCX_TILES + _N_GAB_TILES


def _in_proj_kernel(x_ref, g_ref, w_ref, qkv_ref, kf_ref, vf_ref, bcx_ref, gab_ref, xn_ref):
    j = pl.program_id(1)

    @pl.when(j == 0)
    def _():
        xn_ref[...] = _rms(x_ref[...], g_ref[...]).astype(BF16)

    y = jnp.dot(xn_ref[...], w_ref[...], preferred_element_type=F32)

    @pl.when(j == 0)
    def _():
        qkv_ref[...] = (y * (D_QK ** -0.5)).astype(BF16)

    @pl.when(j == 1)
    def _():
        qkv_ref[...] = y.astype(BF16)
        kf_ref[...] = y

    @pl.when(j == 2)
    def _():
        qkv_ref[...] = y.astype(BF16)
        vf_ref[...] = y

    @pl.when(jnp.logical_and(j >= _N_QKV_TILES, j < _N_QKV_TILES + _N_BCX_TILES))
    def _():
        bcx_ref[...] = y.astype(BF16)

    @pl.when(j >= _N_QKV_TILES + _N_BCX_TILES)
    def _():
        gab_ref[...] = y.astype(BF16)


def _in_proj(x2d, norm_g, w_in_bf, tm):
    t = x2d.shape[0]
    tn = IN_TILE_N
    assert t % tm == 0 and w_in_bf.shape == (D_MODEL, _N_IN_TILES * tn)
    return pl.pallas_call(
        _in_proj_kernel,
        grid=(t // tm, _N_IN_TILES),
        in_specs=[
            pl.BlockSpec((tm, D_MODEL), lambda i, j: (i, 0)),
            pl.BlockSpec((1, D_MODEL), lambda i, j: (0, 0)),
            pl.BlockSpec((D_MODEL, tn), lambda i, j: (0, j)),
        ],
        out_specs=[
            pl.BlockSpec((tm, tn), lambda i, j: (i, jnp.minimum(j, _N_QKV_TILES - 1))),
            pl.BlockSpec((tm, tn), lambda i, j: (i, 0)),
            pl.BlockSpec((tm, tn), lambda i, j: (i, 0)),
            pl.BlockSpec((tm, tn), lambda i, j: (i, jnp.clip(j - _N_QKV_TILES, 0, _N_BCX_TILES - 1))),
            pl.BlockSpec((tm, tn), lambda i, j: (i, jnp.maximum(j - _N_QKV_TILES - _N_BCX_TILES, 0))),
        ],
        out_shape=[
            jax.ShapeDtypeStruct((t, _N_QKV_TILES * tn), BF16),
            jax.ShapeDtypeStruct((t, tn), F32),
            jax.ShapeDtypeStruct((t, tn), F32),
            jax.ShapeDtypeStruct((t, _N_BCX_TILES * tn), BF16),
            jax.ShapeDtypeStruct((t, _N_GAB_TILES * tn), BF16),
        ],
        scratch_shapes=[pltpu.VMEM((tm, D_MODEL), BF16)],
        compiler_params=_params(2),
        name="in_proj",
    )(x2d, norm_g.reshape(1, D_MODEL), w_in_bf)


def _scores(qm, k):
    return lax.dot_general(qm, k, (((1,), (1,)), ((), ())), preferred_element_type=F32)


def _tile_lanes(x, n):
    return x if n == 1 else jnp.concatenate([x] * n, axis=1)


def _with_ones(v):
    return jnp.concatenate([v, jnp.ones_like(v)], axis=1)


def _softmax_step(s, v_ext, state, shift=None):
    m_ref, acc_ref = state
    keys = s.shape[1]
    assert keys % LANE == 0
    row_max = jnp.max(s, axis=-1, keepdims=True)
    if shift is not None:
        row_max = row_max + shift
    m_old = m_ref[...]
    m_new = jnp.maximum(m_old, row_max)
    sub = m_new if shift is None else m_new - shift
    p = jnp.exp(s - _tile_lanes(sub, keys // LANE))
    pv = jnp.dot(p.astype(BF16), v_ext, preferred_element_type=F32)
    alpha = jnp.exp(m_old - m_new)
    acc_ref[...] = _tile_lanes(alpha, 2 * D_HEAD // LANE) * acc_ref[...] + pv
    m_ref[...] = m_new


def _split_maps(q):
    lane = lax.broadcasted_iota(jnp.int32, q.shape, 1)
    zero = jnp.zeros_like(q)
    return jnp.where(lane < D_QK, q, zero), jnp.where(lane >= D_QK, q, zero)


def _lambda(lq1_ref, lk1_ref, lq2_ref, lk2_ref, lambda_init):
    s1 = jnp.sum(lq1_ref[...] * lk1_ref[...], axis=-1, keepdims=True)
    s2 = jnp.sum(lq2_ref[...] * lk2_ref[...], axis=-1, keepdims=True)
    return jnp.exp(s1) - jnp.exp(s2) + lambda_init


def _attn_finish(state1, state2, lam, g, lambda_init):
    a1 = state1[1][...]
    a2 = state2[1][...]
    o = a1[:, :D_HEAD] / a1[:, D_HEAD:] - lam * (a2[:, :D_HEAD] / a2[:, D_HEAD:])
    return _rms(o, g) * (1.0 - lambda_init)


def _attn_prompt_kernel(tq, nq, lambda_init, n_cast, table_ref, lq1_ref, lk1_ref, lq2_ref, lk2_ref, g_ref,
                        q_ref, k_ref, v_ref, bnear_ref, bdiag_ref, *rest):
    cast_in, o_ref, cast_out = rest[:n_cast], rest[n_cast], rest[n_cast + 1:2 * n_cast + 1]
    m1_ref, a1_ref, m2_ref, a2_ref, sa1_ref, sa2_ref, sb1_ref, sb2_ref = rest[2 * n_cast + 1:]
    _cast_slabs(cast_in, cast_out)
    h = pl.program_id(1)
    far_bias = table_ref[FAR_BUCKET, h]
    lam = _lambda(lq1_ref, lk1_ref, lq2_ref, lk2_ref, lambda_init)
    st1 = (m1_ref, a1_ref)
    st2 = (m2_ref, a2_ref)
    buf_a = (sa1_ref, sa2_ref)
    buf_b = (sb1_ref, sb2_ref)

    def q_tile(qi, carry):
        q0 = pl.multiple_of(qi * tq, tq)
        q1, q2 = _split_maps(q_ref[pl.ds(q0, tq), :])

        def score_block(j, buf):
            k = k_ref[pl.ds(pl.multiple_of(j * tq, tq), tq), :]
            buf[0][...] = _scores(q1, k)
            buf[1][...] = _scores(q2, k)

        def score_next_tile():
            qn1, qn2 = _split_maps(q_ref[pl.ds(pl.multiple_of(q0 + tq, tq), tq), :])
            k = k_ref[0:tq, :]
            buf_b[0][...] = _scores(qn1, k)
            buf_b[1][...] = _scores(qn2, k)

        def step(j, buf, nxt=None, bias=None, shift=None, next_tile=False):
            if nxt is not None:
                score_block(j + 1, nxt)
            if next_tile:
                score_next_tile()
            v = _with_ones(v_ref[pl.ds(pl.multiple_of(j * tq, tq), tq), :])
            s1 = buf[0][...]
            s2 = buf[1][...]
            if bias is not None:
                s1 = s1 + bias
                s2 = s2 + bias
            _softmax_step(s1, v, st1, shift=shift)
            _softmax_step(s2, v, st2, shift=shift)

        for m_ref, a_ref in (st1, st2):
            m_ref[...] = jnp.full(m_ref.shape, NEG_INF, F32)
            a_ref[...] = jnp.zeros(a_ref.shape, F32)

        n_far = jnp.maximum(qi - 1, 0)
        odd_far = jnp.bitwise_and(n_far, 1)
        qi_even = jnp.bitwise_and(qi, 1) == 0

        @pl.when(qi_even)
        def _():
            score_block(0, buf_a)

        @pl.when(odd_far == 1)
        def _():
            step(0, buf_a, nxt=buf_b, shift=far_bias)

        def far_pair(t, c):
            j = odd_far + 2 * t
            step(j, buf_b, nxt=buf_a, shift=far_bias)
            step(j + 1, buf_a, nxt=buf_b, shift=far_bias)
            return c

        lax.fori_loop(0, lax.shift_right_logical(n_far, 1), far_pair, 0)

        @pl.when(qi > 0)
        def _():
            step(qi - 1, buf_b, nxt=buf_a, bias=bnear_ref[...])

        feeds_next = jnp.logical_and(qi_even, qi + 1 < nq)

        @pl.when(feeds_next)
        def _():
            step(qi, buf_a, bias=bdiag_ref[...], next_tile=True)

        @pl.when(jnp.logical_not(feeds_next))
        def _():
            step(qi, buf_a, bias=bdiag_ref[...])

        o = _attn_finish(st1, st2, lam, g_ref[...], lambda_init)
        o_ref[pl.ds(q0, tq), :] = o.astype(BF16)
        return carry

    lax.fori_loop(0, nq, q_tile, 0)


def _attn_prompt(qkv, rel_bias, lam_vecs, subln_g, bias_near, bias_diag, nb, seq, tq, lambda_init,
                 cast_weights=()):
    assert seq % tq == 0 and tq % CHUNK == 0 and tq >= FAR_DIST
    qkv3 = qkv.reshape(nb, seq, 3 * WIDTH_A)
    vec = pl.BlockSpec((1, D_QK), lambda b, h: (0, 0))
    head_cols = lambda off: pl.BlockSpec((None, seq, D_HEAD), lambda b, h: (b, 0, off + h))
    tile = pl.BlockSpec((None, tq, tq), lambda b, h: (h, 0, 0))
    stat = pltpu.VMEM((tq, LANE), F32)
    acc = pltpu.VMEM((tq, 2 * D_HEAD), F32)
    cast_in, cast_out, cast_shapes = _cast_slab_specs(cast_weights, nb * N_HEADS, lambda b, h: b * N_HEADS + h)
    o, *cast = pl.pallas_call(
        functools.partial(_attn_prompt_kernel, tq, seq // tq, lambda_init, len(cast_weights)),
        grid=(nb, N_HEADS),
        in_specs=[pl.BlockSpec(memory_space=pltpu.SMEM), vec, vec, vec, vec,
                  pl.BlockSpec((1, D_HEAD), lambda b, h: (0, 0)),
                  head_cols(0), head_cols(N_HEADS), head_cols(2 * N_HEADS), tile, tile] + cast_in,
        out_specs=[pl.BlockSpec((None, seq, D_HEAD), lambda b, h: (b, 0, h))] + cast_out,
        out_shape=[jax.ShapeDtypeStruct((nb, seq, WIDTH_A), BF16)] + cast_shapes,
        scratch_shapes=[stat, acc, stat, acc] + [pltpu.VMEM((tq, tq), F32)] * 4,
        compiler_params=_params(2),
        name="attn_prompt",
    )(rel_bias, *lam_vecs, subln_g.reshape(1, D_HEAD), qkv3, qkv3, qkv3, bias_near, bias_diag, *cast_weights)
    return o.reshape(nb * seq, WIDTH_A), cast


def _softmax_two_blocks(qm, k_a, v_a, bias_a, k_b, v_b, bias_b):
    s_a = _scores(qm, k_a) + bias_a
    s_b = _scores(qm, k_b) + bias_b
    m = jnp.maximum(jnp.max(s_a, axis=-1, keepdims=True), jnp.max(s_b, axis=-1, keepdims=True))
    p_a = jnp.exp(s_a - m).astype(BF16)
    p_b = jnp.exp(s_b - m).astype(BF16)
    return jnp.dot(p_a, v_a, preferred_element_type=F32) + jnp.dot(p_b, v_b, preferred_element_type=F32)


def _attn_sample_kernel(lambda_init, past, lq1_ref, lk1_ref, lq2_ref, lk2_ref, g_ref,
                        qkv_ref, kc_ref, vc_ref, bc_ref, bn_ref, o_ref):
    lam = _lambda(lq1_ref, lk1_ref, lq2_ref, lk2_ref, lambda_init)
    for h in range(N_HEADS):
        cols = lambda part: slice((part * N_HEADS + h) * D_HEAD, (part * N_HEADS + h + 1) * D_HEAD)
        q1, q2 = _split_maps(qkv_ref[:, cols(0)])
        kn = qkv_ref[:, cols(1)]
        vn = _with_ones(qkv_ref[:, cols(2)])
        kc = kc_ref[pl.ds(h, past, stride=N_HEADS), :].astype(BF16)
        vc = _with_ones(vc_ref[pl.ds(h, past, stride=N_HEADS), :].astype(BF16))
        a1 = _softmax_two_blocks(q1, kc, vc, bc_ref[h], kn, vn, bn_ref[h])
        a2 = _softmax_two_blocks(q2, kc, vc, bc_ref[h], kn, vn, bn_ref[h])
        o = a1[:, :D_HEAD] / a1[:, D_HEAD:] - lam * (a2[:, :D_HEAD] / a2[:, D_HEAD:])
        o_ref[:, h * D_HEAD:(h + 1) * D_HEAD] = (_rms(o, g_ref[...]) * (1.0 - lambda_init)).astype(BF16)


def _attn_sample(qkv, cache_k, cache_v, layer, lam_vecs, subln_g, bias_cache, bias_new, nb, seq, past,
                 lambda_init):
    qkv3 = qkv.reshape(nb, seq, 3 * WIDTH_A)
    kc = cache_k.reshape(-1, past * N_HEADS, D_HEAD)
    vc = cache_v.reshape(-1, past * N_HEADS, D_HEAD)
    vec = pl.BlockSpec((1, D_QK), lambda b: (0, 0))
    past_rows = pl.BlockSpec((None, past * N_HEADS, D_HEAD), lambda b: (layer * nb + b, 0, 0))
    full = lambda a: pl.BlockSpec(a.shape, lambda b: (0,) * a.ndim)
    o = pl.pallas_call(
        functools.partial(_attn_sample_kernel, lambda_init, past),
        grid=(nb,),
        in_specs=[vec, vec, vec, vec, pl.BlockSpec((1, D_HEAD), lambda b: (0, 0)),
                  pl.BlockSpec((None, seq, 3 * WIDTH_A), lambda b: (b, 0, 0)), past_rows, past_rows,
                  full(bias_cache), full(bias_new)],
        out_specs=pl.BlockSpec((None, seq, WIDTH_A), lambda b: (b, 0, 0)),
        out_shape=jax.ShapeDtypeStruct((nb, seq, WIDTH_A), BF16),
        compiler_params=_params(1),
        name="attn_sample",
    )(*lam_vecs, subln_g.reshape(1, D_HEAD), qkv3, kc, vc, bias_cache, bias_new)
    return o.reshape(nb * seq, WIDTH_A)


def _causal_conv3(x3, prev8, w):
    p0 = prev8[:, SUBLANE - 2:SUBLANE - 1, :]
    p1 = prev8[:, SUBLANE - 1:SUBLANE, :]
    row = lax.broadcasted_iota(jnp.int32, x3.shape, 1)
    x1 = jnp.where(row == 0, p1, pltpu.roll(x3, 1, axis=1))
    x2 = jnp.where(row == 0, p0, jnp.where(row == 1, p1, pltpu.roll(x3, 2, axis=1)))
    return x2 * w[0:1, :] + x1 * w[1:2, :] + x3 * w[2:3, :]


def _conv_with_state(x, i, nseq, tps, st_ref, so_ref, halo):
    rows = x.shape[0] // nseq
    x3 = x.reshape(nseq, rows, x.shape[1])
    last8 = x3[:, rows - SUBLANE:, :]
    if tps == 1:
        prev8 = st_ref[...]
    else:
        @pl.when(i % tps == 0)
        def _():
            halo[...] = st_ref[...]

        prev8 = halo[...]
        halo[...] = last8
    so_ref[...] = last8
    return x3, prev8


def _merge_kernel(nseq, tps, n_cast, o_ref, b_ref, c_ref, xi_ref, ga_ref, gb_ref, x_ref, st_ref, cw_ref,
                  wa_ref, wb_ref, wo_ref, g2_ref, *rest):
    cast_in, (h_ref, hn_ref, so_ref) = rest[:n_cast], rest[n_cast:n_cast + 3]
    cast_out, scratch = rest[n_cast + 3:2 * n_cast + 3], rest[2 * n_cast + 3:]
    _cast_slabs(cast_in, cast_out)
    i = pl.program_id(0)
    u = c_ref[...].astype(F32) * xi_ref[...].astype(F32)
    u3, prev8 = _conv_with_state(u, i, nseq, tps, st_ref, so_ref, scratch[0] if tps > 1 else None)
    z = _causal_conv3(u3, prev8, cw_ref[...]).reshape(u.shape)
    ob = (b_ref[...].astype(F32) * z).astype(BF16)
    ya = jnp.dot(o_ref[...], wa_ref[...], preferred_element_type=F32)
    yb = jnp.dot(ob, wb_ref[...], preferred_element_type=F32)
    merged = jax.nn.sigmoid(ga_ref[...].astype(F32)) * ya + jax.nn.sigmoid(gb_ref[...].astype(F32)) * yb
    hres = x_ref[...] + jnp.dot(merged.astype(BF16), wo_ref[...], preferred_element_type=F32)
    h_ref[...] = hres
    hn_ref[...] = _rms(hres, g2_ref[...]).astype(BF16)


def _merge(o, bcx, gab, x2d, state8, conv_w, wa, wb, wo, norm2_g, nb, seq, rows, cast_weights=()):
    t = x2d.shape[0]
    nseq = max(rows // seq, 1)
    tps = max(seq // rows, 1)
    assert t % rows == 0 and (rows % seq == 0 or seq % rows == 0)
    tok = lambda width, col: pl.BlockSpec((rows, width), lambda i: (i, col))
    full = lambda a: pl.BlockSpec(a.shape, lambda i: (0,) * a.ndim)
    state_spec = pl.BlockSpec((nseq, SUBLANE, WIDTH_B), lambda i: (i // tps, 0, 0))
    scratch = [pltpu.VMEM((nseq, SUBLANE, WIDTH_B), F32)] if tps > 1 else []
    g2 = norm2_g.reshape(1, D_MODEL)
    cast_in, cast_out, cast_shapes = _cast_slab_specs(cast_weights, t // rows, lambda i: i)
    hres, hn, state_rows, *cast = pl.pallas_call(
        functools.partial(_merge_kernel, nseq, tps, len(cast_weights)),
        grid=(t // rows,),
        in_specs=[tok(WIDTH_A, 0), tok(WIDTH_B, 0), tok(WIDTH_B, 1), tok(WIDTH_B, 2),
                  tok(D_MODEL, 0), tok(D_MODEL, 1), tok(D_MODEL, 0), state_spec,
                  full(conv_w), full(wa), full(wb), full(wo), full(g2)] + cast_in,
        out_specs=[tok(D_MODEL, 0), tok(D_MODEL, 0), state_spec] + cast_out,
        out_shape=[jax.ShapeDtypeStruct((t, D_MODEL), F32),
                   jax.ShapeDtypeStruct((t, D_MODEL), BF16),
                   jax.ShapeDtypeStruct((nb, SUBLANE, WIDTH_B), F32)] + cast_shapes,
        scratch_shapes=scratch,
        compiler_params=_params(1),
        name="merge",
    )(o, bcx, bcx, bcx, gab, gab, x2d, state8, conv_w, wa, wb, wo, g2, *cast_weights)
    return hres, hn, state_rows, cast


FFN_TILE_F = 512
FFN_SUB_TILES = 2


def _ffn_kernel(nseq, tps, nf, hn_ref, h_ref, wg_ref, wv_ref, cwg_ref, cwv_ref, stg_ref, stv_ref,
                wd_ref, fg_ref, y_ref, sog_ref, sov_ref, *scratch):
    i = pl.program_id(0)
    f = pl.program_id(1)
    rows = hn_ref.shape[0]
    sub = rows // FFN_SUB_TILES
    branches = ((wg_ref, cwg_ref, stg_ref, sog_ref), (wv_ref, cwv_ref, stv_ref, sov_ref))

    @pl.when(f == 0)
    def _():
        y_ref[...] = jnp.zeros_like(y_ref)

    if tps > 1:
        halos = [scratch[0].at[2 * f + slot] for slot in range(2)]

        @pl.when(i % tps == 0)
        def _():
            for halo, (_, _, st_ref, _) in zip(halos, branches):
                halo[...] = st_ref[...]

        prev = [halo[...] for halo in halos]
    else:
        assert sub % (rows // nseq) == 0
        seqs = nseq // FFN_SUB_TILES

    for r in range(FFN_SUB_TILES):
        tile_rows = slice(r * sub, (r + 1) * sub)
        hn = hn_ref[tile_rows, :]
        convs = []
        for slot, (w_ref, cw_ref, st_ref, so_ref) in enumerate(branches):
            up = jnp.dot(hn, w_ref[...], preferred_element_type=F32)
            if tps > 1:
                up3 = up.reshape(1, sub, up.shape[1])
                prev8 = prev[slot]
                prev[slot] = up3[:, sub - SUBLANE:, :]
            else:
                up3 = up.reshape(seqs, sub // seqs, up.shape[1])
                prev8 = st_ref[r * seqs:(r + 1) * seqs]
                so_ref[r * seqs:(r + 1) * seqs] = up3[:, sub // seqs - SUBLANE:, :]
            convs.append(_causal_conv3(up3, prev8, cw_ref[...]).reshape(up.shape))
        act = (jax.nn.silu(convs[0]) * convs[1]).astype(BF16)
        part = jnp.dot(act, wd_ref[...], preferred_element_type=F32)
        y_ref[tile_rows, :] += part

    if tps > 1:
        for halo, last8, (_, _, _, so_ref) in zip(halos, prev, branches):
            halo[...] = last8
            so_ref[...] = last8

    @pl.when(f == nf - 1)
    def _():
        y_ref[...] = _rms(h_ref[...] + y_ref[...], fg_ref[...])


def _ffn(hn, hres, w_up, ffn_conv_w, state8, w_down, final_g, nb, seq, rows):
    t = hn.shape[0]
    tf = FFN_TILE_F
    nf = D_FF // tf
    nseq = max(rows // seq, 1)
    tps = max(seq // rows, 1)
    assert D_FF % tf == 0 and t % rows == 0 and (rows % seq == 0 or seq % rows == 0)
    tok = pl.BlockSpec((rows, D_MODEL), lambda i, f: (i, 0))
    gate_cols = lambda r: pl.BlockSpec((r, tf), lambda i, f: (0, f))
    val_cols = lambda r: pl.BlockSpec((r, tf), lambda i, f: (0, nf + f))
    st_gate = pl.BlockSpec((nseq, SUBLANE, tf), lambda i, f: (i // tps, 0, f))
    st_val = pl.BlockSpec((nseq, SUBLANE, tf), lambda i, f: (i // tps, 0, nf + f))
    resid = pl.BlockSpec((rows, D_MODEL), lambda i, f: (i, 0), pipeline_mode=pl.Buffered(1))
    scratch = [pltpu.VMEM((2 * nf, nseq, SUBLANE, tf), F32)] if tps > 1 else []
    tile_rows = pl.BlockSpec((nseq, SUBLANE, tf), lambda i, f: (i, 0, f))
    fg = final_g.reshape(1, D_MODEL)
    y, tail_g, tail_v = pl.pallas_call(
        functools.partial(_ffn_kernel, nseq, tps, nf),
        grid=(t // rows, nf),
        in_specs=[tok, resid, gate_cols(D_MODEL), val_cols(D_MODEL), gate_cols(CONV_W), val_cols(CONV_W),
                  st_gate, st_val,
                  pl.BlockSpec((tf, D_MODEL), lambda i, f: (f, 0)),
                  pl.BlockSpec((1, D_MODEL), lambda i, f: (0, 0))],
        out_specs=[tok, tile_rows, tile_rows],
        out_shape=[jax.ShapeDtypeStruct((t, D_MODEL), F32),
                   jax.ShapeDtypeStruct((nb * tps, SUBLANE, D_FF), F32),
                   jax.ShapeDtypeStruct((nb * tps, SUBLANE, D_FF), F32)],
        scratch_shapes=scratch,
        compiler_params=_params(2),
        name="ffn",
    )(hn, hres, w_up, w_up, ffn_conv_w, ffn_conv_w, state8, state8, w_down, fg)
    return y, tail_g[tps - 1::tps], tail_v[tps - 1::tps]


def _state8(state):
    return jnp.pad(state, ((0, 0), (SUBLANE - (CONV_W - 1), 0), (0, 0)))


def _state_out(state8):
    return state8[:, SUBLANE - (CONV_W - 1):, :]


def _layer(x2d, nb, seq, attn_fn, conv8, ffn8, lp, tiles, cast_in_attn=(), cast_in_merge=()):
    lp = dict(lp)
    qkv, k_f32, v_f32, bcx, gab = _in_proj(x2d, lp["norm1_g"], lp["w_in"], tiles["in"])
    o, done = attn_fn(qkv, [lp[name] for name in cast_in_attn])
    lp.update(zip(cast_in_attn, done))
    hres, hn, conv_out8, done = _merge(o, bcx, gab, x2d, conv8, lp["conv_w"], lp["w_proj_a"], lp["w_proj_b"],
                                       lp["w_o"], lp["norm2_g"], nb, seq, tiles["merge"],
                                       [lp[name] for name in cast_in_merge])
    lp.update(zip(cast_in_merge, done))
    y, ffn_g8, ffn_v8 = _ffn(hn, hres, lp["w_up"], lp["ffn_conv_w"], ffn8, lp["w_down"], lp["final_g"],
                             nb, seq, tiles["ffn"])
    k = k_f32.reshape(nb, seq, N_HEADS, D_HEAD)
    v = v_f32.reshape(nb, seq, N_HEADS, D_HEAD)
    conv_state = _state_out(conv_out8)
    ffn_state = jnp.concatenate([_state_out(ffn_g8), _state_out(ffn_v8)], axis=-1)
    return (y, k, v, conv_state, ffn_state), lp


PROMPT_TILES = {"in": 512, "merge": 256, "ffn": 1024}
SAMPLE_TILES = {"in": 512, "merge": 256, "ffn": 1024}
ATTN_TILE_Q = 512


def kernel(x_prompt, x_sample, cache_k, cache_v, state_conv_mix, state_conv_ffn, rel_bias, norm1_g, w_in,
           lambda_q1, lambda_k1, lambda_q2, lambda_k2, subln_g, conv_w, w_proj_a, w_proj_b, w_o, norm2_g,
           w_up, ffn_conv_w, w_down, final_g):
    depth = w_in.shape[0]
    nbp, seq_p, _ = x_prompt.shape
    nbs, seq_s, _ = x_sample.shape
    past = cache_k.shape[2]
    tq = ATTN_TILE_Q

    bias_near, bias_diag, bias_cache, bias_new = _bias_tiles(
        rel_bias,
        ((tq, tq, tq, 0), (tq, tq, tq, tq), (seq_s, past, past, 0), (seq_s, seq_s, past, past)))

    hp = x_prompt.reshape(nbp * seq_p, D_MODEL)
    hs = x_sample.reshape(nbs * seq_s, D_MODEL)
    outs_p, outs_s = [], []
    for d in range(depth):
        assert depth == 1, "the ffn kernel ends with the final RMSNorm, so it is written for a single layer"
        lp = {
            "norm1_g": norm1_g[d], "w_in": w_in[d].astype(BF16), "conv_w": conv_w[d],
            "w_proj_a": w_proj_a[d], "w_proj_b": w_proj_b[d], "w_o": w_o[d], "norm2_g": norm2_g[d],
            "w_up": w_up[d], "ffn_conv_w": ffn_conv_w[d], "w_down": w_down[d], "final_g": final_g,
        }
        li = _lambda_init(d)
        lam_vecs = tuple(a[d].reshape(1, D_QK) for a in (lambda_q1, lambda_k1, lambda_q2, lambda_k2))

        attn_p = lambda qkv, cast: _attn_prompt(qkv, rel_bias, lam_vecs, subln_g[d], bias_near, bias_diag,
                                                nbp, seq_p, tq, li, cast)
        zeros_mix = jnp.zeros((nbp, SUBLANE, WIDTH_B), F32)
        zeros_ffn = jnp.zeros((nbp, SUBLANE, 2 * D_FF), F32)
        (hp, kp, vp, cmp_, cfp), lp = _layer(
            hp, nbp, seq_p, attn_p, zeros_mix, zeros_ffn, lp, PROMPT_TILES,
            cast_in_attn=("w_proj_a", "w_proj_b", "w_o", "w_down"), cast_in_merge=("w_up",))

        attn_s = lambda qkv, cast: (_attn_sample(qkv, cache_k, cache_v, d, lam_vecs, subln_g[d], bias_cache,
                                                 bias_new, nbs, seq_s, past, li), ())
        (hs, ks, vs, cms, cfs), _ = _layer(hs, nbs, seq_s, attn_s, _state8(state_conv_mix[d]),
                                           _state8(state_conv_ffn[d]), lp, SAMPLE_TILES)
        outs_p.append((kp, vp, cmp_, cfp))
        outs_s.append((ks, vs, cms, cfs))

    stack = lambda outs, idx: jnp.stack([o[idx] for o in outs])
    return (hp.reshape(nbp, seq_p, D_MODEL), hs.reshape(nbs, seq_s, D_MODEL),
            stack(outs_p, 0), stack(outs_p, 1), stack(outs_p, 2), stack(outs_p, 3),
            stack(outs_s, 0), stack(outs_s, 1), stack(outs_s, 2), stack(outs_s, 3))
```

```python
import functools
import math

import jax
import jax.numpy as jnp
from jax import lax
from jax.experimental import pallas as pl
from jax.experimental.pallas import tpu as pltpu

D_MODEL = 2048
N_HEADS = 8
D_QK = 64
D_HEAD = 2 * D_QK
WIDTH_A = N_HEADS * D_HEAD
WIDTH_B = D_MODEL // 2
D_FF = 5632
CONV_W = 3
CHUNK = 64
N_BUCKETS = 32
MAX_DIST = 128
EPS = 1e-6
NEG_INF = -1e30

SUBLANE = 8
LANE = 128
VMEM_LIMIT_BYTES = 56 * 1024 * 1024

F32 = jnp.float32
BF16 = jnp.bfloat16


def _bucket_thresholds():
    nb = N_BUCKETS // 2
    max_exact = nb // 2
    steps = nb - max_exact
    out = []
    for k in range(1, steps):
        n = max_exact
        while n ** steps * max_exact ** k < MAX_DIST ** k * max_exact ** steps:
            n += 1
        out.append(n)
    return tuple(out)


_BUCKET_THRESHOLDS = _bucket_thresholds()
FAR_DIST = _BUCKET_THRESHOLDS[-1]
FAR_BUCKET = N_BUCKETS // 2 - 1
_CHUNK_SHIFT = CHUNK.bit_length() - 1
assert 1 << _CHUNK_SHIFT == CHUNK


def _bucket_of(rel):
    nb = N_BUCKETS // 2
    max_exact = nb // 2
    n = abs(rel)
    large = max_exact + sum(n >= thr for thr in _BUCKET_THRESHOLDS)
    return (nb if rel > 0 else 0) + (n if n < max_exact else large)


def _lambda_init(layer_idx):
    return 0.8 - 0.6 * math.exp(-0.3 * layer_idx)


def _rms(x, g):
    return x * lax.rsqrt(jnp.mean(x * x, axis=-1, keepdims=True) + EPS) * g


def _params(n_axes):
    return pltpu.CompilerParams(
        dimension_semantics=("arbitrary",) * n_axes, vmem_limit_bytes=VMEM_LIMIT_BYTES)


def _cast_slabs(in_refs, out_refs):
    for i_ref, o_ref in zip(in_refs, out_refs):
        o_ref[...] = i_ref[...].astype(BF16)


def _cast_slab_specs(weights, n_steps, step_of):
    specs, shapes = [], []
    for w in weights:
        rows, cols = w.shape
        assert rows % n_steps == 0 and (rows // n_steps) % (2 * SUBLANE) == 0 and cols % LANE == 0
        specs.append(pl.BlockSpec((rows // n_steps, cols), lambda *idx: (step_of(*idx), 0)))
        shapes.append(jax.ShapeDtypeStruct(w.shape, BF16))
    return specs, list(specs), shapes


def _bias_tile_kernel(specs, table_ref, *refs):
    n_cast = (len(refs) - len(specs)) // 2
    cast_in, out_refs, cast_out = refs[:n_cast], refs[n_cast:n_cast + len(specs)], refs[n_cast + len(specs):]
    _cast_slabs(cast_in, cast_out)
    h = pl.program_id(0)
    nb = N_BUCKETS // 2
    max_exact = nb // 2
    for (rows, cols, q0, k0), out_ref in zip(specs, out_refs):
        qpos = q0 + lax.broadcasted_iota(jnp.int32, (rows, cols), 0)
        kpos = k0 + lax.broadcasted_iota(jnp.int32, (rows, cols), 1)
        rel = kpos - qpos
        n = jnp.abs(rel)
        large = jnp.full((rows, cols), max_exact, jnp.int32)
        for thr in _BUCKET_THRESHOLDS:
            large = large + jnp.where(n >= thr, 1, 0)
        bucket = jnp.where(rel > 0, nb, 0) + jnp.where(n < max_exact, n, large)
        bias = jnp.zeros((rows, cols), F32)
        rel_lo = k0 - (q0 + rows - 1)
        rel_hi = min(k0 + cols - 1 - q0, CHUNK - 1)
        for b in sorted({_bucket_of(r) for r in range(rel_lo, rel_hi + 1)}):
            bias = jnp.where(bucket == b, table_ref[b, h], bias)
        visible = (lax.shift_right_logical(kpos, _CHUNK_SHIFT)
                   <= lax.shift_right_logical(qpos, _CHUNK_SHIFT))
        out_ref[...] = jnp.where(visible, bias, NEG_INF)


def _bias_tiles(rel_bias, specs, cast_weights=()):
    cast_in, cast_out, cast_shapes = _cast_slab_specs(cast_weights, N_HEADS, lambda h: h)
    outs = pl.pallas_call(
        functools.partial(_bias_tile_kernel, specs),
        grid=(N_HEADS,),
        in_specs=[pl.BlockSpec(memory_space=pltpu.SMEM)] + cast_in,
        out_specs=[pl.BlockSpec((None, r, c), lambda h: (h, 0, 0)) for r, c, _, _ in specs] + cast_out,
        out_shape=[jax.ShapeDtypeStruct((N_HEADS, r, c), F32) for r, c, _, _ in specs] + cast_shapes,
        compiler_params=_params(1),
        name="bias_tiles",
    )(rel_bias, *cast_weights)
    return outs[:len(specs)], outs[len(specs):]


IN_TILE_N = 2048
IN_WIDTH = 3 * WIDTH_A + 3 * WIDTH_B + 2 * D_MODEL
COL_K = WIDTH_A
COL_V = 2 * WIDTH_A
COL_B = 3 * WIDTH_A
COL_C = COL_B + WIDTH_B
COL_X = COL_C + WIDTH_B
COL_GA = COL_X + WIDTH_B
COL_GB = COL_GA + D_MODEL


def _in_proj_kernel(k_at, v_at, x_ref, g_ref, w_ref, cs_ref, proj_ref, kf_ref, vf_ref, xn_ref):
    j = pl.program_id(1)

    @pl.when(j == 0)
    def _():
        xn_ref[...] = _rms(x_ref[...], g_ref[...]).astype(BF16)

    y = jnp.dot(xn_ref[...], w_ref[...], preferred_element_type=F32)
    proj_ref[...] = (y * cs_ref[...]).astype(BF16)

    for (tile, off), f32_ref in ((k_at, kf_ref), (v_at, vf_ref)):
        @pl.when(j == tile)
        def _():
            f32_ref[...] = y[:, off:off + WIDTH_A]


def _in_proj(x2d, norm_g, w_in_bf, tm):
    t = x2d.shape[0]
    tn = IN_TILE_N
    assert t % tm == 0 and w_in_bf.shape == (D_MODEL, IN_WIDTH) and IN_WIDTH % tn == 0
    k_at, v_at = divmod(COL_K, tn), divmod(COL_V, tn)
    assert k_at[1] + WIDTH_A <= tn and v_at[1] + WIDTH_A <= tn
    col_scale = jnp.concatenate([jnp.full((1, WIDTH_A), D_QK ** -0.5, F32),
                                 jnp.ones((1, IN_WIDTH - WIDTH_A), F32)], axis=1)
    f32_cols = pl.BlockSpec((tm, WIDTH_A), lambda i, j: (i, 0))
    return pl.pallas_call(
        functools.partial(_in_proj_kernel, k_at, v_at),
        grid=(t // tm, IN_WIDTH // tn),
        in_specs=[
            pl.BlockSpec((tm, D_MODEL), lambda i, j: (i, 0)),
            pl.BlockSpec((1, D_MODEL), lambda i, j: (0, 0)),
            pl.BlockSpec((D_MODEL, tn), lambda i, j: (0, j)),
            pl.BlockSpec((1, tn), lambda i, j: (0, j)),
        ],
        out_specs=[pl.BlockSpec((tm, tn), lambda i, j: (i, j)), f32_cols, f32_cols],
        out_shape=[
            jax.ShapeDtypeStruct((t, IN_WIDTH), BF16),
            jax.ShapeDtypeStruct((t, WIDTH_A), F32),
            jax.ShapeDtypeStruct((t, WIDTH_A), F32),
        ],
        scratch_shapes=[pltpu.VMEM((tm, D_MODEL), BF16)],
        compiler_params=_params(2),
        name="in_proj",
    )(x2d, norm_g.reshape(1, D_MODEL), w_in_bf, col_scale)


def _scores(qm, k):
    return lax.dot_general(qm, k, (((1,), (1,)), ((), ())), preferred_element_type=F32)


def _tile_lanes(x, n):
    return x if n == 1 else jnp.concatenate([x] * n, axis=1)


def _with_ones(v):
    return jnp.concatenate([v, jnp.ones_like(v)], axis=1)


def _softmax_step(s, v_ext, state, shift=None):
    m_ref, acc_ref = state
    keys = s.shape[1]
    assert keys % LANE == 0
    row_max = jnp.max(s, axis=-1, keepdims=True)
    if shift is not None:
        row_max = row_max + shift
    m_old = m_ref[...]
    m_new = jnp.maximum(m_old, row_max)
    sub = m_new if shift is None else m_new - shift
    p = jnp.exp(s - _tile_lanes(sub, keys // LANE))
    pv = jnp.dot(p.astype(BF16), v_ext, preferred_element_type=F32)
    alpha = jnp.exp(m_old - m_new)
    acc_ref[...] = _tile_lanes(alpha, 2 * D_HEAD // LANE) * acc_ref[...] + pv
    m_ref[...] = m_new


def _split_maps(q):
    lane = lax.broadcasted_iota(jnp.int32, q.shape, 1)
    zero = jnp.zeros_like(q)
    return jnp.where(lane < D_QK, q, zero), jnp.where(lane >= D_QK, q, zero)


def _lambda(lq1_ref, lk1_ref, lq2_ref, lk2_ref, lambda_init):
    s1 = jnp.sum(lq1_ref[...] * lk1_ref[...], axis=-1, keepdims=True)
    s2 = jnp.sum(lq2_ref[...] * lk2_ref[...], axis=-1, keepdims=True)
    return jnp.exp(s1) - jnp.exp(s2) + lambda_init


def _attn_finish(state1, state2, lam, g, lambda_init):
    a1 = state1[1][...]
    a2 = state2[1][...]
    o = a1[:, :D_HEAD] / a1[:, D_HEAD:] - lam * (a2[:, :D_HEAD] / a2[:, D_HEAD:])
    return _rms(o, g) * (1.0 - lambda_init)


def _attn_prompt_kernel(tq, nq, lambda_init, n_cast, table_ref, lq1_ref, lk1_ref, lq2_ref, lk2_ref, g_ref,
                        q_ref, k_ref, v_ref, bnear_ref, bdiag_ref, *rest):
    cast_in, o_ref, cast_out = rest[:n_cast], rest[n_cast], rest[n_cast + 1:2 * n_cast + 1]
    m1_ref, a1_ref, m2_ref, a2_ref, sa1_ref, sa2_ref, sb1_ref, sb2_ref = rest[2 * n_cast + 1:]
    _cast_slabs(cast_in, cast_out)
    h = pl.program_id(1)
    far_bias = table_ref[FAR_BUCKET, h]
    lam = _lambda(lq1_ref, lk1_ref, lq2_ref, lk2_ref, lambda_init)
    st1 = (m1_ref, a1_ref)
    st2 = (m2_ref, a2_ref)
    buf_a = (sa1_ref, sa2_ref)
    buf_b = (sb1_ref, sb2_ref)

    def q_tile(qi, carry):
        q0 = pl.multiple_of(qi * tq, tq)
        q1, q2 = _split_maps(q_ref[pl.ds(q0, tq), :])

        def score_block(j, buf):
            k = k_ref[pl.ds(pl.multiple_of(j * tq, tq), tq), :]
            buf[0][...] = _scores(q1, k)
            buf[1][...] = _scores(q2, k)

        def score_next_tile():
            qn1, qn2 = _split_maps(q_ref[pl.ds(pl.multiple_of(q0 + tq, tq), tq), :])
            k = k_ref[0:tq, :]
            buf_b[0][...] = _scores(qn1, k)
            buf_b[1][...] = _scores(qn2, k)

        def step(j, buf, nxt=None, bias=None, shift=None, next_tile=False):
            if nxt is not None:
                score_block(j + 1, nxt)
            if next_tile:
                score_next_tile()
            v = _with_ones(v_ref[pl.ds(pl.multiple_of(j * tq, tq), tq), :])
            s1 = buf[0][...]
            s2 = buf[1][...]
            if bias is not None:
                s1 = s1 + bias
                s2 = s2 + bias
            _softmax_step(s1, v, st1, shift=shift)
            _softmax_step(s2, v, st2, shift=shift)

        for m_ref, a_ref in (st1, st2):
            m_ref[...] = jnp.full(m_ref.shape, NEG_INF, F32)
            a_ref[...] = jnp.zeros(a_ref.shape, F32)

        n_far = jnp.maximum(qi - 1, 0)
        odd_far = jnp.bitwise_and(n_far, 1)
        qi_even = jnp.bitwise_and(qi, 1) == 0

        @pl.when(qi_even)
        def _():
            score_block(0, buf_a)

        @pl.when(odd_far == 1)
        def _():
            step(0, buf_a, nxt=buf_b, shift=far_bias)

        def far_pair(t, c):
            j = odd_far + 2 * t
            step(j, buf_b, nxt=buf_a, shift=far_bias)
            step(j + 1, buf_a, nxt=buf_b, shift=far_bias)
            return c

        lax.fori_loop(0, lax.shift_right_logical(n_far, 1), far_pair, 0)

        @pl.when(qi > 0)
        def _():
            step(qi - 1, buf_b, nxt=buf_a, bias=bnear_ref[...])

        feeds_next = jnp.logical_and(qi_even, qi + 1 < nq)

        @pl.when(feeds_next)
        def _():
            step(qi, buf_a, bias=bdiag_ref[...], next_tile=True)

        @pl.when(jnp.logical_not(feeds_next))
        def _():
            step(qi, buf_a, bias=bdiag_ref[...])

        o = _attn_finish(st1, st2, lam, g_ref[...], lambda_init)
        o_ref[pl.ds(q0, tq), :] = o.astype(BF16)
        return carry

    lax.fori_loop(0, nq, q_tile, 0)


def _attn_prompt(proj, rel_bias, lam_vecs, subln_g, bias_near, bias_diag, nb, seq, tq, lambda_init,
                 cast_weights=()):
    assert seq % tq == 0 and tq % CHUNK == 0 and tq >= FAR_DIST
    qkv3 = proj.reshape(nb, seq, IN_WIDTH)
    vec = pl.BlockSpec((1, D_QK), lambda b, h: (0, 0))
    head_cols = lambda off: pl.BlockSpec((None, seq, D_HEAD), lambda b, h: (b, 0, off + h))
    tile = pl.BlockSpec((None, tq, tq), lambda b, h: (h, 0, 0))
    stat = pltpu.VMEM((tq, LANE), F32)
    acc = pltpu.VMEM((tq, 2 * D_HEAD), F32)
    cast_in, cast_out, cast_shapes = _cast_slab_specs(cast_weights, nb * N_HEADS, lambda b, h: b * N_HEADS + h)
    o, *cast = pl.pallas_call(
        functools.partial(_attn_prompt_kernel, tq, seq // tq, lambda_init, len(cast_weights)),
        grid=(nb, N_HEADS),
        in_specs=[pl.BlockSpec(memory_space=pltpu.SMEM), vec, vec, vec, vec,
                  pl.BlockSpec((1, D_HEAD), lambda b, h: (0, 0)),
                  head_cols(0), head_cols(COL_K // D_HEAD), head_cols(COL_V // D_HEAD), tile, tile] + cast_in,
        out_specs=[pl.BlockSpec((None, seq, D_HEAD), lambda b, h: (b, 0, h))] + cast_out,
        out_shape=[jax.ShapeDtypeStruct((nb, seq, WIDTH_A), BF16)] + cast_shapes,
        scratch_shapes=[stat, acc, stat, acc] + [pltpu.VMEM((tq, tq), F32)] * 4,
        compiler_params=_params(2),
        name="attn_prompt",
    )(rel_bias, *lam_vecs, subln_g.reshape(1, D_HEAD), qkv3, qkv3, qkv3, bias_near, bias_diag, *cast_weights)
    return o.reshape(nb * seq, WIDTH_A), cast


def _softmax_two_blocks(qm, k_a, v_a, bias_a, k_b, v_b, bias_b):
    s_a = _scores(qm, k_a) + bias_a
    s_b = _scores(qm, k_b) + bias_b
    m = jnp.maximum(jnp.max(s_a, axis=-1, keepdims=True), jnp.max(s_b, axis=-1, keepdims=True))
    p_a = jnp.exp(s_a - m).astype(BF16)
    p_b = jnp.exp(s_b - m).astype(BF16)
    return jnp.dot(p_a, v_a, preferred_element_type=F32) + jnp.dot(p_b, v_b, preferred_element_type=F32)


def _attn_sample_kernel(lambda_init, past, lq1_ref, lk1_ref, lq2_ref, lk2_ref, g_ref,
                        qkv_ref, kc_ref, vc_ref, bc_ref, bn_ref, o_ref):
    lam = _lambda(lq1_ref, lk1_ref, lq2_ref, lk2_ref, lambda_init)
    for h in range(N_HEADS):
        cols = lambda part: slice((part * N_HEADS + h) * D_HEAD, (part * N_HEADS + h + 1) * D_HEAD)
        q1, q2 = _split_maps(qkv_ref[:, cols(0)])
        kn = qkv_ref[:, cols(1)]
        vn = _with_ones(qkv_ref[:, cols(2)])
        kc = kc_ref[pl.ds(h, past, stride=N_HEADS), :].astype(BF16)
        vc = _with_ones(vc_ref[pl.ds(h, past, stride=N_HEADS), :].astype(BF16))
        a1 = _softmax_two_blocks(q1, kc, vc, bc_ref[h], kn, vn, bn_ref[h])
        a2 = _softmax_two_blocks(q2, kc, vc, bc_ref[h], kn, vn, bn_ref[h])
        o = a1[:, :D_HEAD] / a1[:, D_HEAD:] - lam * (a2[:, :D_HEAD] / a2[:, D_HEAD:])
        o_ref[:, h * D_HEAD:(h + 1) * D_HEAD] = (_rms(o, g_ref[...]) * (1.0 - lambda_init)).astype(BF16)


def _attn_sample(proj, cache_k, cache_v, layer, lam_vecs, subln_g, bias_cache, bias_new, nb, seq, past,
                 lambda_init):
    qkv3 = proj.reshape(nb, seq, IN_WIDTH)
    kc = cache_k.reshape(-1, past * N_HEADS, D_HEAD)
    vc = cache_v.reshape(-1, past * N_HEADS, D_HEAD)
    vec = pl.BlockSpec((1, D_QK), lambda b: (0, 0))
    past_rows = pl.BlockSpec((None, past * N_HEADS, D_HEAD), lambda b: (layer * nb + b, 0, 0))
    full = lambda a: pl.BlockSpec(a.shape, lambda b: (0,) * a.ndim)
    o = pl.pallas_call(
        functools.partial(_attn_sample_kernel, lambda_init, past),
        grid=(nb,),
        in_specs=[vec, vec, vec, vec, pl.BlockSpec((1, D_HEAD), lambda b: (0, 0)),
                  pl.BlockSpec((None, seq, 3 * WIDTH_A), lambda b: (b, 0, 0)), past_rows, past_rows,
                  full(bias_cache), full(bias_new)],
        out_specs=pl.BlockSpec((None, seq, WIDTH_A), lambda b: (b, 0, 0)),
        out_shape=jax.ShapeDtypeStruct((nb, seq, WIDTH_A), BF16),
        compiler_params=_params(1),
        name="attn_sample",
    )(*lam_vecs, subln_g.reshape(1, D_HEAD), qkv3, kc, vc, bias_cache, bias_new)
    return o.reshape(nb * seq, WIDTH_A)


def _causal_conv3(x3, prev8, w):
    p0 = prev8[:, SUBLANE - 2:SUBLANE - 1, :]
    p1 = prev8[:, SUBLANE - 1:SUBLANE, :]
    row = lax.broadcasted_iota(jnp.int32, x3.shape, 1)
    x1 = jnp.where(row == 0, p1, pltpu.roll(x3, 1, axis=1))
    x2 = jnp.where(row == 0, p0, jnp.where(row == 1, p1, pltpu.roll(x3, 2, axis=1)))
    return x2 * w[0:1, :] + x1 * w[1:2, :] + x3 * w[2:3, :]


def _conv_with_state(x, i, nseq, tps, st_ref, so_ref, halo):
    rows = x.shape[0] // nseq
    x3 = x.reshape(nseq, rows, x.shape[1])
    last8 = x3[:, rows - SUBLANE:, :]
    if tps == 1:
        prev8 = st_ref[...]
    else:
        @pl.when(i % tps == 0)
        def _():
            halo[...] = st_ref[...]

        prev8 = halo[...]
        halo[...] = last8
    so_ref[...] = last8
    return x3, prev8


def _merge_kernel(nseq, tps, n_cast, o_ref, b_ref, c_ref, xi_ref, ga_ref, gb_ref, x_ref, st_ref, cw_ref,
                  wa_ref, wb_ref, wo_ref, g2_ref, *rest):
    cast_in, (h_ref, hn_ref, so_ref) = rest[:n_cast], rest[n_cast:n_cast + 3]
    cast_out, scratch = rest[n_cast + 3:2 * n_cast + 3], rest[2 * n_cast + 3:]
    _cast_slabs(cast_in, cast_out)
    i = pl.program_id(0)
    u = c_ref[...].astype(F32) * xi_ref[...].astype(F32)
    u3, prev8 = _conv_with_state(u, i, nseq, tps, st_ref, so_ref, scratch[0] if tps > 1 else None)
    z = _causal_conv3(u3, prev8, cw_ref[...]).reshape(u.shape)
    ob = (b_ref[...].astype(F32) * z).astype(BF16)
    ya = jnp.dot(o_ref[...], wa_ref[...], preferred_element_type=F32)
    yb = jnp.dot(ob, wb_ref[...], preferred_element_type=F32)
    merged = jax.nn.sigmoid(ga_ref[...].astype(F32)) * ya + jax.nn.sigmoid(gb_ref[...].astype(F32)) * yb
    hres = x_ref[...] + jnp.dot(merged.astype(BF16), wo_ref[...], preferred_element_type=F32)
    h_ref[...] = hres
    hn_ref[...] = _rms(hres, g2_ref[...]).astype(BF16)


def _merge(o, proj, x2d, state8, conv_w, wa, wb, wo, norm2_g, nb, seq, rows, cast_weights=()):
    t = x2d.shape[0]
    nseq = max(rows // seq, 1)
    tps = max(seq // rows, 1)
    assert t % rows == 0 and (rows % seq == 0 or seq % rows == 0)
    tok = lambda width, col: pl.BlockSpec((rows, width), lambda i: (i, col))
    full = lambda a: pl.BlockSpec(a.shape, lambda i: (0,) * a.ndim)
    state_spec = pl.BlockSpec((nseq, SUBLANE, WIDTH_B), lambda i: (i // tps, 0, 0))
    scratch = [pltpu.VMEM((nseq, SUBLANE, WIDTH_B), F32)] if tps > 1 else []
    g2 = norm2_g.reshape(1, D_MODEL)
    cast_in, cast_out, cast_shapes = _cast_slab_specs(cast_weights, t // rows, lambda i: i)
    hres, hn, state_rows, *cast = pl.pallas_call(
        functools.partial(_merge_kernel, nseq, tps, len(cast_weights)),
        grid=(t // rows,),
        in_specs=[tok(WIDTH_A, 0), tok(WIDTH_B, COL_B // WIDTH_B), tok(WIDTH_B, COL_C // WIDTH_B),
                  tok(WIDTH_B, COL_X // WIDTH_B), tok(D_MODEL, COL_GA // D_MODEL), tok(D_MODEL, COL_GB // D_MODEL),
                  tok(D_MODEL, 0), state_spec,
                  full(conv_w), full(wa), full(wb), full(wo), full(g2)] + cast_in,
        out_specs=[tok(D_MODEL, 0), tok(D_MODEL, 0), state_spec] + cast_out,
        out_shape=[jax.ShapeDtypeStruct((t, D_MODEL), F32),
                   jax.ShapeDtypeStruct((t, D_MODEL), BF16),
                   jax.ShapeDtypeStruct((nb, SUBLANE, WIDTH_B), F32)] + cast_shapes,
        scratch_shapes=scratch,
        compiler_params=_params(1),
        name="merge",
    )(o, proj, proj, proj, proj, proj, x2d, state8, conv_w, wa, wb, wo, g2, *cast_weights)
    return hres, hn, state_rows, cast


FFN_TILE_F = 512
FFN_SUB_TILES = 2


def _ffn_kernel(nseq, tps, nf, hn_ref, h_ref, wg_ref, wv_ref, cwg_ref, cwv_ref, stg_ref, stv_ref,
                wd_ref, fg_ref, y_ref, sog_ref, sov_ref, *scratch):
    i = pl.program_id(0)
    f = pl.program_id(1)
    rows = hn_ref.shape[0]
    sub = rows // FFN_SUB_TILES
    branches = ((wg_ref, cwg_ref, stg_ref, sog_ref), (wv_ref, cwv_ref, stv_ref, sov_ref))

    @pl.when(f == 0)
    def _():
        y_ref[...] = jnp.zeros_like(y_ref)

    if tps > 1:
        halos = [scratch[0].at[2 * f + slot] for slot in range(2)]

        @pl.when(i % tps == 0)
        def _():
            for halo, (_, _, st_ref, _) in zip(halos, branches):
                halo[...] = st_ref[...]

        prev = [halo[...] for halo in halos]
    else:
        assert sub % (rows // nseq) == 0
        seqs = nseq // FFN_SUB_TILES

    for r in range(FFN_SUB_TILES):
        tile_rows = slice(r * sub, (r + 1) * sub)
        hn = hn_ref[tile_rows, :]
        convs = []
        for slot, (w_ref, cw_ref, st_ref, so_ref) in enumerate(branches):
            up = jnp.dot(hn, w_ref[...], preferred_element_type=F32)
            if tps > 1:
                up3 = up.reshape(1, sub, up.shape[1])
                prev8 = prev[slot]
                prev[slot] = up3[:, sub - SUBLANE:, :]
            else:
                up3 = up.reshape(seqs, sub // seqs, up.shape[1])
                prev8 = st_ref[r * seqs:(r + 1) * seqs]
                so_ref[r * seqs:(r + 1) * seqs] = up3[:, sub // seqs - SUBLANE:, :]
            convs.append(_causal_conv3(up3, prev8, cw_ref[...]).reshape(up.shape))
        act = (jax.nn.silu(convs[0]) * convs[1]).astype(BF16)
        part = jnp.dot(act, wd_ref[...], preferred_element_type=F32)
        y_ref[tile_rows, :] += part

    if tps > 1:
        for halo, last8, (_, _, _, so_ref) in zip(halos, prev, branches):
            halo[...] = last8
            so_ref[...] = last8

    @pl.when(f == nf - 1)
    def _():
        y_ref[...] = _rms(h_ref[...] + y_ref[...], fg_ref[...])


def _ffn(hn, hres, w_up, ffn_conv_w, state8, w_down, final_g, nb, seq, rows):
    t = hn.shape[0]
    tf = FFN_TILE_F
    nf = D_FF // tf
    nseq = max(rows // seq, 1)
    tps = max(seq // rows, 1)
    assert D_FF % tf == 0 and t % rows == 0 and (rows % seq == 0 or seq % rows == 0)
    tok = pl.BlockSpec((rows, D_MODEL), lambda i, f: (i, 0))
    gate_cols = lambda r: pl.BlockSpec((r, tf), lambda i, f: (0, f))
    val_cols = lambda r: pl.BlockSpec((r, tf), lambda i, f: (0, nf + f))
    st_gate = pl.BlockSpec((nseq, SUBLANE, tf), lambda i, f: (i // tps, 0, f))
    st_val = pl.BlockSpec((nseq, SUBLANE, tf), lambda i, f: (i // tps, 0, nf + f))
    resid = pl.BlockSpec((rows, D_MODEL), lambda i, f: (i, 0), pipeline_mode=pl.Buffered(1))
    scratch = [pltpu.VMEM((2 * nf, nseq, SUBLANE, tf), F32)] if tps > 1 else []
    tile_rows = pl.BlockSpec((nseq, SUBLANE, tf), lambda i, f: (i, 0, f))
    fg = final_g.reshape(1, D_MODEL)
    y, tail_g, tail_v = pl.pallas_call(
        functools.partial(_ffn_kernel, nseq, tps, nf),
        grid=(t // rows, nf),
        in_specs=[tok, resid, gate_cols(D_MODEL), val_cols(D_MODEL), gate_cols(CONV_W), val_cols(CONV_W),
                  st_gate, st_val,
                  pl.BlockSpec((tf, D_MODEL), lambda i, f: (f, 0)),
                  pl.BlockSpec((1, D_MODEL), lambda i, f: (0, 0))],
        out_specs=[tok, tile_rows, tile_rows],
        out_shape=[jax.ShapeDtypeStruct((t, D_MODEL), F32),
                   jax.ShapeDtypeStruct((nb * tps, SUBLANE, D_FF), F32),
                   jax.ShapeDtypeStruct((nb * tps, SUBLANE, D_FF), F32)],
        scratch_shapes=scratch,
        compiler_params=_params(2),
        name="ffn",
    )(hn, hres, w_up, w_up, ffn_conv_w, ffn_conv_w, state8, state8, w_down, fg)
    return y, tail_g[tps - 1::tps], tail_v[tps - 1::tps]


def _state8(state):
    return jnp.pad(state, ((0, 0), (SUBLANE - (CONV_W - 1), 0), (0, 0)))


def _state_out(state8):
    return state8[:, SUBLANE - (CONV_W - 1):, :]


def _layer(x2d, nb, seq, attn_fn, conv8, ffn8, lp, tiles, cast_in_attn=(), cast_in_merge=()):
    lp = dict(lp)
    proj, k_f32, v_f32 = _in_proj(x2d, lp["norm1_g"], lp["w_in"], tiles["in"])
    o, done = attn_fn(proj, [lp[name] for name in cast_in_attn])
    lp.update(zip(cast_in_attn, done))
    hres, hn, conv_out8, done = _merge(o, proj, x2d, conv8, lp["conv_w"], lp["w_proj_a"], lp["w_proj_b"],
                                       lp["w_o"], lp["norm2_g"], nb, seq, tiles["merge"],
                                       [lp[name] for name in cast_in_merge])
    lp.update(zip(cast_in_merge, done))
    y, ffn_g8, ffn_v8 = _ffn(hn, hres, lp["w_up"], lp["ffn_conv_w"], ffn8, lp["w_down"], lp["final_g"],
                             nb, seq, tiles["ffn"])
    k = k_f32.reshape(nb, seq, N_HEADS, D_HEAD)
    v = v_f32.reshape(nb, seq, N_HEADS, D_HEAD)
    conv_state = _state_out(conv_out8)
    ffn_state = jnp.concatenate([_state_out(ffn_g8), _state_out(ffn_v8)], axis=-1)
    return (y, k, v, conv_state, ffn_state), lp


PROMPT_TILES = {"in": 512, "merge": 256, "ffn": 1024}
SAMPLE_TILES = {"in": 512, "merge": 256, "ffn": 1024}
ATTN_TILE_Q = 512


def kernel(x_prompt, x_sample, cache_k, cache_v, state_conv_mix, state_conv_ffn, rel_bias, norm1_g, w_in,
           lambda_q1, lambda_k1, lambda_q2, lambda_k2, subln_g, conv_w, w_proj_a, w_proj_b, w_o, norm2_g,
           w_up, ffn_conv_w, w_down, final_g):
    depth = w_in.shape[0]
    nbp, seq_p, _ = x_prompt.shape
    nbs, seq_s, _ = x_sample.shape
    past = cache_k.shape[2]
    tq = ATTN_TILE_Q

    assert depth == 1, "the ffn kernel ends with the final RMSNorm, so it is written for a single layer"
    (bias_near, bias_diag, bias_cache, bias_new), (w_in_bf,) = _bias_tiles(
        rel_bias,
        ((tq, tq, tq, 0), (tq, tq, tq, tq), (seq_s, past, past, 0), (seq_s, seq_s, past, past)),
        cast_weights=(w_in[0],))

    hp = x_prompt.reshape(nbp * seq_p, D_MODEL)
    hs = x_sample.reshape(nbs * seq_s, D_MODEL)
    outs_p, outs_s = [], []
    for d in range(depth):
        lp = {
            "norm1_g": norm1_g[d], "w_in": w_in_bf, "conv_w": conv_w[d],
            "w_proj_a": w_proj_a[d], "w_proj_b": w_proj_b[d], "w_o": w_o[d], "norm2_g": norm2_g[d],
            "w_up": w_up[d], "ffn_conv_w": ffn_conv_w[d], "w_down": w_down[d], "final_g": final_g,
        }
        li = _lambda_init(d)
        lam_vecs = tuple(a[d].reshape(1, D_QK) for a in (lambda_q1, lambda_k1, lambda_q2, lambda_k2))

        attn_p = lambda qkv, cast: _attn_prompt(qkv, rel_bias, lam_vecs, subln_g[d], bias_near, bias_diag,
                                                nbp, seq_p, tq, li, cast)
        zeros_mix = jnp.zeros((nbp, SUBLANE, WIDTH_B), F32)
        zeros_ffn = jnp.zeros((nbp, SUBLANE, 2 * D_FF), F32)
        (hp, kp, vp, cmp_, cfp), lp = _layer(
            hp, nbp, seq_p, attn_p, zeros_mix, zeros_ffn, lp, PROMPT_TILES,
            cast_in_attn=("w_proj_a", "w_proj_b", "w_o", "w_down"), cast_in_merge=("w_up",))

        attn_s = lambda qkv, cast: (_attn_sample(qkv, cache_k, cache_v, d, lam_vecs, subln_g[d], bias_cache,
                                                 bias_new, nbs, seq_s, past, li), ())
        (hs, ks, vs, cms, cfs), _ = _layer(hs, nbs, seq_s, attn_s, _state8(state_conv_mix[d]),
                                           _state8(state_conv_ffn[d]), lp, SAMPLE_TILES)
        outs_p.append((kp, vp, cmp_, cfp))
        outs_s.append((ks, vs, cms, cfs))

    stack = lambda outs, idx: jnp.stack([o[idx] for o in outs])
    return (hp.reshape(nbp, seq_p, D_MODEL), hs.reshape(nbs, seq_s, D_MODEL),
            stack(outs_p, 0), stack(outs_p, 1), stack(outs_p, 2), stack(outs_p, 3),
            stack(outs_s, 0), stack(outs_s, 1), stack(outs_s, 2), stack(outs_s, 3))
```

```python
import functools
import math

import jax
import jax.numpy as jnp
from jax import lax
from jax.experimental import pallas as pl
from jax.experimental.pallas import tpu as pltpu

D_MODEL = 2048
N_HEADS = 8
D_QK = 64
D_HEAD = 2 * D_QK
WIDTH_A = N_HEADS * D_HEAD
WIDTH_B = D_MODEL // 2
D_FF = 5632
CONV_W = 3
CHUNK = 64
N_BUCKETS = 32
MAX_DIST = 128
EPS = 1e-6
NEG_INF = -1e30

SUBLANE = 8
LANE = 128
VMEM_LIMIT_BYTES = 56 * 1024 * 1024

F32 = jnp.float32
BF16 = jnp.bfloat16


def _bucket_thresholds():
    nb = N_BUCKETS // 2
    max_exact = nb // 2
    steps = nb - max_exact
    out = []
    for k in range(1, steps):
        n = max_exact
        while n ** steps * max_exact ** k < MAX_DIST ** k * max_exact ** steps:
            n += 1
        out.append(n)
    return tuple(out)


_BUCKET_THRESHOLDS = _bucket_thresholds()
FAR_DIST = _BUCKET_THRESHOLDS[-1]
FAR_BUCKET = N_BUCKETS // 2 - 1
_CHUNK_SHIFT = CHUNK.bit_length() - 1
assert 1 << _CHUNK_SHIFT == CHUNK


def _bucket_of(rel):
    nb = N_BUCKETS // 2
    max_exact = nb // 2
    n = abs(rel)
    large = max_exact + sum(n >= thr for thr in _BUCKET_THRESHOLDS)
    return (nb if rel > 0 else 0) + (n if n < max_exact else large)


def _lambda_init(layer_idx):
    return 0.8 - 0.6 * math.exp(-0.3 * layer_idx)


def _rms(x, g):
    return x * lax.rsqrt(jnp.mean(x * x, axis=-1, keepdims=True) + EPS) * g


def _params(n_axes):
    return pltpu.CompilerParams(
        dimension_semantics=("arbitrary",) * n_axes, vmem_limit_bytes=VMEM_LIMIT_BYTES)


def _cast_slabs(in_refs, out_refs):
    for i_ref, o_ref in zip(in_refs, out_refs):
        o_ref[...] = i_ref[...].astype(BF16)


def _cast_slab_specs(weights, n_steps, step_of):
    specs, shapes = [], []
    for w in weights:
        rows, cols = w.shape
        assert rows % n_steps == 0 and (rows // n_steps) % (2 * SUBLANE) == 0 and cols % LANE == 0
        specs.append(pl.BlockSpec((rows // n_steps, cols), lambda *idx: (step_of(*idx), 0)))
        shapes.append(jax.ShapeDtypeStruct(w.shape, BF16))
    return specs, list(specs), shapes


def _bias_tile_kernel(specs, table_ref, *refs):
    n_cast = (len(refs) - len(specs)) // 2
    cast_in, out_refs, cast_out = refs[:n_cast], refs[n_cast:n_cast + len(specs)], refs[n_cast + len(specs):]
    _cast_slabs(cast_in, cast_out)
    h = pl.program_id(0)
    nb = N_BUCKETS // 2
    max_exact = nb // 2
    for (rows, cols, q0, k0), out_ref in zip(specs, out_refs):
        rel0 = k0 - q0 - (rows - SUBLANE)
        width = -(-(cols + rows - SUBLANE) // LANE) * LANE
        rel = (lax.broadcasted_iota(jnp.int32, (SUBLANE, width), 1)
               - lax.broadcasted_iota(jnp.int32, (SUBLANE, width), 0) + rel0)
        n = jnp.abs(rel)
        large = jnp.full((SUBLANE, width), max_exact, jnp.int32)
        for thr in _BUCKET_THRESHOLDS:
            large = large + jnp.where(n >= thr, 1, 0)
        bucket = jnp.where(rel > 0, nb, 0) + jnp.where(n < max_exact, n, large)
        strip = jnp.zeros((SUBLANE, width), F32)
        rel_hi = min(rel0 + width - 1, CHUNK - 1)
        for b in sorted({_bucket_of(r) for r in range(rel0 - SUBLANE + 1, rel_hi + 1)}):
            strip = jnp.where(bucket == b, table_ref[b, h], strip)
        assert q0 % SUBLANE == 0 and CHUNK % SUBLANE == 0
        k_chunk = lax.shift_right_logical(k0 + lax.broadcasted_iota(jnp.int32, (SUBLANE, cols), 1), _CHUNK_SHIFT)
        for a in range(rows // SUBLANE):
            start = rows - SUBLANE * (a + 1)
            visible = k_chunk <= (q0 + SUBLANE * a) // CHUNK
            out_ref[SUBLANE * a:SUBLANE * (a + 1), :] = jnp.where(visible, strip[:, start:start + cols], NEG_INF)


def _bias_tiles(rel_bias, specs, cast_weights=()):
    cast_in, cast_out, cast_shapes = _cast_slab_specs(cast_weights, N_HEADS, lambda h: h)
    outs = pl.pallas_call(
        functools.partial(_bias_tile_kernel, specs),
        grid=(N_HEADS,),
        in_specs=[pl.BlockSpec(memory_space=pltpu.SMEM)] + cast_in,
        out_specs=[pl.BlockSpec((None, r, c), lambda h: (h, 0, 0)) for r, c, _, _ in specs] + cast_out,
        out_shape=[jax.ShapeDtypeStruct((N_HEADS, r, c), F32) for r, c, _, _ in specs] + cast_shapes,
        compiler_params=_params(1),
        name="bias_tiles",
    )(rel_bias, *cast_weights)
    return outs[:len(specs)], outs[len(specs):]


IN_TILE_N = 2048
IN_WIDTH = 3 * WIDTH_A + 3 * WIDTH_B + 2 * D_MODEL
COL_K = WIDTH_A
COL_V = 2 * WIDTH_A
COL_B = 3 * WIDTH_A
COL_C = COL_B + WIDTH_B
COL_X = COL_C + WIDTH_B
COL_GA = COL_X + WIDTH_B
COL_GB = COL_GA + D_MODEL


def _in_proj_kernel(k_at, v_at, x_ref, g_ref, w_ref, cs_ref, proj_ref, kf_ref, vf_ref, xn_ref):
    j = pl.program_id(1)

    @pl.when(j == 0)
    def _():
        xn_ref[...] = _rms(x_ref[...], g_ref[...]).astype(BF16)

    y = jnp.dot(xn_ref[...], w_ref[...], preferred_element_type=F32)
    proj_ref[...] = (y * cs_ref[...]).astype(BF16)

    for (tile, off), f32_ref in ((k_at, kf_ref), (v_at, vf_ref)):
        @pl.when(j == tile)
        def _():
            f32_ref[...] = y[:, off:off + WIDTH_A]


def _in_proj(x2d, norm_g, w_in_bf, tm):
    t = x2d.shape[0]
    tn = IN_TILE_N
    assert t % tm == 0 and w_in_bf.shape == (D_MODEL, IN_WIDTH) and IN_WIDTH % tn == 0
    k_at, v_at = divmod(COL_K, tn), divmod(COL_V, tn)
    assert k_at[1] + WIDTH_A <= tn and v_at[1] + WIDTH_A <= tn
    col_scale = jnp.concatenate([jnp.full((1, WIDTH_A), D_QK ** -0.5, F32),
                                 jnp.ones((1, IN_WIDTH - WIDTH_A), F32)], axis=1)
    f32_cols = pl.BlockSpec((tm, WIDTH_A), lambda i, j: (i, 0))
    return pl.pallas_call(
        functools.partial(_in_proj_kernel, k_at, v_at),
        grid=(t // tm, IN_WIDTH // tn),
        in_specs=[
            pl.BlockSpec((tm, D_MODEL), lambda i, j: (i, 0)),
            pl.BlockSpec((1, D_MODEL), lambda i, j: (0, 0)),
            pl.BlockSpec((D_MODEL, tn), lambda i, j: (0, j)),
            pl.BlockSpec((1, tn), lambda i, j: (0, j)),
        ],
        out_specs=[pl.BlockSpec((tm, tn), lambda i, j: (i, j)), f32_cols, f32_cols],
        out_shape=[
            jax.ShapeDtypeStruct((t, IN_WIDTH), BF16),
            jax.ShapeDtypeStruct((t, WIDTH_A), F32),
            jax.ShapeDtypeStruct((t, WIDTH_A), F32),
        ],
        scratch_shapes=[pltpu.VMEM((tm, D_MODEL), BF16)],
        compiler_params=_params(2),
        name="in_proj",
    )(x2d, norm_g.reshape(1, D_MODEL), w_in_bf, col_scale)


def _scores(qm, k):
    return lax.dot_general(qm, k, (((1,), (1,)), ((), ())), preferred_element_type=F32)


def _tile_lanes(x, n):
    return x if n == 1 else jnp.concatenate([x] * n, axis=1)


def _with_ones(v):
    return jnp.concatenate([v, jnp.ones_like(v)], axis=1)


def _softmax_step(s, v_ext, state, shift=None):
    m_ref, acc_ref = state
    keys = s.shape[1]
    assert keys % LANE == 0
    row_max = jnp.max(s, axis=-1, keepdims=True)
    if shift is not None:
        row_max = row_max + shift
    m_old = m_ref[...]
    m_new = jnp.maximum(m_old, row_max)
    sub = m_new if shift is None else m_new - shift
    p = jnp.exp(s - _tile_lanes(sub, keys // LANE))
    pv = jnp.dot(p.astype(BF16), v_ext, preferred_element_type=F32)
    alpha = jnp.exp(m_old - m_new)
    acc_ref[...] = _tile_lanes(alpha, 2 * D_HEAD // LANE) * acc_ref[...] + pv
    m_ref[...] = m_new


def _split_maps(q):
    lane = lax.broadcasted_iota(jnp.int32, q.shape, 1)
    zero = jnp.zeros_like(q)
    return jnp.where(lane < D_QK, q, zero), jnp.where(lane >= D_QK, q, zero)


def _lambda(lq1_ref, lk1_ref, lq2_ref, lk2_ref, lambda_init):
    s1 = jnp.sum(lq1_ref[...] * lk1_ref[...], axis=-1, keepdims=True)
    s2 = jnp.sum(lq2_ref[...] * lk2_ref[...], axis=-1, keepdims=True)
    return jnp.exp(s1) - jnp.exp(s2) + lambda_init


def _attn_finish(state1, state2, lam, g, lambda_init):
    a1 = state1[1][...]
    a2 = state2[1][...]
    o = a1[:, :D_HEAD] / a1[:, D_HEAD:] - lam * (a2[:, :D_HEAD] / a2[:, D_HEAD:])
    return _rms(o, g) * (1.0 - lambda_init)


def _attn_prompt_kernel(tq, nq, lambda_init, n_cast, table_ref, lq1_ref, lk1_ref, lq2_ref, lk2_ref, g_ref,
                        q_ref, k_ref, v_ref, bnear_ref, bdiag_ref, *rest):
    cast_in, o_ref, cast_out = rest[:n_cast], rest[n_cast], rest[n_cast + 1:2 * n_cast + 1]
    m1_ref, a1_ref, m2_ref, a2_ref, sa1_ref, sa2_ref, sb1_ref, sb2_ref = rest[2 * n_cast + 1:]
    _cast_slabs(cast_in, cast_out)
    h = pl.program_id(1)
    far_bias = table_ref[FAR_BUCKET, h]
    lam = _lambda(lq1_ref, lk1_ref, lq2_ref, lk2_ref, lambda_init)
    st1 = (m1_ref, a1_ref)
    st2 = (m2_ref, a2_ref)
    buf_a = (sa1_ref, sa2_ref)
    buf_b = (sb1_ref, sb2_ref)

    def q_tile(qi, carry):
        q0 = pl.multiple_of(qi * tq, tq)
        q1, q2 = _split_maps(q_ref[pl.ds(q0, tq), :])

        def score_block(j, buf):
            k = k_ref[pl.ds(pl.multiple_of(j * tq, tq), tq), :]
            buf[0][...] = _scores(q1, k)
            buf[1][...] = _scores(q2, k)

        def score_next_tile():
            qn1, qn2 = _split_maps(q_ref[pl.ds(pl.multiple_of(q0 + tq, tq), tq), :])
            k = k_ref[0:tq, :]
            buf_b[0][...] = _scores(qn1, k)
            buf_b[1][...] = _scores(qn2, k)

        def step(j, buf, nxt=None, bias=None, shift=None, next_tile=False):
            if nxt is not None:
                score_block(j + 1, nxt)
            if next_tile:
                score_next_tile()
            v = _with_ones(v_ref[pl.ds(pl.multiple_of(j * tq, tq), tq), :])
            s1 = buf[0][...]
            s2 = buf[1][...]
            if bias is not None:
                s1 = s1 + bias
                s2 = s2 + bias
            _softmax_step(s1, v, st1, shift=shift)
            _softmax_step(s2, v, st2, shift=shift)

        for m_ref, a_ref in (st1, st2):
            m_ref[...] = jnp.full(m_ref.shape, NEG_INF, F32)
            a_ref[...] = jnp.zeros(a_ref.shape, F32)

        n_far = jnp.maximum(qi - 1, 0)
        odd_far = jnp.bitwise_and(n_far, 1)
        qi_even = jnp.bitwise_and(qi, 1) == 0

        @pl.when(qi_even)
        def _():
            score_block(0, buf_a)

        @pl.when(odd_far == 1)
        def _():
            step(0, buf_a, nxt=buf_b, shift=far_bias)

        def far_pair(t, c):
            j = odd_far + 2 * t
            step(j, buf_b, nxt=buf_a, shift=far_bias)
            step(j + 1, buf_a, nxt=buf_b, shift=far_bias)
            return c

        lax.fori_loop(0, lax.shift_right_logical(n_far, 1), far_pair, 0)

        @pl.when(qi > 0)
        def _():
            step(qi - 1, buf_b, nxt=buf_a, bias=bnear_ref[...])

        feeds_next = jnp.logical_and(qi_even, qi + 1 < nq)

        @pl.when(feeds_next)
        def _():
            step(qi, buf_a, bias=bdiag_ref[...], next_tile=True)

        @pl.when(jnp.logical_not(feeds_next))
        def _():
            step(qi, buf_a, bias=bdiag_ref[...])

        o = _attn_finish(st1, st2, lam, g_ref[...], lambda_init)
        o_ref[pl.ds(q0, tq), :] = o.astype(BF16)
        return carry

    lax.fori_loop(0, nq, q_tile, 0)


def _attn_prompt(proj, rel_bias, lam_vecs, subln_g, bias_near, bias_diag, nb, seq, tq, lambda_init,
                 cast_weights=()):
    assert seq % tq == 0 and tq % CHUNK == 0 and tq >= FAR_DIST
    qkv3 = proj.reshape(nb, seq, IN_WIDTH)
    vec = pl.BlockSpec((1, D_QK), lambda b, h: (0, 0))
    head_cols = lambda off: pl.BlockSpec((None, seq, D_HEAD), lambda b, h: (b, 0, off + h))
    tile = pl.BlockSpec((None, tq, tq), lambda b, h: (h, 0, 0))
    stat = pltpu.VMEM((tq, LANE), F32)
    acc = pltpu.VMEM((tq, 2 * D_HEAD), F32)
    cast_in, cast_out, cast_shapes = _cast_slab_specs(cast_weights, nb * N_HEADS, lambda b, h: b * N_HEADS + h)
    o, *cast = pl.pallas_call(
        functools.partial(_attn_prompt_kernel, tq, seq // tq, lambda_init, len(cast_weights)),
        grid=(nb, N_HEADS),
        in_specs=[pl.BlockSpec(memory_space=pltpu.SMEM), vec, vec, vec, vec,
                  pl.BlockSpec((1, D_HEAD), lambda b, h: (0, 0)),
                  head_cols(0), head_cols(COL_K // D_HEAD), head_cols(COL_V // D_HEAD), tile, tile] + cast_in,
        out_specs=[pl.BlockSpec((None, seq, D_HEAD), lambda b, h: (b, 0, h))] + cast_out,
        out_shape=[jax.ShapeDtypeStruct((nb, seq, WIDTH_A), BF16)] + cast_shapes,
        scratch_shapes=[stat, acc, stat, acc] + [pltpu.VMEM((tq, tq), F32)] * 4,
        compiler_params=_params(2),
        name="attn_prompt",
    )(rel_bias, *lam_vecs, subln_g.reshape(1, D_HEAD), qkv3, qkv3, qkv3, bias_near, bias_diag, *cast_weights)
    return o.reshape(nb * seq, WIDTH_A), cast


def _softmax_two_blocks(qm, k_a, v_a, bias_a, k_b, v_b, bias_b):
    s_a = _scores(qm, k_a) + bias_a
    s_b = _scores(qm, k_b) + bias_b
    m = jnp.maximum(jnp.max(s_a, axis=-1, keepdims=True), jnp.max(s_b, axis=-1, keepdims=True))
    p_a = jnp.exp(s_a - m).astype(BF16)
    p_b = jnp.exp(s_b - m).astype(BF16)
    return jnp.dot(p_a, v_a, preferred_element_type=F32) + jnp.dot(p_b, v_b, preferred_element_type=F32)


def _attn_sample_kernel(lambda_init, past, lq1_ref, lk1_ref, lq2_ref, lk2_ref, g_ref,
                        qkv_ref, kc_ref, vc_ref, bc_ref, bn_ref, o_ref):
    lam = _lambda(lq1_ref, lk1_ref, lq2_ref, lk2_ref, lambda_init)
    for h in range(N_HEADS):
        cols = lambda part: slice((part * N_HEADS + h) * D_HEAD, (part * N_HEADS + h + 1) * D_HEAD)
        q1, q2 = _split_maps(qkv_ref[:, cols(0)])
        kn = qkv_ref[:, cols(1)]
        vn = _with_ones(qkv_ref[:, cols(2)])
        kc = kc_ref[pl.ds(h, past, stride=N_HEADS), :].astype(BF16)
        vc = _with_ones(vc_ref[pl.ds(h, past, stride=N_HEADS), :].astype(BF16))
        a1 = _softmax_two_blocks(q1, kc, vc, bc_ref[h], kn, vn, bn_ref[h])
        a2 = _softmax_two_blocks(q2, kc, vc, bc_ref[h], kn, vn, bn_ref[h])
        o = a1[:, :D_HEAD] / a1[:, D_HEAD:] - lam * (a2[:, :D_HEAD] / a2[:, D_HEAD:])
        o_ref[:, h * D_HEAD:(h + 1) * D_HEAD] = (_rms(o, g_ref[...]) * (1.0 - lambda_init)).astype(BF16)


def _attn_sample(proj, cache_k, cache_v, layer, lam_vecs, subln_g, bias_cache, bias_new, nb, seq, past,
                 lambda_init):
    qkv3 = proj.reshape(nb, seq, IN_WIDTH)
    kc = cache_k.reshape(-1, past * N_HEADS, D_HEAD)
    vc = cache_v.reshape(-1, past * N_HEADS, D_HEAD)
    vec = pl.BlockSpec((1, D_QK), lambda b: (0, 0))
    past_rows = pl.BlockSpec((None, past * N_HEADS, D_HEAD), lambda b: (layer * nb + b, 0, 0))
    full = lambda a: pl.BlockSpec(a.shape, lambda b: (0,) * a.ndim)
    o = pl.pallas_call(
        functools.partial(_attn_sample_kernel, lambda_init, past),
        grid=(nb,),
        in_specs=[vec, vec, vec, vec, pl.BlockSpec((1, D_HEAD), lambda b: (0, 0)),
                  pl.BlockSpec((None, seq, 3 * WIDTH_A), lambda b: (b, 0, 0)), past_rows, past_rows,
                  full(bias_cache), full(bias_new)],
        out_specs=pl.BlockSpec((None, seq, WIDTH_A), lambda b: (b, 0, 0)),
        out_shape=jax.ShapeDtypeStruct((nb, seq, WIDTH_A), BF16),
        compiler_params=_params(1),
        name="attn_sample",
    )(*lam_vecs, subln_g.reshape(1, D_HEAD), qkv3, kc, vc, bias_cache, bias_new)
    return o.reshape(nb * seq, WIDTH_A)


def _causal_conv3(x3, prev8, w):
    p0 = prev8[:, SUBLANE - 2:SUBLANE - 1, :]
    p1 = prev8[:, SUBLANE - 1:SUBLANE, :]
    row = lax.broadcasted_iota(jnp.int32, x3.shape, 1)
    x1 = jnp.where(row == 0, p1, pltpu.roll(x3, 1, axis=1))
    x2 = jnp.where(row == 0, p0, jnp.where(row == 1, p1, pltpu.roll(x3, 2, axis=1)))
    return x2 * w[0:1, :] + x1 * w[1:2, :] + x3 * w[2:3, :]


def _conv_with_state(x, i, nseq, tps, st_ref, so_ref, halo):
    rows = x.shape[0] // nseq
    x3 = x.reshape(nseq, rows, x.shape[1])
    last8 = x3[:, rows - SUBLANE:, :]
    if tps == 1:
        prev8 = st_ref[...]
    else:
        @pl.when(i % tps == 0)
        def _():
            halo[...] = st_ref[...]

        prev8 = halo[...]
        halo[...] = last8
    so_ref[...] = last8
    return x3, prev8


def _merge_kernel(nseq, tps, n_cast, o_ref, b_ref, c_ref, xi_ref, ga_ref, gb_ref, x_ref, st_ref, cw_ref,
                  wa_ref, wb_ref, wo_ref, g2_ref, *rest):
    cast_in, (h_ref, hn_ref, so_ref) = rest[:n_cast], rest[n_cast:n_cast + 3]
    cast_out, scratch = rest[n_cast + 3:2 * n_cast + 3], rest[2 * n_cast + 3:]
    _cast_slabs(cast_in, cast_out)
    i = pl.program_id(0)
    u = c_ref[...].astype(F32) * xi_ref[...].astype(F32)
    u3, prev8 = _conv_with_state(u, i, nseq, tps, st_ref, so_ref, scratch[0] if tps > 1 else None)
    z = _causal_conv3(u3, prev8, cw_ref[...]).reshape(u.shape)
    ob = (b_ref[...].astype(F32) * z).astype(BF16)
    ya = jnp.dot(o_ref[...], wa_ref[...], preferred_element_type=F32)
    yb = jnp.dot(ob, wb_ref[...], preferred_element_type=F32)
    merged = jax.nn.sigmoid(ga_ref[...].astype(F32)) * ya + jax.nn.sigmoid(gb_ref[...].astype(F32)) * yb
    hres = x_ref[...] + jnp.dot(merged.astype(BF16), wo_ref[...], preferred_element_type=F32)
    h_ref[...] = hres
    hn_ref[...] = _rms(hres, g2_ref[...]).astype(BF16)


def _merge(o, proj, x2d, state8, conv_w, wa, wb, wo, norm2_g, nb, seq, rows, cast_weights=()):
    t = x2d.shape[0]
    nseq = max(rows // seq, 1)
    tps = max(seq // rows, 1)
    assert t % rows == 0 and (rows % seq == 0 or seq % rows == 0)
    tok = lambda width, col: pl.BlockSpec((rows, width), lambda i: (i, col))
    full = lambda a: pl.BlockSpec(a.shape, lambda i: (0,) * a.ndim)
    state_spec = pl.BlockSpec((nseq, SUBLANE, WIDTH_B), lambda i: (i // tps, 0, 0))
    scratch = [pltpu.VMEM((nseq, SUBLANE, WIDTH_B), F32)] if tps > 1 else []
    g2 = norm2_g.reshape(1, D_MODEL)
    cast_in, cast_out, cast_shapes = _cast_slab_specs(cast_weights, t // rows, lambda i: i)
    hres, hn, state_rows, *cast = pl.pallas_call(
        functools.partial(_merge_kernel, nseq, tps, len(cast_weights)),
        grid=(t // rows,),
        in_specs=[tok(WIDTH_A, 0), tok(WIDTH_B, COL_B // WIDTH_B), tok(WIDTH_B, COL_C // WIDTH_B),
                  tok(WIDTH_B, COL_X // WIDTH_B), tok(D_MODEL, COL_GA // D_MODEL), tok(D_MODEL, COL_GB // D_MODEL),
                  tok(D_MODEL, 0), state_spec,
                  full(conv_w), full(wa), full(wb), full(wo), full(g2)] + cast_in,
        out_specs=[tok(D_MODEL, 0), tok(D_MODEL, 0), state_spec] + cast_out,
        out_shape=[jax.ShapeDtypeStruct((t, D_MODEL), F32),
                   jax.ShapeDtypeStruct((t, D_MODEL), BF16),
                   jax.ShapeDtypeStruct((nb, SUBLANE, WIDTH_B), F32)] + cast_shapes,
        scratch_shapes=scratch,
        compiler_params=_params(1),
        name="merge",
    )(o, proj, proj, proj, proj, proj, x2d, state8, conv_w, wa, wb, wo, g2, *cast_weights)
    return hres, hn, state_rows, cast


FFN_TILE_F = 512
FFN_SUB_TILES = 2


def _ffn_kernel(nseq, tps, nf, hn_ref, h_ref, wg_ref, wv_ref, cwg_ref, cwv_ref, stg_ref, stv_ref,
                wd_ref, fg_ref, y_ref, sog_ref, sov_ref, *scratch):
    i = pl.program_id(0)
    f = pl.program_id(1)
    rows = hn_ref.shape[0]
    sub = rows // FFN_SUB_TILES
    branches = ((wg_ref, cwg_ref, stg_ref, sog_ref), (wv_ref, cwv_ref, stv_ref, sov_ref))

    @pl.when(f == 0)
    def _():
        y_ref[...] = jnp.zeros_like(y_ref)

    if tps > 1:
        halos = [scratch[0].at[2 * f + slot] for slot in range(2)]

        @pl.when(i % tps == 0)
        def _():
            for halo, (_, _, st_ref, _) in zip(halos, branches):
                halo[...] = st_ref[...]

        prev = [halo[...] for halo in halos]
    else:
        assert sub % (rows // nseq) == 0
        seqs = nseq // FFN_SUB_TILES

    for r in range(FFN_SUB_TILES):
        tile_rows = slice(r * sub, (r + 1) * sub)
        hn = hn_ref[tile_rows, :]
        convs = []
        for slot, (w_ref, cw_ref, st_ref, so_ref) in enumerate(branches):
            up = jnp.dot(hn, w_ref[...], preferred_element_type=F32)
            if tps > 1:
                up3 = up.reshape(1, sub, up.shape[1])
                prev8 = prev[slot]
                prev[slot] = up3[:, sub - SUBLANE:, :]
            else:
                up3 = up.reshape(seqs, sub // seqs, up.shape[1])
                prev8 = st_ref[r * seqs:(r + 1) * seqs]
                so_ref[r * seqs:(r + 1) * seqs] = up3[:, sub // seqs - SUBLANE:, :]
            convs.append(_causal_conv3(up3, prev8, cw_ref[...]).reshape(up.shape))
        act = (jax.nn.silu(convs[0]) * convs[1]).astype(BF16)
        part = jnp.dot(act, wd_ref[...], preferred_element_type=F32)
        y_ref[tile_rows, :] += part

    if tps > 1:
        for halo, last8, (_, _, _, so_ref) in zip(halos, prev, branches):
            halo[...] = last8
            so_ref[...] = last8

    @pl.when(f == nf - 1)
    def _():
        y_ref[...] = _rms(h_ref[...] + y_ref[...], fg_ref[...])


def _ffn(hn, hres, w_up, ffn_conv_w, state8, w_down, final_g, nb, seq, rows):
    t = hn.shape[0]
    tf = FFN_TILE_F
    nf = D_FF // tf
    nseq = max(rows // seq, 1)
    tps = max(seq // rows, 1)
    assert D_FF % tf == 0 and t % rows == 0 and (rows % seq == 0 or seq % rows == 0)
    tok = pl.BlockSpec((rows, D_MODEL), lambda i, f: (i, 0))
    gate_cols = lambda r: pl.BlockSpec((r, tf), lambda i, f: (0, f))
    val_cols = lambda r: pl.BlockSpec((r, tf), lambda i, f: (0, nf + f))
    st_gate = pl.BlockSpec((nseq, SUBLANE, tf), lambda i, f: (i // tps, 0, f))
    st_val = pl.BlockSpec((nseq, SUBLANE, tf), lambda i, f: (i // tps, 0, nf + f))
    resid = pl.BlockSpec((rows, D_MODEL), lambda i, f: (i, 0), pipeline_mode=pl.Buffered(1))
    scratch = [pltpu.VMEM((2 * nf, nseq, SUBLANE, tf), F32)] if tps > 1 else []
    tile_rows = pl.BlockSpec((nseq, SUBLANE, tf), lambda i, f: (i, 0, f))
    fg = final_g.reshape(1, D_MODEL)
    y, tail_g, tail_v = pl.pallas_call(
        functools.partial(_ffn_kernel, nseq, tps, nf),
        grid=(t // rows, nf),
        in_specs=[tok, resid, gate_cols(D_MODEL), val_cols(D_MODEL), gate_cols(CONV_W), val_cols(CONV_W),
                  st_gate, st_val,
                  pl.BlockSpec((tf, D_MODEL), lambda i, f: (f, 0)),
                  pl.BlockSpec((1, D_MODEL), lambda i, f: (0, 0))],
        out_specs=[tok, tile_rows, tile_rows],
        out_shape=[jax.ShapeDtypeStruct((t, D_MODEL), F32),
                   jax.ShapeDtypeStruct((nb * tps, SUBLANE, D_FF), F32),
                   jax.ShapeDtypeStruct((nb * tps, SUBLANE, D_FF), F32)],
        scratch_shapes=scratch,
        compiler_params=_params(2),
        name="ffn",
    )(hn, hres, w_up, w_up, ffn_conv_w, ffn_conv_w, state8, state8, w_down, fg)
    return y, tail_g[tps - 1::tps], tail_v[tps - 1::tps]


def _state8(state):
    return jnp.pad(state, ((0, 0), (SUBLANE - (CONV_W - 1), 0), (0, 0)))


def _state_out(state8):
    return state8[:, SUBLANE - (CONV_W - 1):, :]


def _layer(x2d, nb, seq, attn_fn, conv8, ffn8, lp, tiles, cast_in_attn=(), cast_in_merge=()):
    lp = dict(lp)
    proj, k_f32, v_f32 = _in_proj(x2d, lp["norm1_g"], lp["w_in"], tiles["in"])
    o, done = attn_fn(proj, [lp[name] for name in cast_in_attn])
    lp.update(zip(cast_in_attn, done))
    hres, hn, conv_out8, done = _merge(o, proj, x2d, conv8, lp["conv_w"], lp["w_proj_a"], lp["w_proj_b"],
                                       lp["w_o"], lp["norm2_g"], nb, seq, tiles["merge"],
                                       [lp[name] for name in cast_in_merge])
    lp.update(zip(cast_in_merge, done))
    y, ffn_g8, ffn_v8 = _ffn(hn, hres, lp["w_up"], lp["ffn_conv_w"], ffn8, lp["w_down"], lp["final_g"],
                             nb, seq, tiles["ffn"])
    k = k_f32.reshape(nb, seq, N_HEADS, D_HEAD)
    v = v_f32.reshape(nb, seq, N_HEADS, D_HEAD)
    conv_state = _state_out(conv_out8)
    ffn_state = jnp.concatenate([_state_out(ffn_g8), _state_out(ffn_v8)], axis=-1)
    return (y, k, v, conv_state, ffn_state), lp


PROMPT_TILES = {"in": 512, "merge": 256, "ffn": 1024}
SAMPLE_TILES = {"in": 512, "merge": 256, "ffn": 1024}
ATTN_TILE_Q = 512


def kernel(x_prompt, x_sample, cache_k, cache_v, state_conv_mix, state_conv_ffn, rel_bias, norm1_g, w_in,
           lambda_q1, lambda_k1, lambda_q2, lambda_k2, subln_g, conv_w, w_proj_a, w_proj_b, w_o, norm2_g,
           w_up, ffn_conv_w, w_down, final_g):
    depth = w_in.shape[0]
    nbp, seq_p, _ = x_prompt.shape
    nbs, seq_s, _ = x_sample.shape
    past = cache_k.shape[2]
    tq = ATTN_TILE_Q

    assert depth == 1, "the ffn kernel ends with the final RMSNorm, so it is written for a single layer"
    (bias_near, bias_diag, bias_cache, bias_new), (w_in_bf,) = _bias_tiles(
        rel_bias,
        ((tq, tq, tq, 0), (tq, tq, tq, tq), (seq_s, past, past, 0), (seq_s, seq_s, past, past)),
        cast_weights=(w_in[0],))

    hp = x_prompt.reshape(nbp * seq_p, D_MODEL)
    hs = x_sample.reshape(nbs * seq_s, D_MODEL)
    outs_p, outs_s = [], []
    for d in range(depth):
        lp = {
            "norm1_g": norm1_g[d], "w_in": w_in_bf, "conv_w": conv_w[d],
            "w_proj_a": w_proj_a[d], "w_proj_b": w_proj_b[d], "w_o": w_o[d], "norm2_g": norm2_g[d],
            "w_up": w_up[d], "ffn_conv_w": ffn_conv_w[d], "w_down": w_down[d], "final_g": final_g,
        }
        li = _lambda_init(d)
        lam_vecs = tuple(a[d].reshape(1, D_QK) for a in (lambda_q1, lambda_k1, lambda_q2, lambda_k2))

        attn_p = lambda qkv, cast: _attn_prompt(qkv, rel_bias, lam_vecs, subln_g[d], bias_near, bias_diag,
                                                nbp, seq_p, tq, li, cast)
        zeros_mix = jnp.zeros((nbp, SUBLANE, WIDTH_B), F32)
        zeros_ffn = jnp.zeros((nbp, SUBLANE, 2 * D_FF), F32)
        lp["w_in_late"] = w_in[d]
        (hp, kp, vp, cmp_, cfp), lp = _layer(
            hp, nbp, seq_p, attn_p, zeros_mix, zeros_ffn, lp, PROMPT_TILES,
            cast_in_attn=("w_proj_a", "w_proj_b", "w_o", "w_down", "w_in_late"), cast_in_merge=("w_up",))

        attn_s = lambda qkv, cast: (_attn_sample(qkv, cache_k, cache_v, d, lam_vecs, subln_g[d], bias_cache,
                                                 bias_new, nbs, seq_s, past, li), ())
        (hs, ks, vs, cms, cfs), _ = _layer(hs, nbs, seq_s, attn_s, _state8(state_conv_mix[d]),
                                           _state8(state_conv_ffn[d]), {**lp, "w_in": lp["w_in_late"]},
                                           SAMPLE_TILES)
        outs_p.append((kp, vp, cmp_, cfp))
        outs_s.append((ks, vs, cms, cfs))

    stack = lambda outs, idx: jnp.stack([o[idx] for o in outs])
    return (hp.reshape(nbp, seq_p, D_MODEL), hs.reshape(nbs, seq_s, D_MODEL),
            stack(outs_p, 0), stack(outs_p, 1), stack(outs_p, 2), stack(outs_p, 3),
            stack(outs_s, 0), stack(outs_s, 1), stack(outs_s, 2), stack(outs_s, 3))
```

```python
import functools
import math

import jax
import jax.numpy as jnp
from jax import lax
from jax.experimental import pallas as pl
from jax.experimental.pallas import tpu as pltpu

D_MODEL = 2048
N_HEADS = 8
D_QK = 64
D_HEAD = 2 * D_QK
WIDTH_A = N_HEADS * D_HEAD
WIDTH_B = D_MODEL // 2
D_FF = 5632
CONV_W = 3
CHUNK = 64
N_BUCKETS = 32
MAX_DIST = 128
EPS = 1e-6
NEG_INF = -1e30

SUBLANE = 8
LANE = 128
VMEM_LIMIT_BYTES = 56 * 1024 * 1024

F32 = jnp.float32
BF16 = jnp.bfloat16


def _bucket_thresholds():
    nb = N_BUCKETS // 2
    max_exact = nb // 2
    steps = nb - max_exact
    out = []
    for k in range(1, steps):
        n = max_exact
        while n ** steps * max_exact ** k < MAX_DIST ** k * max_exact ** steps:
            n += 1
        out.append(n)
    return tuple(out)


_BUCKET_THRESHOLDS = _bucket_thresholds()
FAR_DIST = _BUCKET_THRESHOLDS[-1]
FAR_BUCKET = N_BUCKETS // 2 - 1
_CHUNK_SHIFT = CHUNK.bit_length() - 1
assert 1 << _CHUNK_SHIFT == CHUNK


def _bucket_of(rel):
    nb = N_BUCKETS // 2
    max_exact = nb // 2
    n = abs(rel)
    large = max_exact + sum(n >= thr for thr in _BUCKET_THRESHOLDS)
    return (nb if rel > 0 else 0) + (n if n < max_exact else large)


def _lambda_init(layer_idx):
    return 0.8 - 0.6 * math.exp(-0.3 * layer_idx)


def _rms(x, g):
    return x * lax.rsqrt(jnp.mean(x * x, axis=-1, keepdims=True) + EPS) * g


def _params(n_axes):
    return pltpu.CompilerParams(
        dimension_semantics=("arbitrary",) * n_axes, vmem_limit_bytes=VMEM_LIMIT_BYTES)


def _cast_slabs(in_refs, out_refs):
    for i_ref, o_ref in zip(in_refs, out_refs):
        o_ref[...] = i_ref[...].astype(BF16)


def _cast_slab_specs(weights, n_steps, step_of):
    specs, shapes = [], []
    for w in weights:
        rows, cols = w.shape
        assert rows % n_steps == 0 and (rows // n_steps) % (2 * SUBLANE) == 0 and cols % LANE == 0
        specs.append(pl.BlockSpec((rows // n_steps, cols), lambda *idx: (step_of(*idx), 0)))
        shapes.append(jax.ShapeDtypeStruct(w.shape, BF16))
    return specs, list(specs), shapes


def _bias_tile_kernel(specs, table_ref, *refs):
    n_cast = (len(refs) - len(specs)) // 2
    cast_in, out_refs, cast_out = refs[:n_cast], refs[n_cast:n_cast + len(specs)], refs[n_cast + len(specs):]
    _cast_slabs(cast_in, cast_out)
    h = pl.program_id(0)
    nb = N_BUCKETS // 2
    max_exact = nb // 2
    for (rows, cols, q0, k0), out_ref in zip(specs, out_refs):
        rel0 = k0 - q0 - (rows - SUBLANE)
        width = -(-(cols + rows - SUBLANE) // LANE) * LANE
        rel = (lax.broadcasted_iota(jnp.int32, (SUBLANE, width), 1)
               - lax.broadcasted_iota(jnp.int32, (SUBLANE, width), 0) + rel0)
        n = jnp.abs(rel)
        large = jnp.full((SUBLANE, width), max_exact, jnp.int32)
        for thr in _BUCKET_THRESHOLDS:
            large = large + jnp.where(n >= thr, 1, 0)
        bucket = jnp.where(rel > 0, nb, 0) + jnp.where(n < max_exact, n, large)
        strip = jnp.zeros((SUBLANE, width), F32)
        rel_hi = min(rel0 + width - 1, CHUNK - 1)
        for b in sorted({_bucket_of(r) for r in range(rel0 - SUBLANE + 1, rel_hi + 1)}):
            strip = jnp.where(bucket == b, table_ref[b, h], strip)
        assert q0 % SUBLANE == 0 and CHUNK % SUBLANE == 0
        k_chunk = lax.shift_right_logical(k0 + lax.broadcasted_iota(jnp.int32, (SUBLANE, cols), 1), _CHUNK_SHIFT)
        for a in range(rows // SUBLANE):
            start = rows - SUBLANE * (a + 1)
            visible = k_chunk <= (q0 + SUBLANE * a) // CHUNK
            out_ref[SUBLANE * a:SUBLANE * (a + 1), :] = jnp.where(visible, strip[:, start:start + cols], NEG_INF)


def _bias_tiles(rel_bias, specs, cast_weights=()):
    cast_in, cast_out, cast_shapes = _cast_slab_specs(cast_weights, N_HEADS, lambda h: h)
    outs = pl.pallas_call(
        functools.partial(_bias_tile_kernel, specs),
        grid=(N_HEADS,),
        in_specs=[pl.BlockSpec(memory_space=pltpu.SMEM)] + cast_in,
        out_specs=[pl.BlockSpec((None, r, c), lambda h: (h, 0, 0)) for r, c, _, _ in specs] + cast_out,
        out_shape=[jax.ShapeDtypeStruct((N_HEADS, r, c), F32) for r, c, _, _ in specs] + cast_shapes,
        compiler_params=_params(1),
        name="bias_tiles",
    )(rel_bias, *cast_weights)
    return outs[:len(specs)], outs[len(specs):]


IN_TILE_N = 2048
IN_WIDTH = 3 * WIDTH_A + 3 * WIDTH_B + 2 * D_MODEL
COL_K = WIDTH_A
COL_V = 2 * WIDTH_A
COL_B = 3 * WIDTH_A
COL_C = COL_B + WIDTH_B
COL_X = COL_C + WIDTH_B
COL_GA = COL_X + WIDTH_B
COL_GB = COL_GA + D_MODEL


def _in_proj_kernel(k_at, v_at, x_ref, g_ref, w_ref, cs_ref, proj_ref, kf_ref, vf_ref, xn_ref):
    j = pl.program_id(1)

    @pl.when(j == 0)
    def _():
        xn_ref[...] = _rms(x_ref[...], g_ref[...]).astype(BF16)

    y = jnp.dot(xn_ref[...], w_ref[...], preferred_element_type=F32)
    proj_ref[...] = (y * cs_ref[...]).astype(BF16)

    for (tile, off), f32_ref in ((k_at, kf_ref), (v_at, vf_ref)):
        @pl.when(j == tile)
        def _():
            for h in range(N_HEADS):
                f32_ref[pl.ds(h, y.shape[0], stride=N_HEADS), :] = y[:, off + h * D_HEAD:off + (h + 1) * D_HEAD]


def _in_proj(x2d, norm_g, w_in_bf, tm):
    t = x2d.shape[0]
    tn = IN_TILE_N
    assert t % tm == 0 and w_in_bf.shape == (D_MODEL, IN_WIDTH) and IN_WIDTH % tn == 0
    k_at, v_at = divmod(COL_K, tn), divmod(COL_V, tn)
    assert k_at[1] + WIDTH_A <= tn and v_at[1] + WIDTH_A <= tn
    col_scale = jnp.concatenate([jnp.full((1, WIDTH_A), D_QK ** -0.5, F32),
                                 jnp.ones((1, IN_WIDTH - WIDTH_A), F32)], axis=1)
    f32_cols = pl.BlockSpec((tm * N_HEADS, D_HEAD), lambda i, j: (i, 0))
    return pl.pallas_call(
        functools.partial(_in_proj_kernel, k_at, v_at),
        grid=(t // tm, IN_WIDTH // tn),
        in_specs=[
            pl.BlockSpec((tm, D_MODEL), lambda i, j: (i, 0)),
            pl.BlockSpec((1, D_MODEL), lambda i, j: (0, 0)),
            pl.BlockSpec((D_MODEL, tn), lambda i, j: (0, j)),
            pl.BlockSpec((1, tn), lambda i, j: (0, j)),
        ],
        out_specs=[pl.BlockSpec((tm, tn), lambda i, j: (i, j)), f32_cols, f32_cols],
        out_shape=[
            jax.ShapeDtypeStruct((t, IN_WIDTH), BF16),
            jax.ShapeDtypeStruct((t * N_HEADS, D_HEAD), F32),
            jax.ShapeDtypeStruct((t * N_HEADS, D_HEAD), F32),
        ],
        scratch_shapes=[pltpu.VMEM((tm, D_MODEL), BF16)],
        compiler_params=_params(2),
        name="in_proj",
    )(x2d, norm_g.reshape(1, D_MODEL), w_in_bf, col_scale)


def _scores(qm, k):
    return lax.dot_general(qm, k, (((1,), (1,)), ((), ())), preferred_element_type=F32)


def _tile_lanes(x, n):
    return x if n == 1 else jnp.concatenate([x] * n, axis=1)


def _with_ones(v):
    return jnp.concatenate([v, jnp.ones_like(v)], axis=1)


def _softmax_step(s, v_ext, state, shift=None):
    m_ref, acc_ref = state
    keys = s.shape[1]
    assert keys % LANE == 0
    row_max = jnp.max(s, axis=-1, keepdims=True)
    if shift is not None:
        row_max = row_max + shift
    m_old = m_ref[...]
    m_new = jnp.maximum(m_old, row_max)
    sub = m_new if shift is None else m_new - shift
    p = jnp.exp(s - _tile_lanes(sub, keys // LANE))
    pv = jnp.dot(p.astype(BF16), v_ext, preferred_element_type=F32)
    alpha = jnp.exp(m_old - m_new)
    acc_ref[...] = _tile_lanes(alpha, 2 * D_HEAD // LANE) * acc_ref[...] + pv
    m_ref[...] = m_new


def _split_maps(q):
    lane = lax.broadcasted_iota(jnp.int32, q.shape, 1)
    zero = jnp.zeros_like(q)
    return jnp.where(lane < D_QK, q, zero), jnp.where(lane >= D_QK, q, zero)


def _lambda(lq1_ref, lk1_ref, lq2_ref, lk2_ref, lambda_init):
    s1 = jnp.sum(lq1_ref[...] * lk1_ref[...], axis=-1, keepdims=True)
    s2 = jnp.sum(lq2_ref[...] * lk2_ref[...], axis=-1, keepdims=True)
    return jnp.exp(s1) - jnp.exp(s2) + lambda_init


def _attn_finish(state1, state2, lam, g, lambda_init):
    a1 = state1[1][...]
    a2 = state2[1][...]
    o = a1[:, :D_HEAD] / a1[:, D_HEAD:] - lam * (a2[:, :D_HEAD] / a2[:, D_HEAD:])
    return _rms(o, g) * (1.0 - lambda_init)


def _attn_prompt_kernel(tq, nq, lambda_init, n_cast, table_ref, lq1_ref, lk1_ref, lq2_ref, lk2_ref, g_ref,
                        q_ref, k_ref, v_ref, bnear_ref, bdiag_ref, *rest):
    cast_in, o_ref, cast_out = rest[:n_cast], rest[n_cast], rest[n_cast + 1:2 * n_cast + 1]
    m1_ref, a1_ref, m2_ref, a2_ref, sa1_ref, sa2_ref, sb1_ref, sb2_ref = rest[2 * n_cast + 1:]
    _cast_slabs(cast_in, cast_out)
    h = pl.program_id(1)
    far_bias = table_ref[FAR_BUCKET, h]
    lam = _lambda(lq1_ref, lk1_ref, lq2_ref, lk2_ref, lambda_init)
    st1 = (m1_ref, a1_ref)
    st2 = (m2_ref, a2_ref)
    buf_a = (sa1_ref, sa2_ref)
    buf_b = (sb1_ref, sb2_ref)

    def q_tile(qi, carry):
        q0 = pl.multiple_of(qi * tq, tq)
        q1, q2 = _split_maps(q_ref[pl.ds(q0, tq), :])

        def score_block(j, buf):
            k = k_ref[pl.ds(pl.multiple_of(j * tq, tq), tq), :]
            buf[0][...] = _scores(q1, k)
            buf[1][...] = _scores(q2, k)

        def score_next_tile():
            qn1, qn2 = _split_maps(q_ref[pl.ds(pl.multiple_of(q0 + tq, tq), tq), :])
            k = k_ref[0:tq, :]
            buf_b[0][...] = _scores(qn1, k)
            buf_b[1][...] = _scores(qn2, k)

        def step(j, buf, nxt=None, bias=None, shift=None, next_tile=False):
            if nxt is not None:
                score_block(j + 1, nxt)
            if next_tile:
                score_next_tile()
            v = _with_ones(v_ref[pl.ds(pl.multiple_of(j * tq, tq), tq), :])
            s1 = buf[0][...]
            s2 = buf[1][...]
            if bias is not None:
                s1 = s1 + bias
                s2 = s2 + bias
            _softmax_step(s1, v, st1, shift=shift)
            _softmax_step(s2, v, st2, shift=shift)

        for m_ref, a_ref in (st1, st2):
            m_ref[...] = jnp.full(m_ref.shape, NEG_INF, F32)
            a_ref[...] = jnp.zeros(a_ref.shape, F32)

        n_far = jnp.maximum(qi - 1, 0)
        odd_far = jnp.bitwise_and(n_far, 1)
        qi_even = jnp.bitwise_and(qi, 1) == 0

        @pl.when(qi_even)
        def _():
            score_block(0, buf_a)

        @pl.when(odd_far == 1)
        def _():
            step(0, buf_a, nxt=buf_b, shift=far_bias)

        def far_pair(t, c):
            j = odd_far + 2 * t
            step(j, buf_b, nxt=buf_a, shift=far_bias)
            step(j + 1, buf_a, nxt=buf_b, shift=far_bias)
            return c

        lax.fori_loop(0, lax.shift_right_logical(n_far, 1), far_pair, 0)

        @pl.when(qi > 0)
        def _():
            step(qi - 1, buf_b, nxt=buf_a, bias=bnear_ref[...])

        feeds_next = jnp.logical_and(qi_even, qi + 1 < nq)

        @pl.when(feeds_next)
        def _():
            step(qi, buf_a, bias=bdiag_ref[...], next_tile=True)

        @pl.when(jnp.logical_not(feeds_next))
        def _():
            step(qi, buf_a, bias=bdiag_ref[...])

        o = _attn_finish(st1, st2, lam, g_ref[...], lambda_init)
        o_ref[pl.ds(q0, tq), :] = o.astype(BF16)
        return carry

    lax.fori_loop(0, nq, q_tile, 0)


def _attn_prompt(proj, rel_bias, lam_vecs, subln_g, bias_near, bias_diag, nb, seq, tq, lambda_init,
                 cast_weights=()):
    assert seq % tq == 0 and tq % CHUNK == 0 and tq >= FAR_DIST
    qkv3 = proj.reshape(nb, seq, IN_WIDTH)
    vec = pl.BlockSpec((1, D_QK), lambda b, h: (0, 0))
    head_cols = lambda off: pl.BlockSpec((None, seq, D_HEAD), lambda b, h: (b, 0, off + h))
    tile = pl.BlockSpec((None, tq, tq), lambda b, h: (h, 0, 0))
    stat = pltpu.VMEM((tq, LANE), F32)
    acc = pltpu.VMEM((tq, 2 * D_HEAD), F32)
    cast_in, cast_out, cast_shapes = _cast_slab_specs(cast_weights, nb * N_HEADS, lambda b, h: b * N_HEADS + h)
    o, *cast = pl.pallas_call(
        functools.partial(_attn_prompt_kernel, tq, seq // tq, lambda_init, len(cast_weights)),
        grid=(nb, N_HEADS),
        in_specs=[pl.BlockSpec(memory_space=pltpu.SMEM), vec, vec, vec, vec,
                  pl.BlockSpec((1, D_HEAD), lambda b, h: (0, 0)),
                  head_cols(0), head_cols(COL_K // D_HEAD), head_cols(COL_V // D_HEAD), tile, tile] + cast_in,
        out_specs=[pl.BlockSpec((None, seq, D_HEAD), lambda b, h: (b, 0, h))] + cast_out,
        out_shape=[jax.ShapeDtypeStruct((nb, seq, WIDTH_A), BF16)] + cast_shapes,
        scratch_shapes=[stat, acc, stat, acc] + [pltpu.VMEM((tq, tq), F32)] * 4,
        compiler_params=_params(2),
        name="attn_prompt",
    )(rel_bias, *lam_vecs, subln_g.reshape(1, D_HEAD), qkv3, qkv3, qkv3, bias_near, bias_diag, *cast_weights)
    return o.reshape(nb * seq, WIDTH_A), cast


def _softmax_two_blocks(qm, k_a, v_a, bias_a, k_b, v_b, bias_b):
    s_a = _scores(qm, k_a) + bias_a
    s_b = _scores(qm, k_b) + bias_b
    m = jnp.maximum(jnp.max(s_a, axis=-1, keepdims=True), jnp.max(s_b, axis=-1, keepdims=True))
    p_a = jnp.exp(s_a - m).astype(BF16)
    p_b = jnp.exp(s_b - m).astype(BF16)
    return jnp.dot(p_a, v_a, preferred_element_type=F32) + jnp.dot(p_b, v_b, preferred_element_type=F32)


def _attn_sample_kernel(lambda_init, past, lq1_ref, lk1_ref, lq2_ref, lk2_ref, g_ref,
                        qkv_ref, kc_ref, vc_ref, bc_ref, bn_ref, o_ref):
    lam = _lambda(lq1_ref, lk1_ref, lq2_ref, lk2_ref, lambda_init)
    for h in range(N_HEADS):
        cols = lambda part: slice((part * N_HEADS + h) * D_HEAD, (part * N_HEADS + h + 1) * D_HEAD)
        q1, q2 = _split_maps(qkv_ref[:, cols(0)])
        kn = qkv_ref[:, cols(1)]
        vn = _with_ones(qkv_ref[:, cols(2)])
        kc = kc_ref[pl.ds(h, past, stride=N_HEADS), :].astype(BF16)
        vc = _with_ones(vc_ref[pl.ds(h, past, stride=N_HEADS), :].astype(BF16))
        a1 = _softmax_two_blocks(q1, kc, vc, bc_ref[h], kn, vn, bn_ref[h])
        a2 = _softmax_two_blocks(q2, kc, vc, bc_ref[h], kn, vn, bn_ref[h])
        o = a1[:, :D_HEAD] / a1[:, D_HEAD:] - lam * (a2[:, :D_HEAD] / a2[:, D_HEAD:])
        o_ref[:, h * D_HEAD:(h + 1) * D_HEAD] = (_rms(o, g_ref[...]) * (1.0 - lambda_init)).astype(BF16)


def _attn_sample(proj, cache_k, cache_v, layer, lam_vecs, subln_g, bias_cache, bias_new, nb, seq, past,
                 lambda_init):
    qkv3 = proj.reshape(nb, seq, IN_WIDTH)
    kc = cache_k.reshape(-1, past * N_HEADS, D_HEAD)
    vc = cache_v.reshape(-1, past * N_HEADS, D_HEAD)
    vec = pl.BlockSpec((1, D_QK), lambda b: (0, 0))
    past_rows = pl.BlockSpec((None, past * N_HEADS, D_HEAD), lambda b: (layer * nb + b, 0, 0))
    full = lambda a: pl.BlockSpec(a.shape, lambda b: (0,) * a.ndim)
    o = pl.pallas_call(
        functools.partial(_attn_sample_kernel, lambda_init, past),
        grid=(nb,),
        in_specs=[vec, vec, vec, vec, pl.BlockSpec((1, D_HEAD), lambda b: (0, 0)),
                  pl.BlockSpec((None, seq, 3 * WIDTH_A), lambda b: (b, 0, 0)), past_rows, past_rows,
                  full(bias_cache), full(bias_new)],
        out_specs=pl.BlockSpec((None, seq, WIDTH_A), lambda b: (b, 0, 0)),
        out_shape=jax.ShapeDtypeStruct((nb, seq, WIDTH_A), BF16),
        compiler_params=_params(1),
        name="attn_sample",
    )(*lam_vecs, subln_g.reshape(1, D_HEAD), qkv3, kc, vc, bias_cache, bias_new)
    return o.reshape(nb * seq, WIDTH_A)


def _causal_conv3(x3, prev8, w):
    p0 = prev8[:, SUBLANE - 2:SUBLANE - 1, :]
    p1 = prev8[:, SUBLANE - 1:SUBLANE, :]
    row = lax.broadcasted_iota(jnp.int32, x3.shape, 1)
    x1 = jnp.where(row == 0, p1, pltpu.roll(x3, 1, axis=1))
    x2 = jnp.where(row == 0, p0, jnp.where(row == 1, p1, pltpu.roll(x3, 2, axis=1)))
    return x2 * w[0:1, :] + x1 * w[1:2, :] + x3 * w[2:3, :]


def _conv_with_state(x, i, nseq, tps, st_ref, so_ref, halo):
    rows = x.shape[0] // nseq
    x3 = x.reshape(nseq, rows, x.shape[1])
    last8 = x3[:, rows - SUBLANE:, :]
    if tps == 1:
        prev8 = st_ref[...]
    else:
        @pl.when(i % tps == 0)
        def _():
            halo[...] = st_ref[...]

        prev8 = halo[...]
        halo[...] = last8
    so_ref[...] = last8
    return x3, prev8


def _merge_kernel(nseq, tps, n_cast, o_ref, b_ref, c_ref, xi_ref, ga_ref, gb_ref, x_ref, st_ref, cw_ref,
                  wa_ref, wb_ref, wo_ref, g2_ref, *rest):
    cast_in, (h_ref, hn_ref, so_ref) = rest[:n_cast], rest[n_cast:n_cast + 3]
    cast_out, scratch = rest[n_cast + 3:2 * n_cast + 3], rest[2 * n_cast + 3:]
    _cast_slabs(cast_in, cast_out)
    i = pl.program_id(0)
    u = c_ref[...].astype(F32) * xi_ref[...].astype(F32)
    u3, prev8 = _conv_with_state(u, i, nseq, tps, st_ref, so_ref, scratch[0] if tps > 1 else None)
    z = _causal_conv3(u3, prev8, cw_ref[...]).reshape(u.shape)
    ob = (b_ref[...].astype(F32) * z).astype(BF16)
    ya = jnp.dot(o_ref[...], wa_ref[...], preferred_element_type=F32)
    yb = jnp.dot(ob, wb_ref[...], preferred_element_type=F32)
    merged = jax.nn.sigmoid(ga_ref[...].astype(F32)) * ya + jax.nn.sigmoid(gb_ref[...].astype(F32)) * yb
    hres = x_ref[...] + jnp.dot(merged.astype(BF16), wo_ref[...], preferred_element_type=F32)
    h_ref[...] = hres
    hn_ref[...] = _rms(hres, g2_ref[...]).astype(BF16)


def _merge(o, proj, x2d, state8, conv_w, wa, wb, wo, norm2_g, nb, seq, rows, cast_weights=()):
    t = x2d.shape[0]
    nseq = max(rows // seq, 1)
    tps = max(seq // rows, 1)
    assert t % rows == 0 and (rows % seq == 0 or seq % rows == 0)
    tok = lambda width, col: pl.BlockSpec((rows, width), lambda i: (i, col))
    full = lambda a: pl.BlockSpec(a.shape, lambda i: (0,) * a.ndim)
    state_spec = pl.BlockSpec((nseq, SUBLANE, WIDTH_B), lambda i: (i // tps, 0, 0))
    scratch = [pltpu.VMEM((nseq, SUBLANE, WIDTH_B), F32)] if tps > 1 else []
    g2 = norm2_g.reshape(1, D_MODEL)
    cast_in, cast_out, cast_shapes = _cast_slab_specs(cast_weights, t // rows, lambda i: i)
    hres, hn, state_rows, *cast = pl.pallas_call(
        functools.partial(_merge_kernel, nseq, tps, len(cast_weights)),
        grid=(t // rows,),
        in_specs=[tok(WIDTH_A, 0), tok(WIDTH_B, COL_B // WIDTH_B), tok(WIDTH_B, COL_C // WIDTH_B),
                  tok(WIDTH_B, COL_X // WIDTH_B), tok(D_MODEL, COL_GA // D_MODEL), tok(D_MODEL, COL_GB // D_MODEL),
                  tok(D_MODEL, 0), state_spec,
                  full(conv_w), full(wa), full(wb), full(wo), full(g2)] + cast_in,
        out_specs=[tok(D_MODEL, 0), tok(D_MODEL, 0), state_spec] + cast_out,
        out_shape=[jax.ShapeDtypeStruct((t, D_MODEL), F32),
                   jax.ShapeDtypeStruct((t, D_MODEL), BF16),
                   jax.ShapeDtypeStruct((nb, SUBLANE, WIDTH_B), F32)] + cast_shapes,
        scratch_shapes=scratch,
        compiler_params=_params(1),
        name="merge",
    )(o, proj, proj, proj, proj, proj, x2d, state8, conv_w, wa, wb, wo, g2, *cast_weights)
    return hres, hn, state_rows, cast


FFN_TILE_F = 512
FFN_SUB_TILES = 2


def _ffn_kernel(nseq, tps, nf, hn_ref, h_ref, wg_ref, wv_ref, cwg_ref, cwv_ref, stg_ref, stv_ref,
                wd_ref, fg_ref, y_ref, sog_ref, sov_ref, *scratch):
    i = pl.program_id(0)
    f = pl.program_id(1)
    rows = hn_ref.shape[0]
    sub = rows // FFN_SUB_TILES
    branches = ((wg_ref, cwg_ref, stg_ref, sog_ref), (wv_ref, cwv_ref, stv_ref, sov_ref))

    @pl.when(f == 0)
    def _():
        y_ref[...] = jnp.zeros_like(y_ref)

    if tps > 1:
        halos = [scratch[0].at[2 * f + slot] for slot in range(2)]

        @pl.when(i % tps == 0)
        def _():
            for halo, (_, _, st_ref, _) in zip(halos, branches):
                halo[...] = st_ref[...]

        prev = [halo[...] for halo in halos]
    else:
        assert sub % (rows // nseq) == 0
        seqs = nseq // FFN_SUB_TILES

    for r in range(FFN_SUB_TILES):
        tile_rows = slice(r * sub, (r + 1) * sub)
        hn = hn_ref[tile_rows, :]
        convs = []
        for slot, (w_ref, cw_ref, st_ref, so_ref) in enumerate(branches):
            up = jnp.dot(hn, w_ref[...], preferred_element_type=F32)
            if tps > 1:
                up3 = up.reshape(1, sub, up.shape[1])
                prev8 = prev[slot]
                prev[slot] = up3[:, sub - SUBLANE:, :]
            else:
                up3 = up.reshape(seqs, sub // seqs, up.shape[1])
                prev8 = st_ref[r * seqs:(r + 1) * seqs]
                so_ref[r * seqs:(r + 1) * seqs] = up3[:, sub // seqs - SUBLANE:, :]
            convs.append(_causal_conv3(up3, prev8, cw_ref[...]).reshape(up.shape))
        act = (jax.nn.silu(convs[0]) * convs[1]).astype(BF16)
        part = jnp.dot(act, wd_ref[...], preferred_element_type=F32)
        y_ref[tile_rows, :] += part

    if tps > 1:
        for halo, last8, (_, _, _, so_ref) in zip(halos, prev, branches):
            halo[...] = last8
            so_ref[...] = last8

    @pl.when(f == nf - 1)
    def _():
        y_ref[...] = _rms(h_ref[...] + y_ref[...], fg_ref[...])


def _ffn(hn, hres, w_up, ffn_conv_w, state8, w_down, final_g, nb, seq, rows):
    t = hn.shape[0]
    tf = FFN_TILE_F
    nf = D_FF // tf
    nseq = max(rows // seq, 1)
    tps = max(seq // rows, 1)
    assert D_FF % tf == 0 and t % rows == 0 and (rows % seq == 0 or seq % rows == 0)
    tok = pl.BlockSpec((rows, D_MODEL), lambda i, f: (i, 0))
    gate_cols = lambda r: pl.BlockSpec((r, tf), lambda i, f: (0, f))
    val_cols = lambda r: pl.BlockSpec((r, tf), lambda i, f: (0, nf + f))
    st_gate = pl.BlockSpec((nseq, SUBLANE, tf), lambda i, f: (i // tps, 0, f))
    st_val = pl.BlockSpec((nseq, SUBLANE, tf), lambda i, f: (i // tps, 0, nf + f))
    resid = pl.BlockSpec((rows, D_MODEL), lambda i, f: (i, 0), pipeline_mode=pl.Buffered(1))
    scratch = [pltpu.VMEM((2 * nf, nseq, SUBLANE, tf), F32)] if tps > 1 else []
    tile_rows = pl.BlockSpec((nseq, SUBLANE, tf), lambda i, f: (i, 0, f))
    fg = final_g.reshape(1, D_MODEL)
    y, tail_g, tail_v = pl.pallas_call(
        functools.partial(_ffn_kernel, nseq, tps, nf),
        grid=(t // rows, nf),
        in_specs=[tok, resid, gate_cols(D_MODEL), val_cols(D_MODEL), gate_cols(CONV_W), val_cols(CONV_W),
                  st_gate, st_val,
                  pl.BlockSpec((tf, D_MODEL), lambda i, f: (f, 0)),
                  pl.BlockSpec((1, D_MODEL), lambda i, f: (0, 0))],
        out_specs=[tok, tile_rows, tile_rows],
        out_shape=[jax.ShapeDtypeStruct((t, D_MODEL), F32),
                   jax.ShapeDtypeStruct((nb * tps, SUBLANE, D_FF), F32),
                   jax.ShapeDtypeStruct((nb * tps, SUBLANE, D_FF), F32)],
        scratch_shapes=scratch,
        compiler_params=_params(2),
        name="ffn",
    )(hn, hres, w_up, w_up, ffn_conv_w, ffn_conv_w, state8, state8, w_down, fg)
    return y, tail_g[tps - 1::tps], tail_v[tps - 1::tps]


def _state8(state):
    return jnp.pad(state, ((0, 0), (SUBLANE - (CONV_W - 1), 0), (0, 0)))


def _state_out(state8):
    return state8[:, SUBLANE - (CONV_W - 1):, :]


def _layer(x2d, nb, seq, attn_fn, conv8, ffn8, lp, tiles, cast_in_attn=(), cast_in_merge=()):
    lp = dict(lp)
    proj, k_f32, v_f32 = _in_proj(x2d, lp["norm1_g"], lp["w_in"], tiles["in"])
    o, done = attn_fn(proj, [lp[name] for name in cast_in_attn])
    lp.update(zip(cast_in_attn, done))
    hres, hn, conv_out8, done = _merge(o, proj, x2d, conv8, lp["conv_w"], lp["w_proj_a"], lp["w_proj_b"],
                                       lp["w_o"], lp["norm2_g"], nb, seq, tiles["merge"],
                                       [lp[name] for name in cast_in_merge])
    lp.update(zip(cast_in_merge, done))
    y, ffn_g8, ffn_v8 = _ffn(hn, hres, lp["w_up"], lp["ffn_conv_w"], ffn8, lp["w_down"], lp["final_g"],
                             nb, seq, tiles["ffn"])
    k = k_f32.reshape(nb, seq, N_HEADS, D_HEAD)
    v = v_f32.reshape(nb, seq, N_HEADS, D_HEAD)
    conv_state = _state_out(conv_out8)
    ffn_state = jnp.concatenate([_state_out(ffn_g8), _state_out(ffn_v8)], axis=-1)
    return (y, k, v, conv_state, ffn_state), lp


PROMPT_TILES = {"in": 512, "merge": 256, "ffn": 1024}
SAMPLE_TILES = {"in": 512, "merge": 256, "ffn": 1024}
ATTN_TILE_Q = 512


def kernel(x_prompt, x_sample, cache_k, cache_v, state_conv_mix, state_conv_ffn, rel_bias, norm1_g, w_in,
           lambda_q1, lambda_k1, lambda_q2, lambda_k2, subln_g, conv_w, w_proj_a, w_proj_b, w_o, norm2_g,
           w_up, ffn_conv_w, w_down, final_g):
    depth = w_in.shape[0]
    nbp, seq_p, _ = x_prompt.shape
    nbs, seq_s, _ = x_sample.shape
    past = cache_k.shape[2]
    tq = ATTN_TILE_Q

    assert depth == 1, "the ffn kernel ends with the final RMSNorm, so it is written for a single layer"
    (bias_near, bias_diag, bias_cache, bias_new), (w_in_bf,) = _bias_tiles(
        rel_bias,
        ((tq, tq, tq, 0), (tq, tq, tq, tq), (seq_s, past, past, 0), (seq_s, seq_s, past, past)),
        cast_weights=(w_in[0],))

    hp = x_prompt.reshape(nbp * seq_p, D_MODEL)
    hs = x_sample.reshape(nbs * seq_s, D_MODEL)
    outs_p, outs_s = [], []
    for d in range(depth):
        lp = {
            "norm1_g": norm1_g[d], "w_in": w_in_bf, "conv_w": conv_w[d],
            "w_proj_a": w_proj_a[d], "w_proj_b": w_proj_b[d], "w_o": w_o[d], "norm2_g": norm2_g[d],
            "w_up": w_up[d], "ffn_conv_w": ffn_conv_w[d], "w_down": w_down[d], "final_g": final_g,
        }
        li = _lambda_init(d)
        lam_vecs = tuple(a[d].reshape(1, D_QK) for a in (lambda_q1, lambda_k1, lambda_q2, lambda_k2))

        attn_p = lambda qkv, cast: _attn_prompt(qkv, rel_bias, lam_vecs, subln_g[d], bias_near, bias_diag,
                                                nbp, seq_p, tq, li, cast)
        zeros_mix = jnp.zeros((nbp, SUBLANE, WIDTH_B), F32)
        zeros_ffn = jnp.zeros((nbp, SUBLANE, 2 * D_FF), F32)
        (hp, kp, vp, cmp_, cfp), lp = _layer(
            hp, nbp, seq_p, attn_p, zeros_mix, zeros_ffn, lp, PROMPT_TILES,
            cast_in_attn=("w_proj_a", "w_proj_b", "w_o", "w_down"), cast_in_merge=("w_up",))

        attn_s = lambda qkv, cast: (_attn_sample(qkv, cache_k, cache_v, d, lam_vecs, subln_g[d], bias_cache,
                                                 bias_new, nbs, seq_s, past, li), ())
        (hs, ks, vs, cms, cfs), _ = _layer(hs, nbs, seq_s, attn_s, _state8(state_conv_mix[d]),
                                           _state8(state_conv_ffn[d]), lp, SAMPLE_TILES)
        outs_p.append((kp, vp, cmp_, cfp))
        outs_s.append((ks, vs, cms, cfs))

    stack = lambda outs, idx: jnp.stack([o[idx] for o in outs])
    return (hp.reshape(nbp, seq_p, D_MODEL), hs.reshape(nbs, seq_s, D_MODEL),
            stack(outs_p, 0), stack(outs_p, 1), stack(outs_p, 2), stack(outs_p, 3),
            stack(outs_s, 0), stack(outs_s, 1), stack(outs_s, 2), stack(outs_s, 3))
```

```python
import functools
import math

import jax
import jax.numpy as jnp
from jax import lax
from jax.experimental import pallas as pl
from jax.experimental.pallas import tpu as pltpu

D_MODEL = 2048
N_HEADS = 8
D_QK = 64
D_HEAD = 2 * D_QK
WIDTH_A = N_HEADS * D_HEAD
WIDTH_B = D_MODEL // 2
D_FF = 5632
CONV_W = 3
CHUNK = 64
N_BUCKETS = 32
MAX_DIST = 128
EPS = 1e-6
NEG_INF = -1e30

SUBLANE = 8
LANE = 128
VMEM_LIMIT_BYTES = 60 * 1024 * 1024

F32 = jnp.float32
BF16 = jnp.bfloat16


def _bucket_thresholds():
    nb = N_BUCKETS // 2
    max_exact = nb // 2
    steps = nb - max_exact
    out = []
    for k in range(1, steps):
        n = max_exact
        while n ** steps * max_exact ** k < MAX_DIST ** k * max_exact ** steps:
            n += 1
        out.append(n)
    return tuple(out)


_BUCKET_THRESHOLDS = _bucket_thresholds()
FAR_DIST = _BUCKET_THRESHOLDS[-1]
FAR_BUCKET = N_BUCKETS // 2 - 1
_CHUNK_SHIFT = CHUNK.bit_length() - 1
assert 1 << _CHUNK_SHIFT == CHUNK


def _bucket_of(rel):
    nb = N_BUCKETS // 2
    max_exact = nb // 2
    n = abs(rel)
    large = max_exact + sum(n >= thr for thr in _BUCKET_THRESHOLDS)
    return (nb if rel > 0 else 0) + (n if n < max_exact else large)


def _lambda_init(layer_idx):
    return 0.8 - 0.6 * math.exp(-0.3 * layer_idx)


def _rms(x, g):
    return x * lax.rsqrt(jnp.mean(x * x, axis=-1, keepdims=True) + EPS) * g


def _params(n_axes):
    return pltpu.CompilerParams(
        dimension_semantics=("arbitrary",) * n_axes, vmem_limit_bytes=VMEM_LIMIT_BYTES)


def _cast_slabs(in_refs, out_refs):
    for i_ref, o_ref in zip(in_refs, out_refs):
        o_ref[...] = i_ref[...].astype(BF16)


def _cast_slab_specs(weights, n_steps, step_of):
    specs, shapes = [], []
    for w in weights:
        rows, cols = w.shape
        assert rows % n_steps == 0 and (rows // n_steps) % (2 * SUBLANE) == 0 and cols % LANE == 0
        specs.append(pl.BlockSpec((rows // n_steps, cols), lambda *idx: (step_of(*idx), 0)))
        shapes.append(jax.ShapeDtypeStruct(w.shape, BF16))
    return specs, list(specs), shapes


def _bias_tile_kernel(specs, table_ref, *refs):
    n_cast = (len(refs) - len(specs)) // 2
    cast_in, out_refs, cast_out = refs[:n_cast], refs[n_cast:n_cast + len(specs)], refs[n_cast + len(specs):]
    _cast_slabs(cast_in, cast_out)
    h = pl.program_id(0)
    nb = N_BUCKETS // 2
    max_exact = nb // 2
    for (rows, cols, q0, k0), out_ref in zip(specs, out_refs):
        rel0 = k0 - q0 - (rows - SUBLANE)
        width = -(-(cols + rows - SUBLANE) // LANE) * LANE
        rel = (lax.broadcasted_iota(jnp.int32, (SUBLANE, width), 1)
               - lax.broadcasted_iota(jnp.int32, (SUBLANE, width), 0) + rel0)
        n = jnp.abs(rel)
        large = jnp.full((SUBLANE, width), max_exact, jnp.int32)
        for thr in _BUCKET_THRESHOLDS:
            large = large + jnp.where(n >= thr, 1, 0)
        bucket = jnp.where(rel > 0, nb, 0) + jnp.where(n < max_exact, n, large)
        strip = jnp.zeros((SUBLANE, width), F32)
        rel_hi = min(rel0 + width - 1, CHUNK - 1)
        for b in sorted({_bucket_of(r) for r in range(rel0 - SUBLANE + 1, rel_hi + 1)}):
            strip = jnp.where(bucket == b, table_ref[b, h], strip)
        assert q0 % SUBLANE == 0 and CHUNK % SUBLANE == 0
        k_chunk = lax.shift_right_logical(k0 + lax.broadcasted_iota(jnp.int32, (SUBLANE, cols), 1), _CHUNK_SHIFT)
        for a in range(rows // SUBLANE):
            start = rows - SUBLANE * (a + 1)
            visible = k_chunk <= (q0 + SUBLANE * a) // CHUNK
            out_ref[SUBLANE * a:SUBLANE * (a + 1), :] = jnp.where(visible, strip[:, start:start + cols], NEG_INF)


def _bias_tiles(rel_bias, specs, cast_weights=()):
    cast_in, cast_out, cast_shapes = _cast_slab_specs(cast_weights, N_HEADS, lambda h: h)
    outs = pl.pallas_call(
        functools.partial(_bias_tile_kernel, specs),
        grid=(N_HEADS,),
        in_specs=[pl.BlockSpec(memory_space=pltpu.SMEM)] + cast_in,
        out_specs=[pl.BlockSpec((None, r, c), lambda h: (h, 0, 0)) for r, c, _, _ in specs] + cast_out,
        out_shape=[jax.ShapeDtypeStruct((N_HEADS, r, c), F32) for r, c, _, _ in specs] + cast_shapes,
        compiler_params=_params(1),
        name="bias_tiles",
    )(rel_bias, *cast_weights)
    return outs[:len(specs)], outs[len(specs):]


IN_TILE_N = 2048
IN_WIDTH = 3 * WIDTH_A + 3 * WIDTH_B + 2 * D_MODEL
COL_K = WIDTH_A
COL_V = 2 * WIDTH_A
COL_B = 3 * WIDTH_A
COL_C = COL_B + WIDTH_B
COL_X = COL_C + WIDTH_B
COL_GA = COL_X + WIDTH_B
COL_GB = COL_GA + D_MODEL


def _in_proj_kernel(k_at, v_at, x_ref, g_ref, w_ref, cs_ref, proj_ref, kf_ref, vf_ref, xn_ref):
    j = pl.program_id(1)

    @pl.when(j == 0)
    def _():
        xn_ref[...] = _rms(x_ref[...], g_ref[...]).astype(BF16)

    y = jnp.dot(xn_ref[...], w_ref[...], preferred_element_type=F32)
    proj_ref[...] = (y * cs_ref[...]).astype(BF16)

    for (tile, off), f32_ref in ((k_at, kf_ref), (v_at, vf_ref)):
        @pl.when(j == tile)
        def _():
            for h in range(N_HEADS):
                f32_ref[pl.ds(h, y.shape[0], stride=N_HEADS), :] = y[:, off + h * D_HEAD:off + (h + 1) * D_HEAD]


def _in_proj(x2d, norm_g, w_in_bf, tm):
    t = x2d.shape[0]
    tn = IN_TILE_N
    assert t % tm == 0 and w_in_bf.shape == (D_MODEL, IN_WIDTH) and IN_WIDTH % tn == 0
    k_at, v_at = divmod(COL_K, tn), divmod(COL_V, tn)
    assert k_at[1] + WIDTH_A <= tn and v_at[1] + WIDTH_A <= tn
    col_scale = jnp.concatenate([jnp.full((1, WIDTH_A), D_QK ** -0.5, F32),
                                 jnp.ones((1, IN_WIDTH - WIDTH_A), F32)], axis=1)
    f32_cols = pl.BlockSpec((tm * N_HEADS, D_HEAD), lambda i, j: (i, 0))
    return pl.pallas_call(
        functools.partial(_in_proj_kernel, k_at, v_at),
        grid=(t // tm, IN_WIDTH // tn),
        in_specs=[
            pl.BlockSpec((tm, D_MODEL), lambda i, j: (i, 0)),
            pl.BlockSpec((1, D_MODEL), lambda i, j: (0, 0)),
            pl.BlockSpec((D_MODEL, tn), lambda i, j: (0, j)),
            pl.BlockSpec((1, tn), lambda i, j: (0, j)),
        ],
        out_specs=[pl.BlockSpec((tm, tn), lambda i, j: (i, j)), f32_cols, f32_cols],
        out_shape=[
            jax.ShapeDtypeStruct((t, IN_WIDTH), BF16),
            jax.ShapeDtypeStruct((t * N_HEADS, D_HEAD), F32),
            jax.ShapeDtypeStruct((t * N_HEADS, D_HEAD), F32),
        ],
        scratch_shapes=[pltpu.VMEM((tm, D_MODEL), BF16)],
        compiler_params=_params(2),
        name="in_proj",
    )(x2d, norm_g.reshape(1, D_MODEL), w_in_bf, col_scale)


def _scores(qm, k):
    return lax.dot_general(qm, k, (((1,), (1,)), ((), ())), preferred_element_type=F32)


def _tile_lanes(x, n):
    return x if n == 1 else jnp.concatenate([x] * n, axis=1)


def _with_ones(v):
    return jnp.concatenate([v, jnp.ones_like(v)], axis=1)


def _softmax_step(s, v_ext, state, shift=None):
    m_ref, acc_ref = state
    keys = s.shape[1]
    assert keys % LANE == 0
    row_max = jnp.max(s, axis=-1, keepdims=True)
    if shift is not None:
        row_max = row_max + shift
    m_old = m_ref[...]
    m_new = jnp.maximum(m_old, row_max)
    sub = m_new if shift is None else m_new - shift
    p = jnp.exp(s - _tile_lanes(sub, keys // LANE))
    pv = jnp.dot(p.astype(BF16), v_ext, preferred_element_type=F32)
    alpha = jnp.exp(m_old - m_new)
    acc_ref[...] = _tile_lanes(alpha, 2 * D_HEAD // LANE) * acc_ref[...] + pv
    m_ref[...] = m_new


def _split_maps(q):
    lane = lax.broadcasted_iota(jnp.int32, q.shape, 1)
    zero = jnp.zeros_like(q)
    return jnp.where(lane < D_QK, q, zero), jnp.where(lane >= D_QK, q, zero)


def _lambda(lq1_ref, lk1_ref, lq2_ref, lk2_ref, lambda_init):
    s1 = jnp.sum(lq1_ref[...] * lk1_ref[...], axis=-1, keepdims=True)
    s2 = jnp.sum(lq2_ref[...] * lk2_ref[...], axis=-1, keepdims=True)
    return jnp.exp(s1) - jnp.exp(s2) + lambda_init


def _attn_finish(state1, state2, lam, g, lambda_init):
    a1 = state1[1][...]
    a2 = state2[1][...]
    o = a1[:, :D_HEAD] / a1[:, D_HEAD:] - lam * (a2[:, :D_HEAD] / a2[:, D_HEAD:])
    return _rms(o, g) * (1.0 - lambda_init)


def _attn_prompt_kernel(tq, nq, lambda_init, n_cast, table_ref, lq1_ref, lk1_ref, lq2_ref, lk2_ref, g_ref,
                        q_ref, k_ref, v_ref, bnear_ref, bdiag_ref, *rest):
    cast_in, o_ref, cast_out = rest[:n_cast], rest[n_cast], rest[n_cast + 1:2 * n_cast + 1]
    m1_ref, a1_ref, m2_ref, a2_ref, sa1_ref, sa2_ref, sb1_ref, sb2_ref = rest[2 * n_cast + 1:]
    _cast_slabs(cast_in, cast_out)
    h = pl.program_id(1)
    far_bias = table_ref[FAR_BUCKET, h]
    lam = _lambda(lq1_ref, lk1_ref, lq2_ref, lk2_ref, lambda_init)
    st1 = (m1_ref, a1_ref)
    st2 = (m2_ref, a2_ref)
    buf_a = (sa1_ref, sa2_ref)
    buf_b = (sb1_ref, sb2_ref)

    def q_tile(qi, carry):
        q0 = pl.multiple_of(qi * tq, tq)
        q1, q2 = _split_maps(q_ref[pl.ds(q0, tq), :])

        def score_block(j, buf):
            k = k_ref[pl.ds(pl.multiple_of(j * tq, tq), tq), :]
            buf[0][...] = _scores(q1, k)
            buf[1][...] = _scores(q2, k)

        def score_next_tile():
            qn1, qn2 = _split_maps(q_ref[pl.ds(pl.multiple_of(q0 + tq, tq), tq), :])
            k = k_ref[0:tq, :]
            buf_b[0][...] = _scores(qn1, k)
            buf_b[1][...] = _scores(qn2, k)

        def step(j, buf, nxt=None, bias=None, shift=None, next_tile=False):
            if nxt is not None:
                score_block(j + 1, nxt)
            if next_tile:
                score_next_tile()
            v = _with_ones(v_ref[pl.ds(pl.multiple_of(j * tq, tq), tq), :])
            s1 = buf[0][...]
            s2 = buf[1][...]
            if bias is not None:
                s1 = s1 + bias
                s2 = s2 + bias
            _softmax_step(s1, v, st1, shift=shift)
            _softmax_step(s2, v, st2, shift=shift)

        for m_ref, a_ref in (st1, st2):
            m_ref[...] = jnp.full(m_ref.shape, NEG_INF, F32)
            a_ref[...] = jnp.zeros(a_ref.shape, F32)

        n_far = jnp.maximum(qi - 1, 0)
        odd_far = jnp.bitwise_and(n_far, 1)
        qi_even = jnp.bitwise_and(qi, 1) == 0

        @pl.when(qi_even)
        def _():
            score_block(0, buf_a)

        @pl.when(odd_far == 1)
        def _():
            step(0, buf_a, nxt=buf_b, shift=far_bias)

        def far_pair(t, c):
            j = odd_far + 2 * t
            step(j, buf_b, nxt=buf_a, shift=far_bias)
            step(j + 1, buf_a, nxt=buf_b, shift=far_bias)
            return c

        lax.fori_loop(0, lax.shift_right_logical(n_far, 1), far_pair, 0)

        @pl.when(qi > 0)
        def _():
            step(qi - 1, buf_b, nxt=buf_a, bias=bnear_ref[...])

        feeds_next = jnp.logical_and(qi_even, qi + 1 < nq)

        @pl.when(feeds_next)
        def _():
            step(qi, buf_a, bias=bdiag_ref[...], next_tile=True)

        @pl.when(jnp.logical_not(feeds_next))
        def _():
            step(qi, buf_a, bias=bdiag_ref[...])

        o = _attn_finish(st1, st2, lam, g_ref[...], lambda_init)
        o_ref[pl.ds(q0, tq), :] = o.astype(BF16)
        return carry

    lax.fori_loop(0, nq, q_tile, 0)


def _attn_prompt(proj, rel_bias, lam_vecs, subln_g, bias_near, bias_diag, nb, seq, tq, lambda_init,
                 cast_weights=()):
    assert seq % tq == 0 and tq % CHUNK == 0 and tq >= FAR_DIST
    qkv3 = proj.reshape(nb, seq, IN_WIDTH)
    vec = pl.BlockSpec((1, D_QK), lambda b, h: (0, 0))
    head_cols = lambda off: pl.BlockSpec((None, seq, D_HEAD), lambda b, h: (b, 0, off + h))
    tile = pl.BlockSpec((None, tq, tq), lambda b, h: (h, 0, 0))
    stat = pltpu.VMEM((tq, LANE), F32)
    acc = pltpu.VMEM((tq, 2 * D_HEAD), F32)
    cast_in, cast_out, cast_shapes = _cast_slab_specs(cast_weights, nb * N_HEADS, lambda b, h: b * N_HEADS + h)
    o, *cast = pl.pallas_call(
        functools.partial(_attn_prompt_kernel, tq, seq // tq, lambda_init, len(cast_weights)),
        grid=(nb, N_HEADS),
        in_specs=[pl.BlockSpec(memory_space=pltpu.SMEM), vec, vec, vec, vec,
                  pl.BlockSpec((1, D_HEAD), lambda b, h: (0, 0)),
                  head_cols(0), head_cols(COL_K // D_HEAD), head_cols(COL_V // D_HEAD), tile, tile] + cast_in,
        out_specs=[pl.BlockSpec((None, seq, D_HEAD), lambda b, h: (b, 0, h))] + cast_out,
        out_shape=[jax.ShapeDtypeStruct((nb, seq, WIDTH_A), BF16)] + cast_shapes,
        scratch_shapes=[stat, acc, stat, acc] + [pltpu.VMEM((tq, tq), F32)] * 4,
        compiler_params=_params(2),
        name="attn_prompt",
    )(rel_bias, *lam_vecs, subln_g.reshape(1, D_HEAD), qkv3, qkv3, qkv3, bias_near, bias_diag, *cast_weights)
    return o.reshape(nb * seq, WIDTH_A), cast


def _softmax_two_blocks(qm, k_a, v_a, bias_a, k_b, v_b, bias_b):
    s_a = _scores(qm, k_a) + bias_a
    s_b = _scores(qm, k_b) + bias_b
    m = jnp.maximum(jnp.max(s_a, axis=-1, keepdims=True), jnp.max(s_b, axis=-1, keepdims=True))
    p_a = jnp.exp(s_a - m).astype(BF16)
    p_b = jnp.exp(s_b - m).astype(BF16)
    return jnp.dot(p_a, v_a, preferred_element_type=F32) + jnp.dot(p_b, v_b, preferred_element_type=F32)


def _attn_sample_kernel(lambda_init, past, lq1_ref, lk1_ref, lq2_ref, lk2_ref, g_ref,
                        qkv_ref, kc_ref, vc_ref, bc_ref, bn_ref, o_ref):
    lam = _lambda(lq1_ref, lk1_ref, lq2_ref, lk2_ref, lambda_init)
    for h in range(N_HEADS):
        cols = lambda part: slice((part * N_HEADS + h) * D_HEAD, (part * N_HEADS + h + 1) * D_HEAD)
        q1, q2 = _split_maps(qkv_ref[:, cols(0)])
        kn = qkv_ref[:, cols(1)]
        vn = _with_ones(qkv_ref[:, cols(2)])
        kc = kc_ref[pl.ds(h, past, stride=N_HEADS), :].astype(BF16)
        vc = _with_ones(vc_ref[pl.ds(h, past, stride=N_HEADS), :].astype(BF16))
        a1 = _softmax_two_blocks(q1, kc, vc, bc_ref[h], kn, vn, bn_ref[h])
        a2 = _softmax_two_blocks(q2, kc, vc, bc_ref[h], kn, vn, bn_ref[h])
        o = a1[:, :D_HEAD] / a1[:, D_HEAD:] - lam * (a2[:, :D_HEAD] / a2[:, D_HEAD:])
        o_ref[:, h * D_HEAD:(h + 1) * D_HEAD] = (_rms(o, g_ref[...]) * (1.0 - lambda_init)).astype(BF16)


def _attn_sample(proj, cache_k, cache_v, layer, lam_vecs, subln_g, bias_cache, bias_new, nb, seq, past,
                 lambda_init):
    qkv3 = proj.reshape(nb, seq, IN_WIDTH)
    kc = cache_k.reshape(-1, past * N_HEADS, D_HEAD)
    vc = cache_v.reshape(-1, past * N_HEADS, D_HEAD)
    vec = pl.BlockSpec((1, D_QK), lambda b: (0, 0))
    past_rows = pl.BlockSpec((None, past * N_HEADS, D_HEAD), lambda b: (layer * nb + b, 0, 0))
    full = lambda a: pl.BlockSpec(a.shape, lambda b: (0,) * a.ndim)
    o = pl.pallas_call(
        functools.partial(_attn_sample_kernel, lambda_init, past),
        grid=(nb,),
        in_specs=[vec, vec, vec, vec, pl.BlockSpec((1, D_HEAD), lambda b: (0, 0)),
                  pl.BlockSpec((None, seq, 3 * WIDTH_A), lambda b: (b, 0, 0)), past_rows, past_rows,
                  full(bias_cache), full(bias_new)],
        out_specs=pl.BlockSpec((None, seq, WIDTH_A), lambda b: (b, 0, 0)),
        out_shape=jax.ShapeDtypeStruct((nb, seq, WIDTH_A), BF16),
        compiler_params=_params(1),
        name="attn_sample",
    )(*lam_vecs, subln_g.reshape(1, D_HEAD), qkv3, kc, vc, bias_cache, bias_new)
    return o.reshape(nb * seq, WIDTH_A)


def _causal_conv3(x3, prev8, w):
    p0 = prev8[:, SUBLANE - 2:SUBLANE - 1, :]
    p1 = prev8[:, SUBLANE - 1:SUBLANE, :]
    row = lax.broadcasted_iota(jnp.int32, x3.shape, 1)
    x1 = jnp.where(row == 0, p1, pltpu.roll(x3, 1, axis=1))
    x2 = jnp.where(row == 0, p0, jnp.where(row == 1, p1, pltpu.roll(x3, 2, axis=1)))
    return x2 * w[0:1, :] + x1 * w[1:2, :] + x3 * w[2:3, :]


def _conv_with_state(x, i, nseq, tps, st_ref, so_ref, halo):
    rows = x.shape[0] // nseq
    x3 = x.reshape(nseq, rows, x.shape[1])
    last8 = x3[:, rows - SUBLANE:, :]
    if tps == 1:
        prev8 = st_ref[...]
    else:
        @pl.when(i % tps == 0)
        def _():
            halo[...] = st_ref[...]

        prev8 = halo[...]
        halo[...] = last8
    so_ref[...] = last8
    return x3, prev8


def _merge_kernel(nseq, tps, n_cast, o_ref, b_ref, c_ref, xi_ref, ga_ref, gb_ref, x_ref, st_ref, cw_ref,
                  wa_ref, wb_ref, wo_ref, g2_ref, *rest):
    cast_in, (h_ref, hn_ref, so_ref) = rest[:n_cast], rest[n_cast:n_cast + 3]
    cast_out, scratch = rest[n_cast + 3:2 * n_cast + 3], rest[2 * n_cast + 3:]
    _cast_slabs(cast_in, cast_out)
    i = pl.program_id(0)
    u = c_ref[...].astype(F32) * xi_ref[...].astype(F32)
    u3, prev8 = _conv_with_state(u, i, nseq, tps, st_ref, so_ref, scratch[0] if tps > 1 else None)
    z = _causal_conv3(u3, prev8, cw_ref[...]).reshape(u.shape)
    ob = (b_ref[...].astype(F32) * z).astype(BF16)
    ya = jnp.dot(o_ref[...], wa_ref[...], preferred_element_type=F32)
    yb = jnp.dot(ob, wb_ref[...], preferred_element_type=F32)
    merged = jax.nn.sigmoid(ga_ref[...].astype(F32)) * ya + jax.nn.sigmoid(gb_ref[...].astype(F32)) * yb
    hres = x_ref[...] + jnp.dot(merged.astype(BF16), wo_ref[...], preferred_element_type=F32)
    h_ref[...] = hres
    hn_ref[...] = _rms(hres, g2_ref[...]).astype(BF16)


def _merge(o, proj, x2d, state8, conv_w, wa, wb, wo, norm2_g, nb, seq, rows, cast_weights=()):
    t = x2d.shape[0]
    nseq = max(rows // seq, 1)
    tps = max(seq // rows, 1)
    assert t % rows == 0 and (rows % seq == 0 or seq % rows == 0)
    tok = lambda width, col: pl.BlockSpec((rows, width), lambda i: (i, col))
    full = lambda a: pl.BlockSpec(a.shape, lambda i: (0,) * a.ndim)
    state_spec = pl.BlockSpec((nseq, SUBLANE, WIDTH_B), lambda i: (i // tps, 0, 0))
    scratch = [pltpu.VMEM((nseq, SUBLANE, WIDTH_B), F32)] if tps > 1 else []
    g2 = norm2_g.reshape(1, D_MODEL)
    cast_in, cast_out, cast_shapes = _cast_slab_specs(cast_weights, t // rows, lambda i: i)
    hres, hn, state_rows, *cast = pl.pallas_call(
        functools.partial(_merge_kernel, nseq, tps, len(cast_weights)),
        grid=(t // rows,),
        in_specs=[tok(WIDTH_A, 0), tok(WIDTH_B, COL_B // WIDTH_B), tok(WIDTH_B, COL_C // WIDTH_B),
                  tok(WIDTH_B, COL_X // WIDTH_B), tok(D_MODEL, COL_GA // D_MODEL), tok(D_MODEL, COL_GB // D_MODEL),
                  tok(D_MODEL, 0), state_spec,
                  full(conv_w), full(wa), full(wb), full(wo), full(g2)] + cast_in,
        out_specs=[tok(D_MODEL, 0), tok(D_MODEL, 0), state_spec] + cast_out,
        out_shape=[jax.ShapeDtypeStruct((t, D_MODEL), F32),
                   jax.ShapeDtypeStruct((t, D_MODEL), BF16),
                   jax.ShapeDtypeStruct((nb, SUBLANE, WIDTH_B), F32)] + cast_shapes,
        scratch_shapes=scratch,
        compiler_params=_params(1),
        name="merge",
    )(o, proj, proj, proj, proj, proj, x2d, state8, conv_w, wa, wb, wo, g2, *cast_weights)
    return hres, hn, state_rows, cast


FFN_TILE_F = 512
FFN_SUB_TILES = 2


def _ffn_kernel(nseq, tps, nf, hn_ref, h_ref, wg_ref, wv_ref, cwg_ref, cwv_ref, stg_ref, stv_ref,
                wd_ref, fg_ref, y_ref, sog_ref, sov_ref, act_ref, *scratch):
    s = pl.program_id(0)
    i = lax.div(s, nf)
    f = lax.rem(s, nf)
    f_down = lax.rem(jnp.maximum(s - 1, 0), nf)
    act_new = act_ref.at[lax.rem(s, 2)]
    act_old = act_ref.at[1 - lax.rem(s, 2)]
    rows = hn_ref.shape[0]
    sub = rows // FFN_SUB_TILES
    branches = ((wg_ref, cwg_ref, stg_ref, sog_ref), (wv_ref, cwv_ref, stv_ref, sov_ref))

    @pl.when(s == 0)
    def _():
        act_ref[...] = jnp.zeros_like(act_ref)

    @pl.when(f_down == 0)
    def _():
        y_ref[...] = jnp.zeros_like(y_ref)

    if tps > 1:
        halos = [scratch[0].at[2 * f + slot] for slot in range(2)]

        @pl.when(lax.rem(i, tps) == 0)
        def _():
            for halo, (_, _, st_ref, _) in zip(halos, branches):
                halo[...] = st_ref[...]

        prev = [halo[...] for halo in halos]
    else:
        assert sub % (rows // nseq) == 0
        seqs = nseq // FFN_SUB_TILES

    for r in range(FFN_SUB_TILES):
        tile_rows = slice(r * sub, (r + 1) * sub)
        y_ref[tile_rows, :] += jnp.dot(act_old[tile_rows, :], wd_ref[...], preferred_element_type=F32)
        hn = hn_ref[tile_rows, :]
        convs = []
        for slot, (w_ref, cw_ref, st_ref, so_ref) in enumerate(branches):
            up = jnp.dot(hn, w_ref[...], preferred_element_type=F32)
            if tps > 1:
                up3 = up.reshape(1, sub, up.shape[1])
                prev8 = prev[slot]
                prev[slot] = up3[:, sub - SUBLANE:, :]
            else:
                up3 = up.reshape(seqs, sub // seqs, up.shape[1])
                prev8 = st_ref[r * seqs:(r + 1) * seqs]
                so_ref[r * seqs:(r + 1) * seqs] = up3[:, sub // seqs - SUBLANE:, :]
            convs.append(_causal_conv3(up3, prev8, cw_ref[...]).reshape(up.shape))
        act_new[tile_rows, :] = (jax.nn.silu(convs[0]) * convs[1]).astype(BF16)

    if tps > 1:
        for halo, last8, (_, _, _, so_ref) in zip(halos, prev, branches):
            halo[...] = last8
            so_ref[...] = last8

    @pl.when(jnp.logical_and(f_down == nf - 1, s > 0))
    def _():
        y_ref[...] = _rms(h_ref[...] + y_ref[...], fg_ref[...])


def _ffn(hn, hres, w_up, ffn_conv_w, state8, w_down, final_g, nb, seq, rows):
    t = hn.shape[0]
    tf = FFN_TILE_F
    nf = D_FF // tf
    nseq = max(rows // seq, 1)
    tps = max(seq // rows, 1)
    assert D_FF % tf == 0 and t % rows == 0 and (rows % seq == 0 or seq % rows == 0)
    n_tiles = t // rows
    up_i = lambda s: jnp.minimum(s // nf, n_tiles - 1)
    up_f = lambda s: s % nf
    down_i = lambda s: jnp.maximum(s - 1, 0) // nf
    down_f = lambda s: jnp.maximum(s - 1, 0) % nf
    gate_cols = lambda r: pl.BlockSpec((r, tf), lambda s: (0, up_f(s)))
    val_cols = lambda r: pl.BlockSpec((r, tf), lambda s: (0, nf + up_f(s)))
    st_gate = pl.BlockSpec((nseq, SUBLANE, tf), lambda s: (up_i(s) // tps, 0, up_f(s)))
    st_val = pl.BlockSpec((nseq, SUBLANE, tf), lambda s: (up_i(s) // tps, 0, nf + up_f(s)))
    resid = pl.BlockSpec((rows, D_MODEL), lambda s: (down_i(s), 0), pipeline_mode=pl.Buffered(1))
    scratch = [pltpu.VMEM((2, rows, tf), BF16)]
    if tps > 1:
        scratch.append(pltpu.VMEM((2 * nf, nseq, SUBLANE, tf), F32))
    tile_tail = pl.BlockSpec((nseq, SUBLANE, tf), lambda s: (s // nf, 0, up_f(s)))
    fg = final_g.reshape(1, D_MODEL)
    y, tail_g, tail_v = pl.pallas_call(
        functools.partial(_ffn_kernel, nseq, tps, nf),
        grid=(n_tiles * nf + 1,),
        in_specs=[pl.BlockSpec((rows, D_MODEL), lambda s: (up_i(s), 0)), resid,
                  gate_cols(D_MODEL), val_cols(D_MODEL), gate_cols(CONV_W), val_cols(CONV_W),
                  st_gate, st_val,
                  pl.BlockSpec((tf, D_MODEL), lambda s: (down_f(s), 0)),
                  pl.BlockSpec((1, D_MODEL), lambda s: (0, 0))],
        out_specs=[pl.BlockSpec((rows, D_MODEL), lambda s: (down_i(s), 0)), tile_tail, tile_tail],
        out_shape=[jax.ShapeDtypeStruct((t, D_MODEL), F32),
                   jax.ShapeDtypeStruct(((n_tiles + 1) * nseq, SUBLANE, D_FF), F32),
                   jax.ShapeDtypeStruct(((n_tiles + 1) * nseq, SUBLANE, D_FF), F32)],
        scratch_shapes=scratch,
        compiler_params=_params(1),
        name="ffn",
    )(hn, hres, w_up, w_up, ffn_conv_w, ffn_conv_w, state8, state8, w_down, fg)
    last_of_seq = slice(tps - 1, n_tiles * nseq, tps)
    return y, tail_g[last_of_seq], tail_v[last_of_seq]


def _state8(state):
    return jnp.pad(state, ((0, 0), (SUBLANE - (CONV_W - 1), 0), (0, 0)))


def _state_out(state8):
    return state8[:, SUBLANE - (CONV_W - 1):, :]


def _layer(x2d, nb, seq, attn_fn, conv8, ffn8, lp, tiles, cast_in_attn=(), cast_in_merge=()):
    lp = dict(lp)
    proj, k_f32, v_f32 = _in_proj(x2d, lp["norm1_g"], lp["w_in"], tiles["in"])
    o, done = attn_fn(proj, [lp[name] for name in cast_in_attn])
    lp.update(zip(cast_in_attn, done))
    hres, hn, conv_out8, done = _merge(o, proj, x2d, conv8, lp["conv_w"], lp["w_proj_a"], lp["w_proj_b"],
                                       lp["w_o"], lp["norm2_g"], nb, seq, tiles["merge"],
                                       [lp[name] for name in cast_in_merge])
    lp.update(zip(cast_in_merge, done))
    y, ffn_g8, ffn_v8 = _ffn(hn, hres, lp["w_up"], lp["ffn_conv_w"], ffn8, lp["w_down"], lp["final_g"],
                             nb, seq, tiles["ffn"])
    k = k_f32.reshape(nb, seq, N_HEADS, D_HEAD)
    v = v_f32.reshape(nb, seq, N_HEADS, D_HEAD)
    conv_state = _state_out(conv_out8)
    ffn_state = jnp.concatenate([_state_out(ffn_g8), _state_out(ffn_v8)], axis=-1)
    return (y, k, v, conv_state, ffn_state), lp


PROMPT_TILES = {"in": 512, "merge": 256, "ffn": 1024}
SAMPLE_TILES = {"in": 512, "merge": 256, "ffn": 1024}
ATTN_TILE_Q = 512


def kernel(x_prompt, x_sample, cache_k, cache_v, state_conv_mix, state_conv_ffn, rel_bias, norm1_g, w_in,
           lambda_q1, lambda_k1, lambda_q2, lambda_k2, subln_g, conv_w, w_proj_a, w_proj_b, w_o, norm2_g,
           w_up, ffn_conv_w, w_down, final_g):
    depth = w_in.shape[0]
    nbp, seq_p, _ = x_prompt.shape
    nbs, seq_s, _ = x_sample.shape
    past = cache_k.shape[2]
    tq = ATTN_TILE_Q

    assert depth == 1, "the ffn kernel ends with the final RMSNorm, so it is written for a single layer"
    (bias_near, bias_diag, bias_cache, bias_new), (w_in_bf,) = _bias_tiles(
        rel_bias,
        ((tq, tq, tq, 0), (tq, tq, tq, tq), (seq_s, past, past, 0), (seq_s, seq_s, past, past)),
        cast_weights=(w_in[0],))

    hp = x_prompt.reshape(nbp * seq_p, D_MODEL)
    hs = x_sample.reshape(nbs * seq_s, D_MODEL)
    outs_p, outs_s = [], []
    for d in range(depth):
        lp = {
            "norm1_g": norm1_g[d], "w_in": w_in_bf, "conv_w": conv_w[d],
            "w_proj_a": w_proj_a[d], "w_proj_b": w_proj_b[d], "w_o": w_o[d], "norm2_g": norm2_g[d],
            "w_up": w_up[d], "ffn_conv_w": ffn_conv_w[d], "w_down": w_down[d], "final_g": final_g,
        }
        li = _lambda_init(d)
        lam_vecs = tuple(a[d].reshape(1, D_QK) for a in (lambda_q1, lambda_k1, lambda_q2, lambda_k2))

        attn_p = lambda qkv, cast: _attn_prompt(qkv, rel_bias, lam_vecs, subln_g[d], bias_near, bias_diag,
                                                nbp, seq_p, tq, li, cast)
        zeros_mix = jnp.zeros((nbp, SUBLANE, WIDTH_B), F32)
        zeros_ffn = jnp.zeros((nbp, SUBLANE, 2 * D_FF), F32)
        (hp, kp, vp, cmp_, cfp), lp = _layer(
            hp, nbp, seq_p, attn_p, zeros_mix, zeros_ffn, lp, PROMPT_TILES,
            cast_in_attn=("w_proj_a", "w_proj_b", "w_o", "w_down"), cast_in_merge=("w_up",))

        attn_s = lambda qkv, cast: (_attn_sample(qkv, cache_k, cache_v, d, lam_vecs, subln_g[d], bias_cache,
                                                 bias_new, nbs, seq_s, past, li), ())
        (hs, ks, vs, cms, cfs), _ = _layer(hs, nbs, seq_s, attn_s, _state8(state_conv_mix[d]),
                                           _state8(state_conv_ffn[d]), lp, SAMPLE_TILES)
        outs_p.append((kp, vp, cmp_, cfp))
        outs_s.append((ks, vs, cms, cfs))

    stack = lambda outs, idx: jnp.stack([o[idx] for o in outs])
    return (hp.reshape(nbp, seq_p, D_MODEL), hs.reshape(nbs, seq_s, D_MODEL),
            stack(outs_p, 0), stack(outs_p, 1), stack(outs_p, 2), stack(outs_p, 3),
            stack(outs_s, 0), stack(outs_s, 1), stack(outs_s, 2), stack(outs_s, 3))
```

```python
import functools
import math

import jax
import jax.numpy as jnp
from jax import lax
from jax.experimental import pallas as pl
from jax.experimental.pallas import tpu as pltpu

D_MODEL = 2048
N_HEADS = 8
D_QK = 64
D_HEAD = 2 * D_QK
WIDTH_A = N_HEADS * D_HEAD
WIDTH_B = D_MODEL // 2
D_FF = 5632
CONV_W = 3
CHUNK = 64
N_BUCKETS = 32
MAX_DIST = 128
EPS = 1e-6
NEG_INF = -1e30

SUBLANE = 8
LANE = 128
VMEM_LIMIT_BYTES = 60 * 1024 * 1024

F32 = jnp.float32
BF16 = jnp.bfloat16


def _bucket_thresholds():
    nb = N_BUCKETS // 2
    max_exact = nb // 2
    steps = nb - max_exact
    out = []
    for k in range(1, steps):
        n = max_exact
        while n ** steps * max_exact ** k < MAX_DIST ** k * max_exact ** steps:
            n += 1
        out.append(n)
    return tuple(out)


_BUCKET_THRESHOLDS = _bucket_thresholds()
FAR_DIST = _BUCKET_THRESHOLDS[-1]
FAR_BUCKET = N_BUCKETS // 2 - 1
_CHUNK_SHIFT = CHUNK.bit_length() - 1
assert 1 << _CHUNK_SHIFT == CHUNK


def _bucket_of(rel):
    nb = N_BUCKETS // 2
    max_exact = nb // 2
    n = abs(rel)
    large = max_exact + sum(n >= thr for thr in _BUCKET_THRESHOLDS)
    return (nb if rel > 0 else 0) + (n if n < max_exact else large)


def _lambda_init(layer_idx):
    return 0.8 - 0.6 * math.exp(-0.3 * layer_idx)


def _rms(x, g):
    return x * lax.rsqrt(jnp.mean(x * x, axis=-1, keepdims=True) + EPS) * g


def _params(n_axes):
    return pltpu.CompilerParams(
        dimension_semantics=("arbitrary",) * n_axes, vmem_limit_bytes=VMEM_LIMIT_BYTES)


def _cast_slabs(in_refs, out_refs):
    for i_ref, o_ref in zip(in_refs, out_refs):
        o_ref[...] = i_ref[...].astype(BF16)


def _cast_slab_specs(weights, n_steps, step_of):
    specs, shapes = [], []
    for w in weights:
        rows, cols = w.shape
        assert rows % n_steps == 0 and (rows // n_steps) % (2 * SUBLANE) == 0 and cols % LANE == 0
        specs.append(pl.BlockSpec((rows // n_steps, cols), lambda *idx: (step_of(*idx), 0)))
        shapes.append(jax.ShapeDtypeStruct(w.shape, BF16))
    return specs, list(specs), shapes


def _bias_tile_kernel(specs, table_ref, *refs):
    n_cast = (len(refs) - len(specs)) // 2
    cast_in, out_refs, cast_out = refs[:n_cast], refs[n_cast:n_cast + len(specs)], refs[n_cast + len(specs):]
    _cast_slabs(cast_in, cast_out)
    h = pl.program_id(0)
    nb = N_BUCKETS // 2
    max_exact = nb // 2
    for (rows, cols, q0, k0), out_ref in zip(specs, out_refs):
        rel0 = k0 - q0 - (rows - SUBLANE)
        width = -(-(cols + rows - SUBLANE) // LANE) * LANE
        rel = (lax.broadcasted_iota(jnp.int32, (SUBLANE, width), 1)
               - lax.broadcasted_iota(jnp.int32, (SUBLANE, width), 0) + rel0)
        n = jnp.abs(rel)
        large = jnp.full((SUBLANE, width), max_exact, jnp.int32)
        for thr in _BUCKET_THRESHOLDS:
            large = large + jnp.where(n >= thr, 1, 0)
        bucket = jnp.where(rel > 0, nb, 0) + jnp.where(n < max_exact, n, large)
        strip = jnp.zeros((SUBLANE, width), F32)
        rel_hi = min(rel0 + width - 1, CHUNK - 1)
        for b in sorted({_bucket_of(r) for r in range(rel0 - SUBLANE + 1, rel_hi + 1)}):
            strip = jnp.where(bucket == b, table_ref[b, h], strip)
        assert q0 % SUBLANE == 0 and CHUNK % SUBLANE == 0
        k_chunk = lax.shift_right_logical(k0 + lax.broadcasted_iota(jnp.int32, (SUBLANE, cols), 1), _CHUNK_SHIFT)
        for a in range(rows // SUBLANE):
            start = rows - SUBLANE * (a + 1)
            visible = k_chunk <= (q0 + SUBLANE * a) // CHUNK
            out_ref[SUBLANE * a:SUBLANE * (a + 1), :] = jnp.where(visible, strip[:, start:start + cols], NEG_INF)


def _bias_tiles(rel_bias, specs, cast_weights=()):
    cast_in, cast_out, cast_shapes = _cast_slab_specs(cast_weights, N_HEADS, lambda h: h)
    outs = pl.pallas_call(
        functools.partial(_bias_tile_kernel, specs),
        grid=(N_HEADS,),
        in_specs=[pl.BlockSpec(memory_space=pltpu.SMEM)] + cast_in,
        out_specs=[pl.BlockSpec((None, r, c), lambda h: (h, 0, 0)) for r, c, _, _ in specs] + cast_out,
        out_shape=[jax.ShapeDtypeStruct((N_HEADS, r, c), F32) for r, c, _, _ in specs] + cast_shapes,
        compiler_params=_params(1),
        name="bias_tiles",
    )(rel_bias, *cast_weights)
    return outs[:len(specs)], outs[len(specs):]


IN_TILE_N = 2048
IN_WIDTH = 3 * WIDTH_A + 3 * WIDTH_B + 2 * D_MODEL
COL_K = WIDTH_A
COL_V = 2 * WIDTH_A
COL_B = 3 * WIDTH_A
COL_C = COL_B + WIDTH_B
COL_X = COL_C + WIDTH_B
COL_GA = COL_X + WIDTH_B
COL_GB = COL_GA + D_MODEL


def _in_proj_kernel(k_at, v_at, x_ref, g_ref, w_ref, cs_ref, proj_ref, kf_ref, vf_ref, xn_ref):
    j = pl.program_id(1)

    @pl.when(j == 0)
    def _():
        xn_ref[...] = _rms(x_ref[...], g_ref[...]).astype(BF16)

    y = jnp.dot(xn_ref[...], w_ref[...], preferred_element_type=F32)
    proj_ref[...] = (y * cs_ref[...]).astype(BF16)

    for (tile, off), f32_ref in ((k_at, kf_ref), (v_at, vf_ref)):
        @pl.when(j == tile)
        def _():
            for h in range(N_HEADS):
                f32_ref[pl.ds(h, y.shape[0], stride=N_HEADS), :] = y[:, off + h * D_HEAD:off + (h + 1) * D_HEAD]


def _in_proj(x2d, norm_g, w_in_bf, tm):
    t = x2d.shape[0]
    tn = IN_TILE_N
    assert t % tm == 0 and w_in_bf.shape == (D_MODEL, IN_WIDTH) and IN_WIDTH % tn == 0
    k_at, v_at = divmod(COL_K, tn), divmod(COL_V, tn)
    assert k_at[1] + WIDTH_A <= tn and v_at[1] + WIDTH_A <= tn
    col_scale = jnp.concatenate([jnp.full((1, WIDTH_A), D_QK ** -0.5, F32),
                                 jnp.ones((1, IN_WIDTH - WIDTH_A), F32)], axis=1)
    f32_cols = pl.BlockSpec((tm * N_HEADS, D_HEAD), lambda i, j: (i, 0))
    return pl.pallas_call(
        functools.partial(_in_proj_kernel, k_at, v_at),
        grid=(t // tm, IN_WIDTH // tn),
        in_specs=[
            pl.BlockSpec((tm, D_MODEL), lambda i, j: (i, 0)),
            pl.BlockSpec((1, D_MODEL), lambda i, j: (0, 0)),
            pl.BlockSpec((D_MODEL, tn), lambda i, j: (0, j)),
            pl.BlockSpec((1, tn), lambda i, j: (0, j)),
        ],
        out_specs=[pl.BlockSpec((tm, tn), lambda i, j: (i, j)), f32_cols, f32_cols],
        out_shape=[
            jax.ShapeDtypeStruct((t, IN_WIDTH), BF16),
            jax.ShapeDtypeStruct((t * N_HEADS, D_HEAD), F32),
            jax.ShapeDtypeStruct((t * N_HEADS, D_HEAD), F32),
        ],
        scratch_shapes=[pltpu.VMEM((tm, D_MODEL), BF16)],
        compiler_params=_params(2),
        name="in_proj",
    )(x2d, norm_g.reshape(1, D_MODEL), w_in_bf, col_scale)


def _scores(qm, k):
    return lax.dot_general(qm, k, (((1,), (1,)), ((), ())), preferred_element_type=F32)


def _tile_lanes(x, n):
    return x if n == 1 else jnp.concatenate([x] * n, axis=1)


def _with_ones(v):
    return jnp.concatenate([v, jnp.ones_like(v)], axis=1)


def _softmax_step(s, v_ext, state, shift=None):
    m_ref, acc_ref = state
    keys = s.shape[1]
    assert keys % LANE == 0
    row_max = jnp.max(s, axis=-1, keepdims=True)
    if shift is not None:
        row_max = row_max + shift
    m_old = m_ref[...]
    m_new = jnp.maximum(m_old, row_max)
    sub = m_new if shift is None else m_new - shift
    p = jnp.exp(s - _tile_lanes(sub, keys // LANE))
    pv = jnp.dot(p.astype(BF16), v_ext, preferred_element_type=F32)
    alpha = jnp.exp(m_old - m_new)
    acc_ref[...] = _tile_lanes(alpha, 2 * D_HEAD // LANE) * acc_ref[...] + pv
    m_ref[...] = m_new


def _split_maps(q):
    lane = lax.broadcasted_iota(jnp.int32, q.shape, 1)
    zero = jnp.zeros_like(q)
    return jnp.where(lane < D_QK, q, zero), jnp.where(lane >= D_QK, q, zero)


def _lambda(lq1_ref, lk1_ref, lq2_ref, lk2_ref, lambda_init):
    s1 = jnp.sum(lq1_ref[...] * lk1_ref[...], axis=-1, keepdims=True)
    s2 = jnp.sum(lq2_ref[...] * lk2_ref[...], axis=-1, keepdims=True)
    return jnp.exp(s1) - jnp.exp(s2) + lambda_init


def _attn_finish(state1, state2, lam, g, lambda_init):
    a1 = state1[1][...]
    a2 = state2[1][...]
    o = a1[:, :D_HEAD] / a1[:, D_HEAD:] - lam * (a2[:, :D_HEAD] / a2[:, D_HEAD:])
    return _rms(o, g) * (1.0 - lambda_init)


def _attn_prompt_kernel(tq, nq, lambda_init, n_cast, table_ref, lq1_ref, lk1_ref, lq2_ref, lk2_ref, g_ref,
                        q_ref, k_ref, v_ref, bnear_ref, bdiag_ref, *rest):
    cast_in, o_ref, cast_out = rest[:n_cast], rest[n_cast], rest[n_cast + 1:2 * n_cast + 1]
    m1_ref, a1_ref, m2_ref, a2_ref, sa1_ref, sa2_ref, sb1_ref, sb2_ref = rest[2 * n_cast + 1:]
    _cast_slabs(cast_in, cast_out)
    h = pl.program_id(1)
    far_bias = table_ref[FAR_BUCKET, h]
    lam = _lambda(lq1_ref, lk1_ref, lq2_ref, lk2_ref, lambda_init)
    st1 = (m1_ref, a1_ref)
    st2 = (m2_ref, a2_ref)
    buf_a = (sa1_ref, sa2_ref)
    buf_b = (sb1_ref, sb2_ref)

    half = tq // 2
    assert half % CHUNK == 0 and half % LANE == 0

    def q_tile(qi, carry):
        q0 = pl.multiple_of(qi * tq, tq)
        q1, q2 = _split_maps(q_ref[pl.ds(q0, tq), :])

        def score_block(j, buf, diagonal=False):
            k = k_ref[pl.ds(pl.multiple_of(j * tq, tq), tq), :]
            for qm, s_ref in zip((q1, q2), buf):
                if diagonal:
                    s_ref[0:half, 0:half] = _scores(qm[0:half], k[0:half])
                    s_ref[half:tq, :] = _scores(qm[half:tq], k)
                else:
                    s_ref[...] = _scores(qm, k)

        def score_next_tile():
            qn1, qn2 = _split_maps(q_ref[pl.ds(pl.multiple_of(q0 + tq, tq), tq), :])
            k = k_ref[0:tq, :]
            buf_b[0][...] = _scores(qn1, k)
            buf_b[1][...] = _scores(qn2, k)

        def step(j, buf, nxt=None, bias=None, shift=None, nxt_diagonal=False):
            if nxt is not None:
                score_block(j + 1, nxt, diagonal=nxt_diagonal)
            v = _with_ones(v_ref[pl.ds(pl.multiple_of(j * tq, tq), tq), :])
            s1 = buf[0][...]
            s2 = buf[1][...]
            if bias is not None:
                s1 = s1 + bias
                s2 = s2 + bias
            _softmax_step(s1, v, st1, shift=shift)
            _softmax_step(s2, v, st2, shift=shift)

        def diagonal_step(next_tile=False):
            if next_tile:
                score_next_tile()
            v = _with_ones(v_ref[pl.ds(q0, tq), :])
            bias = bdiag_ref[...]
            for s_ref, (m_ref, a_ref) in zip(buf_a, (st1, st2)):
                top = s_ref[0:half, 0:half] + bias[0:half, 0:half]
                _softmax_step(top, v[0:half], (m_ref.at[0:half], a_ref.at[0:half]))
                bottom = s_ref[half:tq, :] + bias[half:tq, :]
                _softmax_step(bottom, v, (m_ref.at[half:tq], a_ref.at[half:tq]))

        for m_ref, a_ref in (st1, st2):
            m_ref[...] = jnp.full(m_ref.shape, NEG_INF, F32)
            a_ref[...] = jnp.zeros(a_ref.shape, F32)

        n_far = jnp.maximum(qi - 1, 0)
        odd_far = jnp.bitwise_and(n_far, 1)
        qi_even = jnp.bitwise_and(qi, 1) == 0

        @pl.when(qi == 0)
        def _():
            score_block(0, buf_a, diagonal=True)

        @pl.when(jnp.logical_and(qi_even, qi > 0))
        def _():
            score_block(0, buf_a)

        @pl.when(odd_far == 1)
        def _():
            step(0, buf_a, nxt=buf_b, shift=far_bias)

        def far_pair(t, c):
            j = odd_far + 2 * t
            step(j, buf_b, nxt=buf_a, shift=far_bias)
            step(j + 1, buf_a, nxt=buf_b, shift=far_bias)
            return c

        lax.fori_loop(0, lax.shift_right_logical(n_far, 1), far_pair, 0)

        @pl.when(qi > 0)
        def _():
            step(qi - 1, buf_b, nxt=buf_a, bias=bnear_ref[...], nxt_diagonal=True)

        feeds_next = jnp.logical_and(qi_even, qi + 1 < nq)

        @pl.when(feeds_next)
        def _():
            diagonal_step(next_tile=True)

        @pl.when(jnp.logical_not(feeds_next))
        def _():
            diagonal_step()

        o = _attn_finish(st1, st2, lam, g_ref[...], lambda_init)
        o_ref[pl.ds(q0, tq), :] = o.astype(BF16)
        return carry

    lax.fori_loop(0, nq, q_tile, 0)


def _attn_prompt(proj, rel_bias, lam_vecs, subln_g, bias_near, bias_diag, nb, seq, tq, lambda_init,
                 cast_weights=()):
    assert seq % tq == 0 and tq % CHUNK == 0 and tq >= FAR_DIST
    qkv3 = proj.reshape(nb, seq, IN_WIDTH)
    vec = pl.BlockSpec((1, D_QK), lambda b, h: (0, 0))
    head_cols = lambda off: pl.BlockSpec((None, seq, D_HEAD), lambda b, h: (b, 0, off + h))
    tile = pl.BlockSpec((None, tq, tq), lambda b, h: (h, 0, 0))
    stat = pltpu.VMEM((tq, LANE), F32)
    acc = pltpu.VMEM((tq, 2 * D_HEAD), F32)
    cast_in, cast_out, cast_shapes = _cast_slab_specs(cast_weights, nb * N_HEADS, lambda b, h: b * N_HEADS + h)
    o, *cast = pl.pallas_call(
        functools.partial(_attn_prompt_kernel, tq, seq // tq, lambda_init, len(cast_weights)),
        grid=(nb, N_HEADS),
        in_specs=[pl.BlockSpec(memory_space=pltpu.SMEM), vec, vec, vec, vec,
                  pl.BlockSpec((1, D_HEAD), lambda b, h: (0, 0)),
                  head_cols(0), head_cols(COL_K // D_HEAD), head_cols(COL_V // D_HEAD), tile, tile] + cast_in,
        out_specs=[pl.BlockSpec((None, seq, D_HEAD), lambda b, h: (b, 0, h))] + cast_out,
        out_shape=[jax.ShapeDtypeStruct((nb, seq, WIDTH_A), BF16)] + cast_shapes,
        scratch_shapes=[stat, acc, stat, acc] + [pltpu.VMEM((tq, tq), F32)] * 4,
        compiler_params=_params(2),
        name="attn_prompt",
    )(rel_bias, *lam_vecs, subln_g.reshape(1, D_HEAD), qkv3, qkv3, qkv3, bias_near, bias_diag, *cast_weights)
    return o.reshape(nb * seq, WIDTH_A), cast


def _softmax_two_blocks(qm, k_a, v_a, bias_a, k_b, v_b, bias_b):
    s_a = _scores(qm, k_a) + bias_a
    s_b = _scores(qm, k_b) + bias_b
    m = jnp.maximum(jnp.max(s_a, axis=-1, keepdims=True), jnp.max(s_b, axis=-1, keepdims=True))
    p_a = jnp.exp(s_a - m).astype(BF16)
    p_b = jnp.exp(s_b - m).astype(BF16)
    return jnp.dot(p_a, v_a, preferred_element_type=F32) + jnp.dot(p_b, v_b, preferred_element_type=F32)


def _attn_sample_kernel(lambda_init, past, lq1_ref, lk1_ref, lq2_ref, lk2_ref, g_ref,
                        qkv_ref, kc_ref, vc_ref, bc_ref, bn_ref, o_ref):
    lam = _lambda(lq1_ref, lk1_ref, lq2_ref, lk2_ref, lambda_init)
    for h in range(N_HEADS):
        cols = lambda part: slice((part * N_HEADS + h) * D_HEAD, (part * N_HEADS + h + 1) * D_HEAD)
        q1, q2 = _split_maps(qkv_ref[:, cols(0)])
        kn = qkv_ref[:, cols(1)]
        vn = _with_ones(qkv_ref[:, cols(2)])
        kc = kc_ref[pl.ds(h, past, stride=N_HEADS), :].astype(BF16)
        vc = _with_ones(vc_ref[pl.ds(h, past, stride=N_HEADS), :].astype(BF16))
        a1 = _softmax_two_blocks(q1, kc, vc, bc_ref[h], kn, vn, bn_ref[h])
        a2 = _softmax_two_blocks(q2, kc, vc, bc_ref[h], kn, vn, bn_ref[h])
        o = a1[:, :D_HEAD] / a1[:, D_HEAD:] - lam * (a2[:, :D_HEAD] / a2[:, D_HEAD:])
        o_ref[:, h * D_HEAD:(h + 1) * D_HEAD] = (_rms(o, g_ref[...]) * (1.0 - lambda_init)).astype(BF16)


def _attn_sample(proj, cache_k, cache_v, layer, lam_vecs, subln_g, bias_cache, bias_new, nb, seq, past,
                 lambda_init):
    qkv3 = proj.reshape(nb, seq, IN_WIDTH)
    kc = cache_k.reshape(-1, past * N_HEADS, D_HEAD)
    vc = cache_v.reshape(-1, past * N_HEADS, D_HEAD)
    vec = pl.BlockSpec((1, D_QK), lambda b: (0, 0))
    past_rows = pl.BlockSpec((None, past * N_HEADS, D_HEAD), lambda b: (layer * nb + b, 0, 0))
    full = lambda a: pl.BlockSpec(a.shape, lambda b: (0,) * a.ndim)
    o = pl.pallas_call(
        functools.partial(_attn_sample_kernel, lambda_init, past),
        grid=(nb,),
        in_specs=[vec, vec, vec, vec, pl.BlockSpec((1, D_HEAD), lambda b: (0, 0)),
                  pl.BlockSpec((None, seq, 3 * WIDTH_A), lambda b: (b, 0, 0)), past_rows, past_rows,
                  full(bias_cache), full(bias_new)],
        out_specs=pl.BlockSpec((None, seq, WIDTH_A), lambda b: (b, 0, 0)),
        out_shape=jax.ShapeDtypeStruct((nb, seq, WIDTH_A), BF16),
        compiler_params=_params(1),
        name="attn_sample",
    )(*lam_vecs, subln_g.reshape(1, D_HEAD), qkv3, kc, vc, bias_cache, bias_new)
    return o.reshape(nb * seq, WIDTH_A)


def _causal_conv3(x3, prev8, w):
    p0 = prev8[:, SUBLANE - 2:SUBLANE - 1, :]
    p1 = prev8[:, SUBLANE - 1:SUBLANE, :]
    row = lax.broadcasted_iota(jnp.int32, x3.shape, 1)
    x1 = jnp.where(row == 0, p1, pltpu.roll(x3, 1, axis=1))
    x2 = jnp.where(row == 0, p0, jnp.where(row == 1, p1, pltpu.roll(x3, 2, axis=1)))
    return x2 * w[0:1, :] + x1 * w[1:2, :] + x3 * w[2:3, :]


def _conv_with_state(x, i, nseq, tps, st_ref, so_ref, halo):
    rows = x.shape[0] // nseq
    x3 = x.reshape(nseq, rows, x.shape[1])
    last8 = x3[:, rows - SUBLANE:, :]
    if tps == 1:
        prev8 = st_ref[...]
    else:
        @pl.when(i % tps == 0)
        def _():
            halo[...] = st_ref[...]

        prev8 = halo[...]
        halo[...] = last8
    so_ref[...] = last8
    return x3, prev8


def _merge_kernel(nseq, tps, n_cast, o_ref, b_ref, c_ref, xi_ref, ga_ref, gb_ref, x_ref, st_ref, cw_ref,
                  wa_ref, wb_ref, wo_ref, g2_ref, *rest):
    cast_in, (h_ref, hn_ref, so_ref) = rest[:n_cast], rest[n_cast:n_cast + 3]
    cast_out, scratch = rest[n_cast + 3:2 * n_cast + 3], rest[2 * n_cast + 3:]
    _cast_slabs(cast_in, cast_out)
    i = pl.program_id(0)
    u = c_ref[...].astype(F32) * xi_ref[...].astype(F32)
    u3, prev8 = _conv_with_state(u, i, nseq, tps, st_ref, so_ref, scratch[0] if tps > 1 else None)
    z = _causal_conv3(u3, prev8, cw_ref[...]).reshape(u.shape)
    ob = (b_ref[...].astype(F32) * z).astype(BF16)
    ya = jnp.dot(o_ref[...], wa_ref[...], preferred_element_type=F32)
    yb = jnp.dot(ob, wb_ref[...], preferred_element_type=F32)
    merged = jax.nn.sigmoid(ga_ref[...].astype(F32)) * ya + jax.nn.sigmoid(gb_ref[...].astype(F32)) * yb
    hres = x_ref[...] + jnp.dot(merged.astype(BF16), wo_ref[...], preferred_element_type=F32)
    h_ref[...] = hres
    hn_ref[...] = _rms(hres, g2_ref[...]).astype(BF16)


def _merge(o, proj, x2d, state8, conv_w, wa, wb, wo, norm2_g, nb, seq, rows, cast_weights=()):
    t = x2d.shape[0]
    nseq = max(rows // seq, 1)
    tps = max(seq // rows, 1)
    assert t % rows == 0 and (rows % seq == 0 or seq % rows == 0)
    tok = lambda width, col: pl.BlockSpec((rows, width), lambda i: (i, col))
    full = lambda a: pl.BlockSpec(a.shape, lambda i: (0,) * a.ndim)
    state_spec = pl.BlockSpec((nseq, SUBLANE, WIDTH_B), lambda i: (i // tps, 0, 0))
    scratch = [pltpu.VMEM((nseq, SUBLANE, WIDTH_B), F32)] if tps > 1 else []
    g2 = norm2_g.reshape(1, D_MODEL)
    cast_in, cast_out, cast_shapes = _cast_slab_specs(cast_weights, t // rows, lambda i: i)
    hres, hn, state_rows, *cast = pl.pallas_call(
        functools.partial(_merge_kernel, nseq, tps, len(cast_weights)),
        grid=(t // rows,),
        in_specs=[tok(WIDTH_A, 0), tok(WIDTH_B, COL_B // WIDTH_B), tok(WIDTH_B, COL_C // WIDTH_B),
                  tok(WIDTH_B, COL_X // WIDTH_B), tok(D_MODEL, COL_GA // D_MODEL), tok(D_MODEL, COL_GB // D_MODEL),
                  tok(D_MODEL, 0), state_spec,
                  full(conv_w), full(wa), full(wb), full(wo), full(g2)] + cast_in,
        out_specs=[tok(D_MODEL, 0), tok(D_MODEL, 0), state_spec] + cast_out,
        out_shape=[jax.ShapeDtypeStruct((t, D_MODEL), F32),
                   jax.ShapeDtypeStruct((t, D_MODEL), BF16),
                   jax.ShapeDtypeStruct((nb, SUBLANE, WIDTH_B), F32)] + cast_shapes,
        scratch_shapes=scratch,
        compiler_params=_params(1),
        name="merge",
    )(o, proj, proj, proj, proj, proj, x2d, state8, conv_w, wa, wb, wo, g2, *cast_weights)
    return hres, hn, state_rows, cast


FFN_TILE_F = 512
FFN_SUB_TILES = 2


def _ffn_kernel(nseq, tps, nf, hn_ref, h_ref, wg_ref, wv_ref, cwg_ref, cwv_ref, stg_ref, stv_ref,
                wd_ref, fg_ref, y_ref, sog_ref, sov_ref, act_ref, *scratch):
    s = pl.program_id(0)
    i = lax.div(s, nf)
    f = lax.rem(s, nf)
    f_down = lax.rem(jnp.maximum(s - 1, 0), nf)
    act_new = act_ref.at[lax.rem(s, 2)]
    act_old = act_ref.at[1 - lax.rem(s, 2)]
    rows = hn_ref.shape[0]
    sub = rows // FFN_SUB_TILES
    branches = ((wg_ref, cwg_ref, stg_ref, sog_ref), (wv_ref, cwv_ref, stv_ref, sov_ref))

    @pl.when(s == 0)
    def _():
        act_ref[...] = jnp.zeros_like(act_ref)

    @pl.when(f_down == 0)
    def _():
        y_ref[...] = jnp.zeros_like(y_ref)

    if tps > 1:
        halos = [scratch[0].at[2 * f + slot] for slot in range(2)]

        @pl.when(lax.rem(i, tps) == 0)
        def _():
            for halo, (_, _, st_ref, _) in zip(halos, branches):
                halo[...] = st_ref[...]

        prev = [halo[...] for halo in halos]
    else:
        assert sub % (rows // nseq) == 0
        seqs = nseq // FFN_SUB_TILES

    for r in range(FFN_SUB_TILES):
        tile_rows = slice(r * sub, (r + 1) * sub)
        y_ref[tile_rows, :] += jnp.dot(act_old[tile_rows, :], wd_ref[...], preferred_element_type=F32)
        hn = hn_ref[tile_rows, :]
        convs = []
        for slot, (w_ref, cw_ref, st_ref, so_ref) in enumerate(branches):
            up = jnp.dot(hn, w_ref[...], preferred_element_type=F32)
            if tps > 1:
                up3 = up.reshape(1, sub, up.shape[1])
                prev8 = prev[slot]
                prev[slot] = up3[:, sub - SUBLANE:, :]
            else:
                up3 = up.reshape(seqs, sub // seqs, up.shape[1])
                prev8 = st_ref[r * seqs:(r + 1) * seqs]
                so_ref[r * seqs:(r + 1) * seqs] = up3[:, sub // seqs - SUBLANE:, :]
            convs.append(_causal_conv3(up3, prev8, cw_ref[...]).reshape(up.shape))
        act_new[tile_rows, :] = (jax.nn.silu(convs[0]) * convs[1]).astype(BF16)

    if tps > 1:
        for halo, last8, (_, _, _, so_ref) in zip(halos, prev, branches):
            halo[...] = last8
            so_ref[...] = last8

    @pl.when(jnp.logical_and(f_down == nf - 1, s > 0))
    def _():
        y_ref[...] = _rms(h_ref[...] + y_ref[...], fg_ref[...])


def _ffn(hn, hres, w_up, ffn_conv_w, state8, w_down, final_g, nb, seq, rows):
    t = hn.shape[0]
    tf = FFN_TILE_F
    nf = D_FF // tf
    nseq = max(rows // seq, 1)
    tps = max(seq // rows, 1)
    assert D_FF % tf == 0 and t % rows == 0 and (rows % seq == 0 or seq % rows == 0)
    n_tiles = t // rows
    up_i = lambda s: jnp.minimum(s // nf, n_tiles - 1)
    up_f = lambda s: s % nf
    down_i = lambda s: jnp.maximum(s - 1, 0) // nf
    down_f = lambda s: jnp.maximum(s - 1, 0) % nf
    gate_cols = lambda r: pl.BlockSpec((r, tf), lambda s: (0, up_f(s)))
    val_cols = lambda r: pl.BlockSpec((r, tf), lambda s: (0, nf + up_f(s)))
    st_gate = pl.BlockSpec((nseq, SUBLANE, tf), lambda s: (up_i(s) // tps, 0, up_f(s)))
    st_val = pl.BlockSpec((nseq, SUBLANE, tf), lambda s: (up_i(s) // tps, 0, nf + up_f(s)))
    resid = pl.BlockSpec((rows, D_MODEL), lambda s: (down_i(s), 0), pipeline_mode=pl.Buffered(1))
    scratch = [pltpu.VMEM((2, rows, tf), BF16)]
    if tps > 1:
        scratch.append(pltpu.VMEM((2 * nf, nseq, SUBLANE, tf), F32))
    tile_tail = pl.BlockSpec((nseq, SUBLANE, tf), lambda s: (s // nf, 0, up_f(s)))
    fg = final_g.reshape(1, D_MODEL)
    y, tail_g, tail_v = pl.pallas_call(
        functools.partial(_ffn_kernel, nseq, tps, nf),
        grid=(n_tiles * nf + 1,),
        in_specs=[pl.BlockSpec((rows, D_MODEL), lambda s: (up_i(s), 0)), resid,
                  gate_cols(D_MODEL), val_cols(D_MODEL), gate_cols(CONV_W), val_cols(CONV_W),
                  st_gate, st_val,
                  pl.BlockSpec((tf, D_MODEL), lambda s: (down_f(s), 0)),
                  pl.BlockSpec((1, D_MODEL), lambda s: (0, 0))],
        out_specs=[pl.BlockSpec((rows, D_MODEL), lambda s: (down_i(s), 0)), tile_tail, tile_tail],
        out_shape=[jax.ShapeDtypeStruct((t, D_MODEL), F32),
                   jax.ShapeDtypeStruct(((n_tiles + 1) * nseq, SUBLANE, D_FF), F32),
                   jax.ShapeDtypeStruct(((n_tiles + 1) * nseq, SUBLANE, D_FF), F32)],
        scratch_shapes=scratch,
        compiler_params=_params(1),
        name="ffn",
    )(hn, hres, w_up, w_up, ffn_conv_w, ffn_conv_w, state8, state8, w_down, fg)
    last_of_seq = slice(tps - 1, n_tiles * nseq, tps)
    return y, tail_g[last_of_seq], tail_v[last_of_seq]


def _state8(state):
    return jnp.pad(state, ((0, 0), (SUBLANE - (CONV_W - 1), 0), (0, 0)))


def _state_out(state8):
    return state8[:, SUBLANE - (CONV_W - 1):, :]


def _layer(x2d, nb, seq, attn_fn, conv8, ffn8, lp, tiles, cast_in_attn=(), cast_in_merge=()):
    lp = dict(lp)
    proj, k_f32, v_f32 = _in_proj(x2d, lp["norm1_g"], lp["w_in"], tiles["in"])
    o, done = attn_fn(proj, [lp[name] for name in cast_in_attn])
    lp.update(zip(cast_in_attn, done))
    hres, hn, conv_out8, done = _merge(o, proj, x2d, conv8, lp["conv_w"], lp["w_proj_a"], lp["w_proj_b"],
                                       lp["w_o"], lp["norm2_g"], nb, seq, tiles["merge"],
                                       [lp[name] for name in cast_in_merge])
    lp.update(zip(cast_in_merge, done))
    y, ffn_g8, ffn_v8 = _ffn(hn, hres, lp["w_up"], lp["ffn_conv_w"], ffn8, lp["w_down"], lp["final_g"],
                             nb, seq, tiles["ffn"])
    k = k_f32.reshape(nb, seq, N_HEADS, D_HEAD)
    v = v_f32.reshape(nb, seq, N_HEADS, D_HEAD)
    conv_state = _state_out(conv_out8)
    ffn_state = jnp.concatenate([_state_out(ffn_g8), _state_out(ffn_v8)], axis=-1)
    return (y, k, v, conv_state, ffn_state), lp


PROMPT_TILES = {"in": 512, "merge": 256, "ffn": 1024}
SAMPLE_TILES = {"in": 512, "merge": 256, "ffn": 1024}
ATTN_TILE_Q = 512


def kernel(x_prompt, x_sample, cache_k, cache_v, state_conv_mix, state_conv_ffn, rel_bias, norm1_g, w_in,
           lambda_q1, lambda_k1, lambda_q2, lambda_k2, subln_g, conv_w, w_proj_a, w_proj_b, w_o, norm2_g,
           w_up, ffn_conv_w, w_down, final_g):
    depth = w_in.shape[0]
    nbp, seq_p, _ = x_prompt.shape
    nbs, seq_s, _ = x_sample.shape
    past = cache_k.shape[2]
    tq = ATTN_TILE_Q

    assert depth == 1, "the ffn kernel ends with the final RMSNorm, so it is written for a single layer"
    (bias_near, bias_diag, bias_cache, bias_new), (w_in_bf,) = _bias_tiles(
        rel_bias,
        ((tq, tq, tq, 0), (tq, tq, tq, tq), (seq_s, past, past, 0), (seq_s, seq_s, past, past)),
        cast_weights=(w_in[0],))

    hp = x_prompt.reshape(nbp * seq_p, D_MODEL)
    hs = x_sample.reshape(nbs * seq_s, D_MODEL)
    outs_p, outs_s = [], []
    for d in range(depth):
        lp = {
            "norm1_g": norm1_g[d], "w_in": w_in_bf, "conv_w": conv_w[d],
            "w_proj_a": w_proj_a[d], "w_proj_b": w_proj_b[d], "w_o": w_o[d], "norm2_g": norm2_g[d],
            "w_up": w_up[d], "ffn_conv_w": ffn_conv_w[d], "w_down": w_down[d], "final_g": final_g,
        }
        li = _lambda_init(d)
        lam_vecs = tuple(a[d].reshape(1, D_QK) for a in (lambda_q1, lambda_k1, lambda_q2, lambda_k2))

        attn_p = lambda qkv, cast: _attn_prompt(qkv, rel_bias, lam_vecs, subln_g[d], bias_near, bias_diag,
                                                nbp, seq_p, tq, li, cast)
        zeros_mix = jnp.zeros((nbp, SUBLANE, WIDTH_B), F32)
        zeros_ffn = jnp.zeros((nbp, SUBLANE, 2 * D_FF), F32)
        (hp, kp, vp, cmp_, cfp), lp = _layer(
            hp, nbp, seq_p, attn_p, zeros_mix, zeros_ffn, lp, PROMPT_TILES,
            cast_in_attn=("w_proj_a", "w_proj_b", "w_o", "w_down"), cast_in_merge=("w_up",))

        attn_s = lambda qkv, cast: (_attn_sample(qkv, cache_k, cache_v, d, lam_vecs, subln_g[d], bias_cache,
                                                 bias_new, nbs, seq_s, past, li), ())
        (hs, ks, vs, cms, cfs), _ = _layer(hs, nbs, seq_s, attn_s, _state8(state_conv_mix[d]),
                                           _state8(state_conv_ffn[d]), lp, SAMPLE_TILES)
        outs_p.append((kp, vp, cmp_, cfp))
        outs_s.append((ks, vs, cms, cfs))

    stack = lambda outs, idx: jnp.stack([o[idx] for o in outs])
    return (hp.reshape(nbp, seq_p, D_MODEL), hs.reshape(nbs, seq_s, D_MODEL),
            stack(outs_p, 0), stack(outs_p, 1), stack(outs_p, 2), stack(outs_p, 3),
            stack(outs_s, 0), stack(outs_s, 1), stack(outs_s, 2), stack(outs_s, 3))
```

```python
import functools
import math

import jax
import jax.numpy as jnp
from jax import lax
from jax.experimental import pallas as pl
from jax.experimental.pallas import tpu as pltpu

D_MODEL = 2048
N_HEADS = 8
D_QK = 64
D_HEAD = 2 * D_QK
WIDTH_A = N_HEADS * D_HEAD
WIDTH_B = D_MODEL // 2
D_FF = 5632
CONV_W = 3
CHUNK = 64
N_BUCKETS = 32
MAX_DIST = 128
EPS = 1e-6
NEG_INF = -1e30

SUBLANE = 8
LANE = 128
VMEM_LIMIT_BYTES = 60 * 1024 * 1024

F32 = jnp.float32
BF16 = jnp.bfloat16


def _bucket_thresholds():
    nb = N_BUCKETS // 2
    max_exact = nb // 2
    steps = nb - max_exact
    out = []
    for k in range(1, steps):
        n = max_exact
        while n ** steps * max_exact ** k < MAX_DIST ** k * max_exact ** steps:
            n += 1
        out.append(n)
    return tuple(out)


_BUCKET_THRESHOLDS = _bucket_thresholds()
FAR_DIST = _BUCKET_THRESHOLDS[-1]
FAR_BUCKET = N_BUCKETS // 2 - 1
_CHUNK_SHIFT = CHUNK.bit_length() - 1
assert 1 << _CHUNK_SHIFT == CHUNK


def _bucket_of(rel):
    nb = N_BUCKETS // 2
    max_exact = nb // 2
    n = abs(rel)
    large = max_exact + sum(n >= thr for thr in _BUCKET_THRESHOLDS)
    return (nb if rel > 0 else 0) + (n if n < max_exact else large)


def _lambda_init(layer_idx):
    return 0.8 - 0.6 * math.exp(-0.3 * layer_idx)


def _rms(x, g):
    return x * lax.rsqrt(jnp.mean(x * x, axis=-1, keepdims=True) + EPS) * g


def _params(n_axes):
    return pltpu.CompilerParams(
        dimension_semantics=("arbitrary",) * n_axes, vmem_limit_bytes=VMEM_LIMIT_BYTES)


def _cast_slabs(in_refs, out_refs):
    for i_ref, o_ref in zip(in_refs, out_refs):
        o_ref[...] = i_ref[...].astype(BF16)


def _cast_slab_specs(weights, n_steps, step_of):
    specs, shapes = [], []
    for w in weights:
        rows, cols = w.shape
        assert rows % n_steps == 0 and (rows // n_steps) % (2 * SUBLANE) == 0 and cols % LANE == 0
        specs.append(pl.BlockSpec((rows // n_steps, cols), lambda *idx: (step_of(*idx), 0)))
        shapes.append(jax.ShapeDtypeStruct(w.shape, BF16))
    return specs, list(specs), shapes


def _bias_tile_kernel(specs, table_ref, *refs):
    n_cast = (len(refs) - len(specs)) // 2
    cast_in, out_refs, cast_out = refs[:n_cast], refs[n_cast:n_cast + len(specs)], refs[n_cast + len(specs):]
    _cast_slabs(cast_in, cast_out)
    h = pl.program_id(0)
    nb = N_BUCKETS // 2
    max_exact = nb // 2
    for (rows, cols, q0, k0), out_ref in zip(specs, out_refs):
        rel0 = k0 - q0 - (rows - SUBLANE)
        width = -(-(cols + rows - SUBLANE) // LANE) * LANE
        rel = (lax.broadcasted_iota(jnp.int32, (SUBLANE, width), 1)
               - lax.broadcasted_iota(jnp.int32, (SUBLANE, width), 0) + rel0)
        n = jnp.abs(rel)
        large = jnp.full((SUBLANE, width), max_exact, jnp.int32)
        for thr in _BUCKET_THRESHOLDS:
            large = large + jnp.where(n >= thr, 1, 0)
        bucket = jnp.where(rel > 0, nb, 0) + jnp.where(n < max_exact, n, large)
        strip = jnp.zeros((SUBLANE, width), F32)
        rel_hi = min(rel0 + width - 1, CHUNK - 1)
        for b in sorted({_bucket_of(r) for r in range(rel0 - SUBLANE + 1, rel_hi + 1)}):
            strip = jnp.where(bucket == b, table_ref[b, h], strip)
        assert q0 % SUBLANE == 0 and CHUNK % SUBLANE == 0
        k_chunk = lax.shift_right_logical(k0 + lax.broadcasted_iota(jnp.int32, (SUBLANE, cols), 1), _CHUNK_SHIFT)
        for a in range(rows // SUBLANE):
            start = rows - SUBLANE * (a + 1)
            visible = k_chunk <= (q0 + SUBLANE * a) // CHUNK
            out_ref[SUBLANE * a:SUBLANE * (a + 1), :] = jnp.where(visible, strip[:, start:start + cols], NEG_INF)


def _bias_tiles(rel_bias, specs, cast_weights=()):
    cast_in, cast_out, cast_shapes = _cast_slab_specs(cast_weights, N_HEADS, lambda h: h)
    outs = pl.pallas_call(
        functools.partial(_bias_tile_kernel, specs),
        grid=(N_HEADS,),
        in_specs=[pl.BlockSpec(memory_space=pltpu.SMEM)] + cast_in,
        out_specs=[pl.BlockSpec((None, r, c), lambda h: (h, 0, 0)) for r, c, _, _ in specs] + cast_out,
        out_shape=[jax.ShapeDtypeStruct((N_HEADS, r, c), F32) for r, c, _, _ in specs] + cast_shapes,
        compiler_params=_params(1),
        name="bias_tiles",
    )(rel_bias, *cast_weights)
    return outs[:len(specs)], outs[len(specs):]


IN_TILE_N = 2048
IN_WIDTH = 3 * WIDTH_A + 3 * WIDTH_B + 2 * D_MODEL
COL_K = WIDTH_A
COL_V = 2 * WIDTH_A
COL_B = 3 * WIDTH_A
COL_C = COL_B + WIDTH_B
COL_X = COL_C + WIDTH_B
COL_GA = COL_X + WIDTH_B
COL_GB = COL_GA + D_MODEL


def _in_proj_kernel(k_at, v_at, x_ref, g_ref, w_ref, cs_ref, proj_ref, kf_ref, vf_ref, xn_ref):
    j = pl.program_id(1)

    @pl.when(j == 0)
    def _():
        xn_ref[...] = _rms(x_ref[...], g_ref[...]).astype(BF16)

    y = jnp.dot(xn_ref[...], w_ref[...], preferred_element_type=F32)
    proj_ref[...] = (y * cs_ref[...]).astype(BF16)

    for (tile, off), f32_ref in ((k_at, kf_ref), (v_at, vf_ref)):
        @pl.when(j == tile)
        def _():
            for h in range(N_HEADS):
                f32_ref[pl.ds(h, y.shape[0], stride=N_HEADS), :] = y[:, off + h * D_HEAD:off + (h + 1) * D_HEAD]


def _in_proj(x2d, norm_g, w_in_bf, tm):
    t = x2d.shape[0]
    tn = IN_TILE_N
    assert t % tm == 0 and w_in_bf.shape == (D_MODEL, IN_WIDTH) and IN_WIDTH % tn == 0
    k_at, v_at = divmod(COL_K, tn), divmod(COL_V, tn)
    assert k_at[1] + WIDTH_A <= tn and v_at[1] + WIDTH_A <= tn
    col_scale = jnp.concatenate([jnp.full((1, WIDTH_A), D_QK ** -0.5, F32),
                                 jnp.ones((1, IN_WIDTH - WIDTH_A), F32)], axis=1)
    f32_cols = pl.BlockSpec((tm * N_HEADS, D_HEAD), lambda i, j: (i, 0))
    return pl.pallas_call(
        functools.partial(_in_proj_kernel, k_at, v_at),
        grid=(t // tm, IN_WIDTH // tn),
        in_specs=[
            pl.BlockSpec((tm, D_MODEL), lambda i, j: (i, 0)),
            pl.BlockSpec((1, D_MODEL), lambda i, j: (0, 0)),
            pl.BlockSpec((D_MODEL, tn), lambda i, j: (0, j)),
            pl.BlockSpec((1, tn), lambda i, j: (0, j)),
        ],
        out_specs=[pl.BlockSpec((tm, tn), lambda i, j: (i, j)), f32_cols, f32_cols],
        out_shape=[
            jax.ShapeDtypeStruct((t, IN_WIDTH), BF16),
            jax.ShapeDtypeStruct((t * N_HEADS, D_HEAD), F32),
            jax.ShapeDtypeStruct((t * N_HEADS, D_HEAD), F32),
        ],
        scratch_shapes=[pltpu.VMEM((tm, D_MODEL), BF16)],
        compiler_params=_params(2),
        name="in_proj",
    )(x2d, norm_g.reshape(1, D_MODEL), w_in_bf, col_scale)


def _scores(qm, k):
    return lax.dot_general(qm, k, (((1,), (1,)), ((), ())), preferred_element_type=F32)


def _tile_lanes(x, n):
    return x if n == 1 else jnp.concatenate([x] * n, axis=1)


def _with_ones(v):
    return jnp.concatenate([v, jnp.ones_like(v)], axis=1)


def _softmax_step(s, v_ext, state, shift=None):
    m_ref, acc_ref = state
    keys = s.shape[1]
    assert keys % LANE == 0
    row_max = jnp.max(s, axis=-1, keepdims=True)
    if shift is not None:
        row_max = row_max + shift
    m_old = m_ref[...]
    m_new = jnp.maximum(m_old, row_max)
    sub = m_new if shift is None else m_new - shift
    p = jnp.exp(s - _tile_lanes(sub, keys // LANE))
    pv = jnp.dot(p.astype(BF16), v_ext, preferred_element_type=F32)
    alpha = jnp.exp(m_old - m_new)
    acc_ref[...] = _tile_lanes(alpha, 2 * D_HEAD // LANE) * acc_ref[...] + pv
    m_ref[...] = m_new


def _split_maps(q):
    lane = lax.broadcasted_iota(jnp.int32, q.shape, 1)
    zero = jnp.zeros_like(q)
    return jnp.where(lane < D_QK, q, zero), jnp.where(lane >= D_QK, q, zero)


def _lambda(lq1_ref, lk1_ref, lq2_ref, lk2_ref, lambda_init):
    s1 = jnp.sum(lq1_ref[...] * lk1_ref[...], axis=-1, keepdims=True)
    s2 = jnp.sum(lq2_ref[...] * lk2_ref[...], axis=-1, keepdims=True)
    return jnp.exp(s1) - jnp.exp(s2) + lambda_init


def _attn_finish(state1, state2, lam, g, lambda_init):
    a1 = state1[1][...]
    a2 = state2[1][...]
    o = a1[:, :D_HEAD] / a1[:, D_HEAD:] - lam * (a2[:, :D_HEAD] / a2[:, D_HEAD:])
    return _rms(o, g) * (1.0 - lambda_init)


def _attn_prompt_kernel(tq, nq, lambda_init, n_cast, table_ref, lq1_ref, lk1_ref, lq2_ref, lk2_ref, g_ref,
                        q_ref, k_ref, v_ref, bnear_ref, bdiag_ref, *rest):
    cast_in, o_ref, cast_out = rest[:n_cast], rest[n_cast], rest[n_cast + 1:2 * n_cast + 1]
    m1_ref, a1_ref, m2_ref, a2_ref, sa1_ref, sa2_ref, sb1_ref, sb2_ref = rest[2 * n_cast + 1:]
    _cast_slabs(cast_in, cast_out)
    h = pl.program_id(1)
    far_bias = table_ref[FAR_BUCKET, h]
    lam = _lambda(lq1_ref, lk1_ref, lq2_ref, lk2_ref, lambda_init)
    st1 = (m1_ref, a1_ref)
    st2 = (m2_ref, a2_ref)
    buf_a = (sa1_ref, sa2_ref)
    buf_b = (sb1_ref, sb2_ref)

    half = tq // 2
    assert half % CHUNK == 0 and half % LANE == 0

    def q_tile(qi, carry):
        q0 = pl.multiple_of(qi * tq, tq)
        q1, q2 = _split_maps(q_ref[pl.ds(q0, tq), :])

        def score_block(j, buf, diagonal=False):
            k = k_ref[pl.ds(pl.multiple_of(j * tq, tq), tq), :]
            for qm, s_ref in zip((q1, q2), buf):
                if diagonal:
                    s_ref[0:half, 0:half] = _scores(qm[0:half], k[0:half])
                    s_ref[half:tq, :] = _scores(qm[half:tq], k)
                else:
                    s_ref[...] = _scores(qm, k)

        def score_next_tile():
            qn1, qn2 = _split_maps(q_ref[pl.ds(pl.multiple_of(q0 + tq, tq), tq), :])
            k = k_ref[0:tq, :]
            buf_b[0][...] = _scores(qn1, k)
            buf_b[1][...] = _scores(qn2, k)

        def step(j, buf, nxt=None, bias=None, shift=None, nxt_diagonal=False):
            if nxt is not None:
                score_block(j + 1, nxt, diagonal=nxt_diagonal)
            v = _with_ones(v_ref[pl.ds(pl.multiple_of(j * tq, tq), tq), :])
            s1 = buf[0][...]
            s2 = buf[1][...]
            if bias is not None:
                s1 = s1 + bias
                s2 = s2 + bias
            _softmax_step(s1, v, st1, shift=shift)
            _softmax_step(s2, v, st2, shift=shift)

        def diagonal_step(next_tile=False):
            if next_tile:
                score_next_tile()
            v = _with_ones(v_ref[pl.ds(q0, tq), :])
            bias = bdiag_ref[...]
            for s_ref, (m_ref, a_ref) in zip(buf_a, (st1, st2)):
                top = s_ref[0:half, 0:half] + bias[0:half, 0:half]
                _softmax_step(top, v[0:half], (m_ref.at[0:half], a_ref.at[0:half]))
                bottom = s_ref[half:tq, :] + bias[half:tq, :]
                _softmax_step(bottom, v, (m_ref.at[half:tq], a_ref.at[half:tq]))

        for m_ref, a_ref in (st1, st2):
            m_ref[...] = jnp.full(m_ref.shape, NEG_INF, F32)
            a_ref[...] = jnp.zeros(a_ref.shape, F32)

        n_far = jnp.maximum(qi - 1, 0)
        odd_far = jnp.bitwise_and(n_far, 1)
        qi_even = jnp.bitwise_and(qi, 1) == 0

        @pl.when(qi == 0)
        def _():
            score_block(0, buf_a, diagonal=True)

        @pl.when(jnp.logical_and(qi_even, qi > 0))
        def _():
            score_block(0, buf_a)

        @pl.when(odd_far == 1)
        def _():
            step(0, buf_a, nxt=buf_b, shift=far_bias)

        def far_pair(t, c):
            j = odd_far + 2 * t
            step(j, buf_b, nxt=buf_a, shift=far_bias)
            step(j + 1, buf_a, nxt=buf_b, shift=far_bias)
            return c

        lax.fori_loop(0, lax.shift_right_logical(n_far, 1), far_pair, 0)

        @pl.when(qi > 0)
        def _():
            step(qi - 1, buf_b, nxt=buf_a, bias=bnear_ref[...], nxt_diagonal=True)

        feeds_next = jnp.logical_and(qi_even, qi + 1 < nq)

        @pl.when(feeds_next)
        def _():
            diagonal_step(next_tile=True)

        @pl.when(jnp.logical_not(feeds_next))
        def _():
            diagonal_step()

        o = _attn_finish(st1, st2, lam, g_ref[...], lambda_init)
        o_ref[pl.ds(q0, tq), :] = o.astype(BF16)
        return carry

    lax.fori_loop(0, nq, q_tile, 0)


def _attn_prompt(proj, rel_bias, lam_vecs, subln_g, bias_near, bias_diag, nb, seq, tq, lambda_init,
                 cast_weights=()):
    assert seq % tq == 0 and tq % CHUNK == 0 and tq >= FAR_DIST
    qkv3 = proj.reshape(nb, seq, IN_WIDTH)
    vec = pl.BlockSpec((1, D_QK), lambda b, h: (0, 0))
    head_cols = lambda off: pl.BlockSpec((None, seq, D_HEAD), lambda b, h: (b, 0, off + h))
    tile = pl.BlockSpec((None, tq, tq), lambda b, h: (h, 0, 0))
    stat = pltpu.VMEM((tq, LANE), F32)
    acc = pltpu.VMEM((tq, 2 * D_HEAD), F32)
    cast_in, cast_out, cast_shapes = _cast_slab_specs(cast_weights, nb * N_HEADS, lambda b, h: b * N_HEADS + h)
    o, *cast = pl.pallas_call(
        functools.partial(_attn_prompt_kernel, tq, seq // tq, lambda_init, len(cast_weights)),
        grid=(nb, N_HEADS),
        in_specs=[pl.BlockSpec(memory_space=pltpu.SMEM), vec, vec, vec, vec,
                  pl.BlockSpec((1, D_HEAD), lambda b, h: (0, 0)),
                  head_cols(0), head_cols(COL_K // D_HEAD), head_cols(COL_V // D_HEAD), tile, tile] + cast_in,
        out_specs=[pl.BlockSpec((None, seq, D_HEAD), lambda b, h: (b, 0, h))] + cast_out,
        out_shape=[jax.ShapeDtypeStruct((nb, seq, WIDTH_A), BF16)] + cast_shapes,
        scratch_shapes=[stat, acc, stat, acc] + [pltpu.VMEM((tq, tq), F32)] * 4,
        compiler_params=_params(2),
        name="attn_prompt",
    )(rel_bias, *lam_vecs, subln_g.reshape(1, D_HEAD), qkv3, qkv3, qkv3, bias_near, bias_diag, *cast_weights)
    return o.reshape(nb * seq, WIDTH_A), cast


def _softmax_two_blocks(qm, k_a, v_a, bias_a, k_b, v_b, bias_b):
    s_a = _scores(qm, k_a) + bias_a
    s_b = _scores(qm, k_b) + bias_b
    m = jnp.maximum(jnp.max(s_a, axis=-1, keepdims=True), jnp.max(s_b, axis=-1, keepdims=True))
    p_a = jnp.exp(s_a - m).astype(BF16)
    p_b = jnp.exp(s_b - m).astype(BF16)
    return jnp.dot(p_a, v_a, preferred_element_type=F32) + jnp.dot(p_b, v_b, preferred_element_type=F32)


def _attn_sample_kernel(lambda_init, past, lq1_ref, lk1_ref, lq2_ref, lk2_ref, g_ref,
                        qkv_ref, kc_ref, vc_ref, bc_ref, bn_ref, o_ref):
    lam = _lambda(lq1_ref, lk1_ref, lq2_ref, lk2_ref, lambda_init)
    for h in range(N_HEADS):
        cols = lambda part: slice((part * N_HEADS + h) * D_HEAD, (part * N_HEADS + h + 1) * D_HEAD)
        q1, q2 = _split_maps(qkv_ref[:, cols(0)])
        kn = qkv_ref[:, cols(1)]
        vn = _with_ones(qkv_ref[:, cols(2)])
        kc = kc_ref[pl.ds(h, past, stride=N_HEADS), :].astype(BF16)
        vc = _with_ones(vc_ref[pl.ds(h, past, stride=N_HEADS), :].astype(BF16))
        a1 = _softmax_two_blocks(q1, kc, vc, bc_ref[h], kn, vn, bn_ref[h])
        a2 = _softmax_two_blocks(q2, kc, vc, bc_ref[h], kn, vn, bn_ref[h])
        o = a1[:, :D_HEAD] / a1[:, D_HEAD:] - lam * (a2[:, :D_HEAD] / a2[:, D_HEAD:])
        o_ref[:, h * D_HEAD:(h + 1) * D_HEAD] = (_rms(o, g_ref[...]) * (1.0 - lambda_init)).astype(BF16)


def _attn_sample(proj, cache_k, cache_v, layer, lam_vecs, subln_g, bias_cache, bias_new, nb, seq, past,
                 lambda_init):
    qkv3 = proj.reshape(nb, seq, IN_WIDTH)
    kc = cache_k.reshape(-1, past * N_HEADS, D_HEAD)
    vc = cache_v.reshape(-1, past * N_HEADS, D_HEAD)
    vec = pl.BlockSpec((1, D_QK), lambda b: (0, 0))
    past_rows = pl.BlockSpec((None, past * N_HEADS, D_HEAD), lambda b: (layer * nb + b, 0, 0))
    full = lambda a: pl.BlockSpec(a.shape, lambda b: (0,) * a.ndim)
    o = pl.pallas_call(
        functools.partial(_attn_sample_kernel, lambda_init, past),
        grid=(nb,),
        in_specs=[vec, vec, vec, vec, pl.BlockSpec((1, D_HEAD), lambda b: (0, 0)),
                  pl.BlockSpec((None, seq, 3 * WIDTH_A), lambda b: (b, 0, 0)), past_rows, past_rows,
                  full(bias_cache), full(bias_new)],
        out_specs=pl.BlockSpec((None, seq, WIDTH_A), lambda b: (b, 0, 0)),
        out_shape=jax.ShapeDtypeStruct((nb, seq, WIDTH_A), BF16),
        compiler_params=_params(1),
        name="attn_sample",
    )(*lam_vecs, subln_g.reshape(1, D_HEAD), qkv3, kc, vc, bias_cache, bias_new)
    return o.reshape(nb * seq, WIDTH_A)


def _causal_conv3(x3, prev8, w):
    p0 = prev8[:, SUBLANE - 2:SUBLANE - 1, :]
    p1 = prev8[:, SUBLANE - 1:SUBLANE, :]
    row = lax.broadcasted_iota(jnp.int32, x3.shape, 1)
    x1 = jnp.where(row == 0, p1, pltpu.roll(x3, 1, axis=1))
    x2 = jnp.where(row == 0, p0, jnp.where(row == 1, p1, pltpu.roll(x3, 2, axis=1)))
    return x2 * w[0:1, :] + x1 * w[1:2, :] + x3 * w[2:3, :]


def _conv_with_state(x, i, nseq, tps, st_ref, so_ref, halo):
    rows = x.shape[0] // nseq
    x3 = x.reshape(nseq, rows, x.shape[1])
    last8 = x3[:, rows - SUBLANE:, :]
    if tps == 1:
        prev8 = st_ref[...]
    else:
        @pl.when(i % tps == 0)
        def _():
            halo[...] = st_ref[...]

        prev8 = halo[...]
        halo[...] = last8
    so_ref[...] = last8
    return x3, prev8


def _merge_kernel(nseq, tps, n_cast, o_ref, b_ref, c_ref, xi_ref, ga_ref, gb_ref, x_ref, st_ref, cw_ref,
                  wa_ref, wb_ref, wo_ref, g2_ref, *rest):
    cast_in, (h_ref, hn_ref, so_ref) = rest[:n_cast], rest[n_cast:n_cast + 3]
    cast_out, scratch = rest[n_cast + 3:2 * n_cast + 3], rest[2 * n_cast + 3:]
    _cast_slabs(cast_in, cast_out)
    i = pl.program_id(0)
    u = c_ref[...].astype(F32) * xi_ref[...].astype(F32)
    u3, prev8 = _conv_with_state(u, i, nseq, tps, st_ref, so_ref, scratch[0] if tps > 1 else None)
    z = _causal_conv3(u3, prev8, cw_ref[...]).reshape(u.shape)
    ob = (b_ref[...].astype(F32) * z).astype(BF16)
    ya = jnp.dot(o_ref[...], wa_ref[...], preferred_element_type=F32)
    yb = jnp.dot(ob, wb_ref[...], preferred_element_type=F32)
    merged = jax.nn.sigmoid(ga_ref[...].astype(F32)) * ya + jax.nn.sigmoid(gb_ref[...].astype(F32)) * yb
    hres = x_ref[...] + jnp.dot(merged.astype(BF16), wo_ref[...], preferred_element_type=F32)
    h_ref[...] = hres
    hn_ref[...] = _rms(hres, g2_ref[...]).astype(BF16)


def _merge(o, proj, x2d, state8, conv_w, wa, wb, wo, norm2_g, nb, seq, rows, cast_weights=()):
    t = x2d.shape[0]
    nseq = max(rows // seq, 1)
    tps = max(seq // rows, 1)
    assert t % rows == 0 and (rows % seq == 0 or seq % rows == 0)
    tok = lambda width, col: pl.BlockSpec((rows, width), lambda i: (i, col))
    full = lambda a: pl.BlockSpec(a.shape, lambda i: (0,) * a.ndim)
    state_spec = pl.BlockSpec((nseq, SUBLANE, WIDTH_B), lambda i: (i // tps, 0, 0))
    scratch = [pltpu.VMEM((nseq, SUBLANE, WIDTH_B), F32)] if tps > 1 else []
    g2 = norm2_g.reshape(1, D_MODEL)
    cast_in, cast_out, cast_shapes = _cast_slab_specs(cast_weights, t // rows, lambda i: i)
    hres, hn, state_rows, *cast = pl.pallas_call(
        functools.partial(_merge_kernel, nseq, tps, len(cast_weights)),
        grid=(t // rows,),
        in_specs=[tok(WIDTH_A, 0), tok(WIDTH_B, COL_B // WIDTH_B), tok(WIDTH_B, COL_C // WIDTH_B),
                  tok(WIDTH_B, COL_X // WIDTH_B), tok(D_MODEL, COL_GA // D_MODEL), tok(D_MODEL, COL_GB // D_MODEL),
                  tok(D_MODEL, 0), state_spec,
                  full(conv_w), full(wa), full(wb), full(wo), full(g2)] + cast_in,
        out_specs=[tok(D_MODEL, 0), tok(D_MODEL, 0), state_spec] + cast_out,
        out_shape=[jax.ShapeDtypeStruct((t, D_MODEL), F32),
                   jax.ShapeDtypeStruct((t, D_MODEL), BF16),
                   jax.ShapeDtypeStruct((nb, SUBLANE, WIDTH_B), F32)] + cast_shapes,
        scratch_shapes=scratch,
        compiler_params=_params(1),
        name="merge",
    )(o, proj, proj, proj, proj, proj, x2d, state8, conv_w, wa, wb, wo, g2, *cast_weights)
    return hres, hn, state_rows, cast


FFN_TILE_F = 512
FFN_SUB_TILES = 2


def _ffn_kernel(nseq, tps, nf, hn_ref, h_ref, wg_ref, wv_ref, cwg_ref, cwv_ref, stg_ref, stv_ref,
                wd_ref, fg_ref, y_ref, sog_ref, sov_ref, act_ref, *scratch):
    s = pl.program_id(0)
    i = lax.div(s, nf)
    f = lax.rem(s, nf)
    f_down = lax.rem(jnp.maximum(s - 1, 0), nf)
    drain = s == pl.num_programs(0) - 1
    act_new = act_ref.at[lax.rem(s, 2)]
    act_old = act_ref.at[1 - lax.rem(s, 2)]
    rows = hn_ref.shape[0]
    sub = rows // FFN_SUB_TILES
    branches = ((wg_ref, cwg_ref, stg_ref, sog_ref), (wv_ref, cwv_ref, stv_ref, sov_ref))

    @pl.when(s == 0)
    def _():
        act_ref[...] = jnp.zeros_like(act_ref)

    @pl.when(f_down == 0)
    def _():
        y_ref[...] = jnp.zeros_like(y_ref)

    if tps > 1:
        halos = [scratch[0].at[2 * f + slot] for slot in range(2)]

        @pl.when(lax.rem(i, tps) == 0)
        def _():
            for halo, (_, _, st_ref, _) in zip(halos, branches):
                halo[...] = st_ref[...]

        prev = [halo[...] for halo in halos]
    else:
        assert sub % (rows // nseq) == 0
        seqs = nseq // FFN_SUB_TILES

    for r in range(FFN_SUB_TILES):
        tile_rows = slice(r * sub, (r + 1) * sub)
        y_ref[tile_rows, :] += jnp.dot(act_old[tile_rows, :], wd_ref[...], preferred_element_type=F32)
        hn = hn_ref[tile_rows, :]
        convs = []
        for slot, (w_ref, cw_ref, st_ref, so_ref) in enumerate(branches):
            up = jnp.dot(hn, w_ref[...], preferred_element_type=F32)
            if tps > 1:
                up3 = up.reshape(1, sub, up.shape[1])
                prev8 = prev[slot]
                prev[slot] = up3[:, sub - SUBLANE:, :]
            else:
                up3 = up.reshape(seqs, sub // seqs, up.shape[1])
                prev8 = st_ref[r * seqs:(r + 1) * seqs]
                so_rows = so_ref.at[r * seqs:(r + 1) * seqs]
                so_rows[...] = jnp.where(drain, so_rows[...], up3[:, sub // seqs - SUBLANE:, :])
            convs.append(_causal_conv3(up3, prev8, cw_ref[...]).reshape(up.shape))
        act_new[tile_rows, :] = (jax.nn.silu(convs[0]) * convs[1]).astype(BF16)

    if tps > 1:
        for halo, last8, (_, _, _, so_ref) in zip(halos, prev, branches):
            halo[...] = last8
            so_ref[...] = jnp.where(drain, so_ref[...], last8)

    @pl.when(jnp.logical_and(f_down == nf - 1, s > 0))
    def _():
        y_ref[...] = _rms(h_ref[...] + y_ref[...], fg_ref[...])


def _ffn(hn, hres, w_up, ffn_conv_w, state8, w_down, final_g, nb, seq, rows):
    t = hn.shape[0]
    tf = FFN_TILE_F
    nf = D_FF // tf
    nseq = max(rows // seq, 1)
    tps = max(seq // rows, 1)
    assert D_FF % tf == 0 and t % rows == 0 and (rows % seq == 0 or seq % rows == 0)
    n_tiles = t // rows
    up_i = lambda s: jnp.minimum(s // nf, n_tiles - 1)
    up_f = lambda s: s % nf
    down_i = lambda s: jnp.maximum(s - 1, 0) // nf
    down_f = lambda s: jnp.maximum(s - 1, 0) % nf
    gate_cols = lambda r: pl.BlockSpec((r, tf), lambda s: (0, up_f(s)))
    val_cols = lambda r: pl.BlockSpec((r, tf), lambda s: (0, nf + up_f(s)))
    st_gate = pl.BlockSpec((nseq, SUBLANE, tf), lambda s: (up_i(s) // tps, 0, up_f(s)))
    st_val = pl.BlockSpec((nseq, SUBLANE, tf), lambda s: (up_i(s) // tps, 0, nf + up_f(s)))
    resid = pl.BlockSpec((rows, D_MODEL), lambda s: (down_i(s), 0), pipeline_mode=pl.Buffered(1))
    scratch = [pltpu.VMEM((2, rows, tf), BF16)]
    if tps > 1:
        scratch.append(pltpu.VMEM((2 * nf, nseq, SUBLANE, tf), F32))
    n_steps = n_tiles * nf + 1
    tile_tail = pl.BlockSpec(
        (nseq, SUBLANE, tf), lambda s: (up_i(s), 0, jnp.where(s == n_steps - 1, nf - 1, up_f(s))))
    fg = final_g.reshape(1, D_MODEL)
    y, tail_g, tail_v = pl.pallas_call(
        functools.partial(_ffn_kernel, nseq, tps, nf),
        grid=(n_tiles * nf + 1,),
        in_specs=[pl.BlockSpec((rows, D_MODEL), lambda s: (up_i(s), 0)), resid,
                  gate_cols(D_MODEL), val_cols(D_MODEL), gate_cols(CONV_W), val_cols(CONV_W),
                  st_gate, st_val,
                  pl.BlockSpec((tf, D_MODEL), lambda s: (down_f(s), 0)),
                  pl.BlockSpec((1, D_MODEL), lambda s: (0, 0))],
        out_specs=[pl.BlockSpec((rows, D_MODEL), lambda s: (down_i(s), 0)), tile_tail, tile_tail],
        out_shape=[jax.ShapeDtypeStruct((t, D_MODEL), F32),
                   jax.ShapeDtypeStruct((n_tiles * nseq, SUBLANE, D_FF), F32),
                   jax.ShapeDtypeStruct((n_tiles * nseq, SUBLANE, D_FF), F32)],
        scratch_shapes=scratch,
        compiler_params=_params(1),
        name="ffn",
    )(hn, hres, w_up, w_up, ffn_conv_w, ffn_conv_w, state8, state8, w_down, fg)
    last_of_seq = slice(tps - 1, n_tiles * nseq, tps)
    return y, tail_g[last_of_seq], tail_v[last_of_seq]


def _state8(state):
    return jnp.pad(state, ((0, 0), (SUBLANE - (CONV_W - 1), 0), (0, 0)))


def _state_out(state8):
    return state8[:, SUBLANE - (CONV_W - 1):, :]


def _layer(x2d, nb, seq, attn_fn, conv8, ffn8, lp, tiles, cast_in_attn=(), cast_in_merge=()):
    lp = dict(lp)
    proj, k_f32, v_f32 = _in_proj(x2d, lp["norm1_g"], lp["w_in"], tiles["in"])
    o, done = attn_fn(proj, [lp[name] for name in cast_in_attn])
    lp.update(zip(cast_in_attn, done))
    hres, hn, conv_out8, done = _merge(o, proj, x2d, conv8, lp["conv_w"], lp["w_proj_a"], lp["w_proj_b"],
                                       lp["w_o"], lp["norm2_g"], nb, seq, tiles["merge"],
                                       [lp[name] for name in cast_in_merge])
    lp.update(zip(cast_in_merge, done))
    y, ffn_g8, ffn_v8 = _ffn(hn, hres, lp["w_up"], lp["ffn_conv_w"], ffn8, lp["w_down"], lp["final_g"],
                             nb, seq, tiles["ffn"])
    k = k_f32.reshape(nb, seq, N_HEADS, D_HEAD)
    v = v_f32.reshape(nb, seq, N_HEADS, D_HEAD)
    conv_state = _state_out(conv_out8)
    ffn_state = jnp.concatenate([_state_out(ffn_g8), _state_out(ffn_v8)], axis=-1)
    return (y, k, v, conv_state, ffn_state), lp


PROMPT_TILES = {"in": 512, "merge": 256, "ffn": 1024}
SAMPLE_TILES = {"in": 512, "merge": 256, "ffn": 1024}
ATTN_TILE_Q = 512


def kernel(x_prompt, x_sample, cache_k, cache_v, state_conv_mix, state_conv_ffn, rel_bias, norm1_g, w_in,
           lambda_q1, lambda_k1, lambda_q2, lambda_k2, subln_g, conv_w, w_proj_a, w_proj_b, w_o, norm2_g,
           w_up, ffn_conv_w, w_down, final_g):
    depth = w_in.shape[0]
    nbp, seq_p, _ = x_prompt.shape
    nbs, seq_s, _ = x_sample.shape
    past = cache_k.shape[2]
    tq = ATTN_TILE_Q

    assert depth == 1, "the ffn kernel ends with the final RMSNorm, so it is written for a single layer"
    (bias_near, bias_diag, bias_cache, bias_new), (w_in_bf,) = _bias_tiles(
        rel_bias,
        ((tq, tq, tq, 0), (tq, tq, tq, tq), (seq_s, past, past, 0), (seq_s, seq_s, past, past)),
        cast_weights=(w_in[0],))

    hp = x_prompt.reshape(nbp * seq_p, D_MODEL)
    hs = x_sample.reshape(nbs * seq_s, D_MODEL)
    outs_p, outs_s = [], []
    for d in range(depth):
        lp = {
            "norm1_g": norm1_g[d], "w_in": w_in_bf, "conv_w": conv_w[d],
            "w_proj_a": w_proj_a[d], "w_proj_b": w_proj_b[d], "w_o": w_o[d], "norm2_g": norm2_g[d],
            "w_up": w_up[d], "ffn_conv_w": ffn_conv_w[d], "w_down": w_down[d], "final_g": final_g,
        }
        li = _lambda_init(d)
        lam_vecs = tuple(a[d].reshape(1, D_QK) for a in (lambda_q1, lambda_k1, lambda_q2, lambda_k2))

        attn_p = lambda qkv, cast: _attn_prompt(qkv, rel_bias, lam_vecs, subln_g[d], bias_near, bias_diag,
                                                nbp, seq_p, tq, li, cast)
        zeros_mix = jnp.zeros((nbp, SUBLANE, WIDTH_B), F32)
        zeros_ffn = jnp.zeros((nbp, SUBLANE, 2 * D_FF), F32)
        (hp, kp, vp, cmp_, cfp), lp = _layer(
            hp, nbp, seq_p, attn_p, zeros_mix, zeros_ffn, lp, PROMPT_TILES,
            cast_in_attn=("w_proj_a", "w_proj_b", "w_o", "w_down"), cast_in_merge=("w_up",))

        attn_s = lambda qkv, cast: (_attn_sample(qkv, cache_k, cache_v, d, lam_vecs, subln_g[d], bias_cache,
                                                 bias_new, nbs, seq_s, past, li), ())
        (hs, ks, vs, cms, cfs), _ = _layer(hs, nbs, seq_s, attn_s, _state8(state_conv_mix[d]),
                                           _state8(state_conv_ffn[d]), lp, SAMPLE_TILES)
        outs_p.append((kp, vp, cmp_, cfp))
        outs_s.append((ks, vs, cms, cfs))

    stack = lambda outs, idx: jnp.stack([o[idx] for o in outs])
    return (hp.reshape(nbp, seq_p, D_MODEL), hs.reshape(nbs, seq_s, D_MODEL),
            stack(outs_p, 0), stack(outs_p, 1), stack(outs_p, 2), stack(outs_p, 3),
            stack(outs_s, 0), stack(outs_s, 1), stack(outs_s, 2), stack(outs_s, 3))
```
